```python
import jax, jax.numpy as jnp
from jax import lax
import numpy as np

D_MODEL = 2048
BATCH = 4
SEQ = 2048
DEPTH = 4

EPS = 1e-6
MLA_HEADS = 8
MLA_Q_RANK = 512
MLA_KV_RANK = 256
MLA_NOPE = 128
MLA_ROPE = 64
MLA_V = 128
MLA_OUT = MLA_HEADS * MLA_V
ROPE_BASE = 10000.0
Q_BLOCK = 128
SSD_HEADS = 8
SSD_HEAD_DIM = 64
SSD_GROUPS = 2
SSD_STATE = 128
SSD_CONV = 4
SSD_CHUNK = 128
SSD_INNER = SSD_HEADS * SSD_HEAD_DIM
SSD_CONV_DIM = SSD_INNER + 2 * SSD_GROUPS * SSD_STATE
GLA_HEADS = 4
GLA_HEAD_K = 64
GLA_HEAD_V = 128
GLA_GATE_RANK = 16
GLA_GATE_TAU = 16.0
GLA_CHUNK = 64
GLA_KD = GLA_HEADS * GLA_HEAD_K
GLA_VD = GLA_HEADS * GLA_HEAD_V
D_FF = 5632
FFN_CONV = 3
D_MIX = MLA_OUT + SSD_INNER + GLA_VD
IN_SIZES = (MLA_Q_RANK, MLA_KV_RANK, MLA_ROPE,
            SSD_INNER, SSD_CONV_DIM, SSD_HEADS,
            GLA_KD, GLA_KD, GLA_VD, GLA_GATE_RANK, GLA_VD)
D_IN = MLA_Q_RANK + MLA_KV_RANK + MLA_ROPE + SSD_INNER + SSD_CONV_DIM + SSD_HEADS + 2 * GLA_KD + 2 * GLA_VD + GLA_GATE_RANK

kernel_name = "hybrid_mla_ssd_gla_convffn"


def rms_norm(x, g):
    xf = x.astype(jnp.float32)
    var = jnp.mean(xf * xf, axis=-1, keepdims=True)
    return (xf * lax.rsqrt(var + EPS)).astype(x.dtype) * g


def rope_tables(seq, dim, dtype):
    inv_freq = ROPE_BASE ** (-jnp.arange(0, dim, 2, dtype=jnp.float32) / dim)
    ang = jnp.arange(seq, dtype=jnp.float32)[:, None] * inv_freq[None, :]
    return jnp.cos(ang).astype(dtype), jnp.sin(ang).astype(dtype)


def apply_rope(x, cos, sin):
    half = x.shape[-1] // 2
    x1, x2 = x[..., :half], x[..., half:]
    return jnp.concatenate([x1 * cos - x2 * sin, x1 * sin + x2 * cos], axis=-1)


def causal_dwconv(x, w, b):
    k, c = w.shape
    out = lax.conv_general_dilated(
        x, w[:, None, :].astype(x.dtype), window_strides=(1,), padding=[(k - 1, 0)],
        dimension_numbers=("NWC", "WIO", "NWC"), feature_group_count=c)
    return out + b


def mla_mixer(cq_raw, ckv_raw, kpe_raw, q_norm, w_uq, kv_norm, w_ukv, cos, sin):
    b, s, _ = cq_raw.shape
    q = (rms_norm(cq_raw, q_norm) @ w_uq).reshape(b, s, MLA_HEADS, MLA_NOPE + MLA_ROPE)
    q_nope = q[..., :MLA_NOPE]
    q_pe = apply_rope(q[..., MLA_NOPE:], cos[:, None, :], sin[:, None, :])
    kv = (rms_norm(ckv_raw, kv_norm) @ w_ukv).reshape(b, s, MLA_HEADS, MLA_NOPE + MLA_V)
    k_nope, v = kv[..., :MLA_NOPE], kv[..., MLA_NOPE:]
    k_pe = apply_rope(kpe_raw, cos, sin)
    scale = (MLA_NOPE + MLA_ROPE) ** -0.5
    nb = s // Q_BLOCK
    qn_blocks = q_nope.reshape(b, nb, Q_BLOCK, MLA_HEADS, MLA_NOPE).transpose(1, 0, 2, 3, 4)
    qp_blocks = q_pe.reshape(b, nb, Q_BLOCK, MLA_HEADS, MLA_ROPE).transpose(1, 0, 2, 3, 4)
    k_pos = jnp.arange(s)

    def attend(args):
        qn, qp, blk = args
        scores = (jnp.einsum("bqhd,bkhd->bhqk", qn, k_nope)
                  + jnp.einsum("bqhr,bkr->bhqk", qp, k_pe)).astype(jnp.float32) * scale
        q_pos = blk * Q_BLOCK + jnp.arange(Q_BLOCK)
        scores = jnp.where(k_pos[None, :] <= q_pos[:, None], scores, -jnp.inf)
        p = jax.nn.softmax(scores, axis=-1).astype(v.dtype)
        return jnp.einsum("bhqk,bkhd->bqhd", p, v)

    out = lax.map(attend, (qn_blocks, qp_blocks, jnp.arange(nb)))
    return out.transpose(1, 0, 2, 3, 4).reshape(b, s, MLA_OUT)


def ssd_mixer(z, xbc, dt_raw, conv_w, conv_b, dt_bias, a_log, d_skip, norm_g):
    b, s, _ = z.shape
    H, P, N, L = SSD_HEADS, SSD_HEAD_DIM, SSD_STATE, SSD_CHUNK
    xbc = jax.nn.silu(causal_dwconv(xbc, conv_w, conv_b))
    xs, bm, cm = jnp.split(xbc, [SSD_INNER, SSD_INNER + SSD_GROUPS * SSD_STATE], axis=-1)
    xs = xs.reshape(b, s, H, P)
    rep = H // SSD_GROUPS
    bm = jnp.repeat(bm.reshape(b, s, SSD_GROUPS, N), rep, axis=2)
    cm = jnp.repeat(cm.reshape(b, s, SSD_GROUPS, N), rep, axis=2)
    dt = jax.nn.softplus((dt_raw + dt_bias).astype(jnp.float32))
    a_dt = dt * (-jnp.exp(a_log.astype(jnp.float32)))
    x_dt = xs * dt[..., None].astype(xs.dtype)
    nc = s // L
    xc = x_dt.reshape(b, nc, L, H, P)
    bc = bm.reshape(b, nc, L, H, N)
    cc = cm.reshape(b, nc, L, H, N)
    a_cum = jnp.cumsum(a_dt.reshape(b, nc, L, H).transpose(0, 3, 1, 2), axis=-1)
    seg = a_cum[..., :, None] - a_cum[..., None, :]
    causal = jnp.tril(jnp.ones((L, L), dtype=bool))
    decay = jnp.exp(jnp.where(causal, seg, -jnp.inf)).astype(cc.dtype)
    scores = jnp.einsum("bclhn,bcshn->bhcls", cc, bc) * decay
    y_diag = jnp.einsum("bhcls,bcshp->bclhp", scores, xc)
    decay_states = jnp.exp(a_cum[..., -1:] - a_cum).astype(bc.dtype)
    states = jnp.einsum("bclhn,bhcl,bclhp->bchpn", bc, decay_states, xc)
    chunk_decay = jnp.exp(a_cum[..., -1]).astype(states.dtype)

    def step(h, inp):
        st, dec = inp
        return dec[..., None, None] * h + st, h

    h0 = jnp.zeros((b, H, P, N), states.dtype)
    _, prev = lax.scan(step, h0, (states.transpose(1, 0, 2, 3, 4), chunk_decay.transpose(2, 0, 1)))
    prev = prev.transpose(1, 0, 2, 3, 4)
    y_off = jnp.einsum("bclhn,bchpn,bhcl->bclhp", cc, prev, jnp.exp(a_cum).astype(cc.dtype))
    y = (y_diag + y_off).reshape(b, s, H, P) + xs * d_skip[:, None]
    y = y.reshape(b, s, SSD_INNER)
    return rms_norm(y * jax.nn.silu(z), norm_g)


def gla_mixer(q, k, v, gate_low, r, w_gk, b_gk, norm_g):
    b, s, _ = q.shape
    H, K, V, L = GLA_HEADS, GLA_HEAD_K, GLA_HEAD_V, GLA_CHUNK
    nc = s // L
    log_a = jax.nn.log_sigmoid((gate_low @ w_gk + b_gk).astype(jnp.float32)) / GLA_GATE_TAU
    g = jnp.cumsum(log_a.reshape(b, nc, L, H, K), axis=2)
    qc = (q * K ** -0.5).reshape(b, nc, L, H, K).transpose(1, 0, 2, 3, 4)
    kc = k.reshape(b, nc, L, H, K).transpose(1, 0, 2, 3, 4)
    vc = v.reshape(b, nc, L, H, V).transpose(1, 0, 2, 3, 4)
    gc = g.transpose(1, 0, 2, 3, 4)
    causal = jnp.tril(jnp.ones((L, L), dtype=bool))[:, :, None, None]

    def chunk_step(state, inp):
        qb, kb, vb, gb = inp
        rel = gb[:, :, None] - gb[:, None, :]
        rel = jnp.exp(jnp.where(causal, rel, -jnp.inf)).astype(qb.dtype)
        attn = jnp.einsum("blhd,bshd,blshd->bhls", qb, kb, rel)
        o_intra = jnp.einsum("bhls,bshv->blhv", attn, vb)
        o_inter = jnp.einsum("blhd,bhdv->blhv", qb * jnp.exp(gb).astype(qb.dtype), state)
        g_last = gb[:, -1]
        k_dec = kb * jnp.exp(g_last[:, None] - gb).astype(kb.dtype)
        new_state = (jnp.exp(g_last).astype(state.dtype)[..., None] * state
                     + jnp.einsum("bshd,bshv->bhdv", k_dec, vb))
        return new_state, o_intra + o_inter

    s0 = jnp.zeros((b, H, K, V), v.dtype)
    _, o = lax.scan(chunk_step, s0, (qc, kc, vc, gc))
    o = o.transpose(1, 0, 2, 3, 4).reshape(b, s, H, V)
    o = rms_norm(o, norm_g).reshape(b, s, GLA_VD)
    return o * jax.nn.silu(r)


def hybrid_layer(x, cos, sin, attn_norm, w_in, mla_q_norm, mla_w_uq, mla_kv_norm, mla_w_ukv,
                 ssd_conv_w, ssd_conv_b, ssd_dt_bias, ssd_a_log, ssd_d, ssd_norm,
                 gla_w_gk, gla_b_gk, gla_norm, w_out,
                 ffn_norm, ffn_w_gate, ffn_w_up, ffn_dw_w, ffn_dw_b, ffn_w_down):
    h = rms_norm(x, attn_norm)
    proj = h @ w_in
    points = np.cumsum(IN_SIZES)[:-1].tolist()
    cq, ckv, kpe, z, xbc, dt_raw, gq, gk, gv, glow, gr = jnp.split(proj, points, axis=-1)
    y_mla = mla_mixer(cq, ckv, kpe, mla_q_norm, mla_w_uq, mla_kv_norm, mla_w_ukv, cos, sin)
    y_ssd = ssd_mixer(z, xbc, dt_raw, ssd_conv_w, ssd_conv_b, ssd_dt_bias, ssd_a_log, ssd_d, ssd_norm)
    y_gla = gla_mixer(gq, gk, gv, glow, gr, gla_w_gk, gla_b_gk, gla_norm)
    x = x + jnp.concatenate([y_mla, y_ssd, y_gla], axis=-1) @ w_out
    h = rms_norm(x, ffn_norm)
    gate = causal_dwconv(h @ ffn_w_gate, ffn_dw_w, ffn_dw_b)
    return x + (jax.nn.silu(gate) * (h @ ffn_w_up)) @ ffn_w_down


def setup_inputs(seed: int = 0) -> dict:
    key = jax.random.key(seed)
    ks = jax.random.split(key, 24)

    def nrm(k, shape, scale):
        return jax.random.normal(k, shape, jnp.float32) * scale

    def gain(k, shape):
        return 1.0 + 0.02 * jax.random.normal(k, shape, jnp.float32)

    res_scale = (2 * DEPTH) ** -0.5
    dt_init = jnp.exp(jax.random.uniform(ks[9], (DEPTH, SSD_HEADS), jnp.float32,
                                         np.log(1e-3).astype(np.float32), np.log(1e-1).astype(np.float32)))
    return {
        "x": nrm(ks[0], (BATCH, SEQ, D_MODEL), 1.0),
        "attn_norm": gain(ks[1], (DEPTH, D_MODEL)),
        "w_in": nrm(ks[2], (DEPTH, D_MODEL, D_IN), D_MODEL ** -0.5),
        "mla_q_norm": gain(ks[3], (DEPTH, MLA_Q_RANK)),
        "mla_w_uq": nrm(ks[4], (DEPTH, MLA_Q_RANK, MLA_HEADS * (MLA_NOPE + MLA_ROPE)), MLA_Q_RANK ** -0.5),
        "mla_kv_norm": gain(ks[5], (DEPTH, MLA_KV_RANK)),
        "mla_w_ukv": nrm(ks[6], (DEPTH, MLA_KV_RANK, MLA_HEADS * (MLA_NOPE + MLA_V)), MLA_KV_RANK ** -0.5),
        "ssd_conv_w": nrm(ks[7], (DEPTH, SSD_CONV, SSD_CONV_DIM), SSD_CONV ** -0.5),
        "ssd_conv_b": nrm(ks[8], (DEPTH, SSD_CONV_DIM), 0.02),
        "ssd_dt_bias": dt_init + jnp.log(-jnp.expm1(-dt_init)),
        "ssd_a_log": jnp.log(jax.random.uniform(ks[10], (DEPTH, SSD_HEADS), jnp.float32, 1.0, 16.0)),
        "ssd_d": gain(ks[11], (DEPTH, SSD_HEADS)),
        "ssd_norm": gain(ks[12], (DEPTH, SSD_INNER)),
        "gla_w_gk": nrm(ks[13], (DEPTH, GLA_GATE_RANK, GLA_KD), GLA_GATE_RANK ** -0.5),
        "gla_b_gk": nrm(ks[14], (DEPTH, GLA_KD), 0.02),
        "gla_norm": gain(ks[15], (DEPTH, GLA_HEAD_V)),
        "w_out": nrm(ks[16], (DEPTH, D_MIX, D_MODEL), D_MIX ** -0.5 * res_scale),
        "ffn_norm": gain(ks[17], (DEPTH, D_MODEL)),
        "ffn_w_gate": nrm(ks[18], (DEPTH, D_MODEL, D_FF), D_MODEL ** -0.5),
        "ffn_w_up": nrm(ks[19], (DEPTH, D_MODEL, D_FF), D_MODEL ** -0.5),
        "ffn_dw_w": nrm(ks[20], (DEPTH, FFN_CONV, D_FF), FFN_CONV ** -0.5),
        "ffn_dw_b": nrm(ks[21], (DEPTH, D_FF), 0.02),
        "ffn_w_down": nrm(ks[22], (DEPTH, D_FF, D_MODEL), D_FF ** -0.5 * res_scale),
        "final_norm": gain(ks[23], (D_MODEL,)),
    }


def reference(x, attn_norm, w_in, mla_q_norm, mla_w_uq, mla_kv_norm, mla_w_ukv,
              ssd_conv_w, ssd_conv_b, ssd_dt_bias, ssd_a_log, ssd_d, ssd_norm,
              gla_w_gk, gla_b_gk, gla_norm, w_out,
              ffn_norm, ffn_w_gate, ffn_w_up, ffn_dw_w, ffn_dw_b, ffn_w_down, final_norm):
    cos, sin = rope_tables(x.shape[1], MLA_ROPE, x.dtype)
    for i in range(DEPTH):
        x = hybrid_layer(x, cos, sin, attn_norm[i], w_in[i], mla_q_norm[i], mla_w_uq[i],
                         mla_kv_norm[i], mla_w_ukv[i], ssd_conv_w[i], ssd_conv_b[i],
                         ssd_dt_bias[i], ssd_a_log[i], ssd_d[i], ssd_norm[i],
                         gla_w_gk[i], gla_b_gk[i], gla_norm[i], w_out[i],
                         ffn_norm[i], ffn_w_gate[i], ffn_w_up[i], ffn_dw_w[i], ffn_dw_b[i],
                         ffn_w_down[i])
    return rms_norm(x, final_norm)
```

```python
import functools

import jax
import jax.numpy as jnp
from jax import lax
from jax.experimental import pallas as pl
from jax.experimental.pallas import tpu as pltpu

F32 = jnp.float32
BF16 = jnp.bfloat16
HIGHEST = lax.Precision.HIGHEST

D_MODEL = 2048
DEPTH = 4
EPS = 1e-6
MLA_HEADS = 8
MLA_Q_RANK = 512
MLA_KV_RANK = 256
MLA_NOPE = 128
MLA_ROPE = 64
MLA_V = 128
MLA_QK = 256
ROPE_BASE = 10000.0
SSD_HEADS = 8
SSD_HEAD_DIM = 64
SSD_STATE = 128
SSD_CONV = 4
SSD_CHUNK = 128
SSD_INNER = 512
SSD_CONV_DIM = 1024
GLA_HEADS = 4
GLA_HEAD_K = 64
GLA_HEAD_V = 128
GLA_GATE_RANK = 16
GLA_GATE_TAU = 16.0
GLA_CHUNK = 64
GLA_SUB = 16
GLA_KD = 256
GLA_VD = 512
D_FF = 5632
FFN_CONV = 3
HALO = 16

P_WIDTH = 4096
C_CQ, C_Z, C_GV, C_GR, C_XBC, C_CKV, C_GQ, C_GK, C_KPE, C_SM = (
    0, 512, 1024, 1536, 2048, 3072, 3328, 3584, 3840, 3968)
SM_DT = 0
SM_GLOW = 8

VMEM_LIMIT = 52 * 1024 * 1024


def _cparams(sem):
    return pltpu.CompilerParams(dimension_semantics=sem, vmem_limit_bytes=VMEM_LIMIT)


def _rms(x, g):
    var = jnp.mean(x * x, axis=-1, keepdims=True)
    return x * lax.rsqrt(var + EPS) * g


def _softplus(x):
    return jnp.maximum(x, 0.0) + jnp.log(1.0 + jnp.exp(-jnp.abs(x)))


def _silu(x):
    return x * jax.nn.sigmoid(x)


def _dot(a, b):
    return jnp.dot(a, b, preferred_element_type=F32)


def _dot_nt(a, b):
    return lax.dot_general(a, b, (((1,), (1,)), ((), ())), preferred_element_type=F32)


def _dot_tn(a, b):
    return lax.dot_general(a, b, (((0,), (0,)), ((), ())), preferred_element_type=F32)


def _iota(shape, axis):
    return lax.broadcasted_iota(jnp.int32, shape, axis)


def _in_proj_body(x_ref, g_ref, w_ref, o_ref, h_scr):
    @pl.when(pl.program_id(1) == 0)
    def _():
        h_scr[...] = _rms(x_ref[...], g_ref[...]).astype(BF16)

    o_ref[...] = _dot(h_scr[...], w_ref[...])


def _in_proj(x2d, gains, w_all, layer, tm=1024, tn=1024):
    t, d = x2d.shape
    tm = min(tm, t)
    return pl.pallas_call(
        _in_proj_body,
        grid=(t // tm, P_WIDTH // tn),
        in_specs=[
            pl.BlockSpec((tm, d), lambda i, j: (i, 0)),
            pl.BlockSpec((None, 1, d), lambda i, j: (layer, 0, 0)),
            pl.BlockSpec((None, d, tn), lambda i, j: (layer, 0, j)),
        ],
        out_specs=pl.BlockSpec((tm, tn), lambda i, j: (i, j)),
        out_shape=jax.ShapeDtypeStruct((t, P_WIDTH), F32),
        scratch_shapes=[pltpu.VMEM((tm, d), BF16)],
        compiler_params=_cparams(("parallel", "arbitrary")),
        name="in_proj",
    )(x2d, gains, w_all)


def _rope(blk, cc, ss):
    return blk * cc + pltpu.roll(blk, MLA_ROPE // 2, axis=1) * ss


def _mla_proj_body(cq_ref, ckv_ref, kpe_ref, qn_ref, kvn_ref, wq_ref, wkv_ref, cc_ref, ss_ref,
                   q_out, k_out, v_out, hq_scr, hkv_scr, kpe_scr):
    @pl.when(pl.program_id(1) == 0)
    def _():
        hq_scr[...] = _rms(cq_ref[...], qn_ref[...]).astype(BF16)
        hkv_scr[...] = _rms(ckv_ref[...], kvn_ref[...]).astype(BF16)
        kpe_scr[...] = _rope(kpe_ref[...], cc_ref[...], ss_ref[...]).astype(BF16)

    scale = (MLA_NOPE + MLA_ROPE) ** -0.5
    qf = _dot(hq_scr[...], wq_ref[...])
    q_pe = _rope(qf[:, MLA_NOPE:], cc_ref[...], ss_ref[...])
    q_out[...] = (jnp.concatenate([qf[:, :MLA_NOPE], q_pe], axis=1) * scale).astype(BF16)
    kvf = _dot(hkv_scr[...], wkv_ref[...])
    k_out[...] = jnp.concatenate([kvf[:, :MLA_NOPE].astype(BF16), kpe_scr[...]], axis=1)
    v_out[...] = kvf[:, MLA_NOPE:].astype(BF16)


def _mla_proj(proj, q_norm, kv_norm, wq_all, wkv_all, cc, ss, layer, batch, seq, tm=512):
    t = proj.shape[0]
    tm = min(tm, seq)
    nsb = seq // tm
    hd = (None, None, tm, MLA_QK)
    out_idx = lambda i, h: (i // nsb, h, i % nsb, 0)
    return pl.pallas_call(
        _mla_proj_body,
        grid=(t // tm, MLA_HEADS),
        in_specs=[
            pl.BlockSpec((tm, MLA_Q_RANK), lambda i, h: (i, C_CQ // MLA_Q_RANK)),
            pl.BlockSpec((tm, MLA_KV_RANK), lambda i, h: (i, C_CKV // MLA_KV_RANK)),
            pl.BlockSpec((tm, 128), lambda i, h: (i, C_KPE // 128)),
            pl.BlockSpec((None, 1, MLA_Q_RANK), lambda i, h: (layer, 0, 0)),
            pl.BlockSpec((None, 1, MLA_KV_RANK), lambda i, h: (layer, 0, 0)),
            pl.BlockSpec((None, None, MLA_Q_RANK, MLA_QK), lambda i, h: (layer, h, 0, 0)),
            pl.BlockSpec((None, None, MLA_KV_RANK, MLA_NOPE + MLA_V), lambda i, h: (layer, h, 0, 0)),
            pl.BlockSpec((tm, 128), lambda i, h: (i % nsb, 0)),
            pl.BlockSpec((tm, 128), lambda i, h: (i % nsb, 0)),
        ],
        out_specs=[
            pl.BlockSpec(hd, out_idx),
            pl.BlockSpec(hd, out_idx),
            pl.BlockSpec((None, None, tm, MLA_V), out_idx),
        ],
        out_shape=[
            jax.ShapeDtypeStruct((batch, MLA_HEADS, seq, MLA_QK), BF16),
            jax.ShapeDtypeStruct((batch, MLA_HEADS, seq, MLA_QK), BF16),
            jax.ShapeDtypeStruct((batch, MLA_HEADS, seq, MLA_V), BF16),
        ],
        scratch_shapes=[
            pltpu.VMEM((tm, MLA_Q_RANK), BF16),
            pltpu.VMEM((tm, MLA_KV_RANK), BF16),
            pltpu.VMEM((tm, 128), BF16),
        ],
        compiler_params=_cparams(("parallel", "arbitrary")),
        name="mla_proj",
    )(proj, proj, proj, q_norm, kv_norm, wq_all, wkv_all, cc, ss)


def _attn_body(q_ref, k_ref, v_ref, o_ref, *, tq):
    qi = pl.program_id(2)
    q = q_ref[...]

    def step(kb, carry, diagonal):
        m, l, acc = carry
        start = pl.multiple_of(kb * tq, tq)
        s = _dot_nt(q, k_ref[pl.ds(start, tq), :])
        if diagonal:
            s = jnp.where(_iota((tq, tq), 0) >= _iota((tq, tq), 1), s, -jnp.inf)
        m_new = jnp.maximum(m, jnp.max(s, axis=-1, keepdims=True))
        alpha = jnp.exp(m - m_new)
        p = jnp.exp(s - m_new)
        l = alpha * l + jnp.sum(p, axis=-1, keepdims=True)
        acc = alpha * acc + _dot(p.astype(BF16), v_ref[pl.ds(start, tq), :])
        return m_new, l, acc

    init = (jnp.full((tq, 1), -jnp.inf, F32), jnp.zeros((tq, 1), F32), jnp.zeros((tq, MLA_V), F32))
    carry = lax.fori_loop(0, qi, lambda kb, c: step(kb, c, False), init)
    _, l, acc = step(qi, carry, True)
    o_ref[...] = (acc / l).astype(BF16)


def _mla_attn(q, k, v, tq=256):
    batch, heads, seq, _ = q.shape
    tq = min(tq, seq)
    return pl.pallas_call(
        functools.partial(_attn_body, tq=tq),
        grid=(batch, heads, seq // tq),
        in_specs=[
            pl.BlockSpec((None, None, tq, MLA_QK), lambda b, h, i: (b, h, i, 0)),
            pl.BlockSpec((None, None, seq, MLA_QK), lambda b, h, i: (b, h, 0, 0)),
            pl.BlockSpec((None, None, seq, MLA_V), lambda b, h, i: (b, h, 0, 0)),
        ],
        out_specs=pl.BlockSpec((None, tq, MLA_V), lambda b, h, i: (b, i, h)),
        out_shape=jax.ShapeDtypeStruct((batch, seq, heads * MLA_V), BF16),
        compiler_params=_cparams(("parallel", "parallel", "arbitrary")),
        name="mla_attn",
    )(q, k, v)


def _ssd_body(z_ref, xbc_ref, sm_ref, cw_ref, cb_ref, dtb_ref, dtbt_ref, alog_ref, alogt_ref,
              dexp_ref, ng_ref, o_ref, xb_scr, st_scr):
    L, N, P = SSD_CHUNK, SSD_STATE, SSD_HEAD_DIM
    pad = 8

    @pl.when(pl.program_id(1) == 0)
    def _():
        xb_scr[0:pad, :] = jnp.zeros((pad, SSD_CONV_DIM), F32)
        st_scr[...] = jnp.zeros_like(st_scr)

    xb_scr[pad:pad + L, :] = xbc_ref[...]
    cw = cw_ref[...]
    conv = cb_ref[...] + cw[3:4] * xb_scr[pad:pad + L, :]
    for tap in range(1, SSD_CONV):
        conv = conv + cw[3 - tap:4 - tap] * xb_scr[pad - tap:pad - tap + L, :]
    xb_scr[0:pad, :] = xb_scr[L:L + pad, :]
    act = _silu(conv)
    xs = act[:, :SSD_INNER]
    bm = act[:, SSD_INNER:SSD_INNER + 2 * N]
    cm = act[:, SSD_INNER + 2 * N:]

    sm = sm_ref[...]
    lane = _iota((1, 128), 1)
    nega = jnp.where(lane < SSD_HEADS, -jnp.exp(alog_ref[...]), 0.0)
    dt = _softplus(sm + dtb_ref[...])
    a = dt * nega
    smt = sm.T[0:SSD_HEADS, :]
    dtt = _softplus(smt + dtbt_ref[...])
    at = dtt * (-jnp.exp(alogt_ref[...]))
    row = _iota((L, L), 0)
    col = _iota((L, L), 1)
    causal = row >= col
    acum = jnp.dot(causal.astype(F32), a, precision=HIGHEST, preferred_element_type=F32)
    acumt = jnp.dot(at, (row <= col).astype(F32), precision=HIGHEST, preferred_element_type=F32)

    colb = [jnp.broadcast_to(acum[:, h:h + 1], (L, L)) for h in range(SSD_HEADS)]
    dtb = [jnp.broadcast_to(dt[:, h:h + 1], (L, L)) for h in range(SSD_HEADS)]
    left = _iota((L, 2 * P), 1) < P

    def per_head_lanes(vals):
        return jnp.concatenate([jnp.where(left, vals[2 * p], vals[2 * p + 1])
                                for p in range(SSD_HEADS // 2)], axis=1)

    acum_e = per_head_lanes(colb)
    dt_e = per_head_lanes(dtb)
    alast_e = acum_e[L - 1:L, :]
    x_dt = xs * dt_e
    xd = (x_dt * jnp.exp(alast_e - acum_e)).astype(BF16)
    x_dt16 = x_dt.astype(BF16)

    ht = st_scr[...]
    ht16 = ht.astype(BF16)
    heads_per_group = SSD_HEADS // 2
    g_mats, c_mats, st_new = [], [], []
    for g in range(2):
        b_g = bm[:, g * N:(g + 1) * N]
        c_g = cm[:, g * N:(g + 1) * N]
        c_mats.append(c_g)
        g_mats.append(_dot_nt(c_g.astype(BF16), b_g.astype(BF16)))
        width = heads_per_group * P
        st_new.append(_dot(b_g.T.astype(BF16), xd[:, g * width:(g + 1) * width]))
    ys = []
    for p in range(SSD_HEADS // 2):
        g = (2 * p) // heads_per_group
        rhs = jnp.concatenate([x_dt16[:, 2 * p * P:(2 * p + 2) * P],
                               ht16[:, 2 * p * P:(2 * p + 2) * P]], axis=0)
        pair = []
        for h in (2 * p, 2 * p + 1):
            rowb = jnp.broadcast_to(acumt[h:h + 1, :], (L, L))
            decay = jnp.where(causal, jnp.exp(jnp.minimum(colb[h] - rowb, 0.0)), 0.0)
            lhs = jnp.concatenate([g_mats[g] * decay, c_mats[g] * jnp.exp(colb[h])], axis=1)
            pair.append(_dot(lhs.astype(BF16), rhs))
        ys.append(jnp.where(left, pair[0], pair[1]))
    y = jnp.concatenate(ys, axis=1) + xs * dexp_ref[...]
    st_scr[...] = ht * jnp.exp(alast_e) + jnp.concatenate(st_new, axis=1)
    o_ref[...] = _rms(y * _silu(z_ref[...]), ng_ref[...]).astype(BF16)


def _ssd(proj, prm, layer, batch, seq):
    L = SSD_CHUNK
    nc = seq // L
    t = proj.shape[0]
    row = lambda b, c: b * nc + c
    lay3 = lambda shape: pl.BlockSpec((None,) + shape, lambda b, c: (layer, 0, 0))
    return pl.pallas_call(
        _ssd_body,
        grid=(batch, nc),
        in_specs=[
            pl.BlockSpec((L, SSD_INNER), lambda b, c: (row(b, c), C_Z // SSD_INNER)),
            pl.BlockSpec((L, SSD_CONV_DIM), lambda b, c: (row(b, c), C_XBC // SSD_CONV_DIM)),
            pl.BlockSpec((L, 128), lambda b, c: (row(b, c), C_SM // 128)),
            lay3((SSD_CONV, SSD_CONV_DIM)),
            lay3((1, SSD_CONV_DIM)),
            lay3((1, 128)),
            lay3((SSD_HEADS, 1)),
            lay3((1, 128)),
            lay3((SSD_HEADS, 1)),
            lay3((1, SSD_INNER)),
            lay3((1, SSD_INNER)),
        ],
        out_specs=pl.BlockSpec((L, SSD_INNER), lambda b, c: (row(b, c), 0)),
        out_shape=jax.ShapeDtypeStruct((t, SSD_INNER), BF16),
        scratch_shapes=[
            pltpu.VMEM((L + 8, SSD_CONV_DIM), F32),
            pltpu.VMEM((SSD_STATE, SSD_INNER), F32),
        ],
        compiler_params=_cparams(("parallel", "arbitrary")),
        name="ssd",
    )(proj, proj, proj, prm["ssd_conv_w"], prm["ssd_conv_b"], prm["ssd_dtb"], prm["ssd_dtb_t"],
      prm["ssd_alog"], prm["ssd_alog_t"], prm["ssd_dexp"], prm["ssd_norm"])


def _gla_body(q_ref, k_ref, v_ref, r_ref, sm_ref, wgk_ref, bgk_ref, gn_ref, seg_ref, o_ref,
              g_scr, st_scr):
    L, C, H, K, V = GLA_CHUNK, GLA_SUB, GLA_HEADS, GLA_HEAD_K, GLA_HEAD_V
    nb = L // C

    @pl.when(pl.program_id(1) == 0)
    def _():
        st_scr[...] = jnp.zeros_like(st_scr)

    xg = _dot(sm_ref[...].astype(BF16), wgk_ref[...]) + bgk_ref[...]
    log_a = -_softplus(-xg) * (1.0 / GLA_GATE_TAU)
    tril = (_iota((L, L), 0) >= _iota((L, L), 1)).astype(F32)
    g = jnp.dot(tril, log_a, precision=HIGHEST, preferred_element_type=F32)
    g_scr[...] = g
    qs = q_ref[...] * (K ** -0.5)
    k = k_ref[...]
    v = v_ref[...]

    ends = [g_scr[C * j + C - 1:C * j + C, :] for j in range(nb)]
    end_b = jnp.concatenate([jnp.broadcast_to(e, (C, GLA_KD)) for e in ends], axis=0)
    kd = k * jnp.exp(end_b - g)
    k_exp = jnp.where(_iota((H * L, GLA_KD), 0) // L == _iota((H * L, GLA_KD), 1) // K,
                      jnp.concatenate([kd] * H, axis=0), 0.0).astype(BF16)
    rowblk = _iota((L, H * L), 0) // C
    colblk = (_iota((L, H * L), 1) % L) // C

    attn = jnp.zeros((L, H * L), F32)
    for j in range(nb - 1):
        qd = qs * jnp.exp(jnp.minimum(g - ends[j], 0.0))
        aj = _dot_nt(qd.astype(BF16), k_exp)
        attn = jnp.where(colblk == j, jnp.where(rowblk > j, aj, 0.0), attn)

    lrow = _iota((C, GLA_KD), 0)
    pieces = []
    for j in range(C):
        blocks = []
        for i in range(nb):
            r0 = C * i
            gj = g_scr[r0 + j:r0 + j + 1, :]
            kj = k_ref[r0 + j:r0 + j + 1, :]
            rel = jnp.exp(jnp.minimum(g[r0:r0 + C, :] - gj, 0.0))
            blocks.append(jnp.where(lrow >= j, qs[r0:r0 + C, :] * kj * rel, 0.0))
        pieces.append(jnp.concatenate(blocks, axis=0).astype(BF16))
    a_diag = _dot(jnp.concatenate(pieces, axis=1), seg_ref[...])
    attn = jnp.where(colblk == rowblk, a_diag, attn)

    v_bd = jnp.where(_iota((H * L, GLA_VD), 0) // L == _iota((H * L, GLA_VD), 1) // V,
                     jnp.concatenate([v] * H, axis=0), 0.0).astype(BF16)
    st = st_scr[...]
    o = _dot(attn.astype(BF16), v_bd)
    o = o + _dot_nt((qs * jnp.exp(g)).astype(BF16), st.astype(BF16))

    g_last = ends[nb - 1]
    k_dec = (k * jnp.exp(g_last - g)).astype(BF16)
    upd = _dot_tn(v.astype(BF16), k_dec)
    same_head = _iota((GLA_VD, GLA_KD), 0) // V == _iota((GLA_VD, GLA_KD), 1) // K
    st_scr[...] = st * jnp.exp(g_last) + jnp.where(same_head, upd, 0.0)

    normed = jnp.concatenate([_rms(o[:, h * V:(h + 1) * V], gn_ref[...]) for h in range(H)], axis=1)
    o_ref[...] = (normed * _silu(r_ref[...])).astype(BF16)


def _gla(proj, prm, layer, batch, seq):
    L = GLA_CHUNK
    nc = seq // L
    t = proj.shape[0]
    row = lambda b, c: b * nc + c
    lay3 = lambda shape: pl.BlockSpec((None,) + shape, lambda b, c: (layer, 0, 0))
    seg = prm["gla_seg"]
    return pl.pallas_call(
        _gla_body,
        grid=(batch, nc),
        in_specs=[
            pl.BlockSpec((L, GLA_KD), lambda b, c: (row(b, c), C_GQ // GLA_KD)),
            pl.BlockSpec((L, GLA_KD), lambda b, c: (row(b, c), C_GK // GLA_KD)),
            pl.BlockSpec((L, GLA_VD), lambda b, c: (row(b, c), C_GV // GLA_VD)),
            pl.BlockSpec((L, GLA_VD), lambda b, c: (row(b, c), C_GR // GLA_VD)),
            pl.BlockSpec((L, 128), lambda b, c: (row(b, c), C_SM // 128)),
            lay3((128, GLA_KD)),
            lay3((1, GLA_KD)),
            lay3((1, GLA_HEAD_V)),
            pl.BlockSpec(seg.shape, lambda b, c: (0, 0)),
        ],
        out_specs=pl.BlockSpec((L, GLA_VD), lambda b, c: (row(b, c), 0)),
        out_shape=jax.ShapeDtypeStruct((t, GLA_VD), BF16),
        scratch_shapes=[
            pltpu.VMEM((L, GLA_KD), F32),
            pltpu.VMEM((GLA_VD, GLA_KD), F32),
        ],
        compiler_params=_cparams(("parallel", "arbitrary")),
        name="gla",
    )(proj, proj, proj, proj, proj, prm["gla_wgk"], prm["gla_bgk"], prm["gla_norm"], seg)


def _out_proj_body(x_ref, ya_ref, yb_ref, yc_ref, wa_ref, wb_ref, wc_ref, o_ref):
    o_ref[...] = (x_ref[...] + _dot(ya_ref[...], wa_ref[...]) + _dot(yb_ref[...], wb_ref[...])
                  + _dot(yc_ref[...], wc_ref[...]))


def _out_proj(x2d, y_mla, y_ssd, y_gla, w_all, layer, tm=1024, tn=1024):
    t, d = x2d.shape
    da, db, dc = y_mla.shape[1], y_ssd.shape[1], y_gla.shape[1]
    tm = min(tm, t)
    return pl.pallas_call(
        _out_proj_body,
        grid=(t // tm, d // tn),
        in_specs=[
            pl.BlockSpec((tm, tn), lambda i, j: (i, j)),
            pl.BlockSpec((tm, da), lambda i, j: (i, 0)),
            pl.BlockSpec((tm, db), lambda i, j: (i, 0)),
            pl.BlockSpec((tm, dc), lambda i, j: (i, 0)),
            pl.BlockSpec((None, da, tn), lambda i, j: (layer, 0, j)),
            pl.BlockSpec((None, db, tn), lambda i, j: (layer, da // db, j)),
            pl.BlockSpec((None, dc, tn), lambda i, j: (layer, (da + db) // dc, j)),
        ],
        out_specs=pl.BlockSpec((tm, tn), lambda i, j: (i, j)),
        out_shape=jax.ShapeDtypeStruct((t, d), F32),
        compiler_params=_cparams(("parallel", "arbitrary")),
        name="out_proj",
    )(x2d, y_mla, y_ssd, y_gla, w_all, w_all, w_all)


def _ffn_body(x_ref, xh_ref, g_ref, wg_ref, wu_ref, cw_ref, cb_ref, wd_ref, o_ref,
              h_scr, hh_scr, gate_scr, *, tm, tiles_per_seq):
    i = pl.program_id(0)
    f = pl.program_id(1)

    @pl.when(f == 0)
    def _():
        h_scr[...] = _rms(x_ref[...], g_ref[...]).astype(BF16)
        hh_scr[...] = _rms(xh_ref[...], g_ref[...]).astype(BF16)

    wg = wg_ref[...]
    gate = _dot(h_scr[...], wg)
    halo = _dot(hh_scr[...], wg)
    gate_scr[0:HALO, :] = jnp.where(i % tiles_per_seq == 0, 0.0, halo)
    gate_scr[HALO:HALO + tm, :] = gate
    cw = cw_ref[...]
    conv = cb_ref[...] + cw[FFN_CONV - 1:FFN_CONV] * gate
    for tap in range(1, FFN_CONV):
        conv = conv + cw[FFN_CONV - 1 - tap:FFN_CONV - tap] * gate_scr[HALO - tap:HALO - tap + tm, :]
    act = (_silu(conv) * _dot(h_scr[...], wu_ref[...])).astype(BF16)
    contrib = _dot(act, wd_ref[...])

    @pl.when(f == 0)
    def _():
        o_ref[...] = x_ref[...] + contrib

    @pl.when(f > 0)
    def _():
        o_ref[...] += contrib


def _ffn(x2d, prm, layer, seq, tm=512, tf=512):
    t, d = x2d.shape
    tm = min(tm, seq)
    tiles_per_seq = seq // tm
    halo_blocks = tm // HALO
    lay = lambda shape, idx: pl.BlockSpec((None,) + shape, idx)
    return pl.pallas_call(
        functools.partial(_ffn_body, tm=tm, tiles_per_seq=tiles_per_seq),
        grid=(t // tm, D_FF // tf),
        in_specs=[
            pl.BlockSpec((tm, d), lambda i, f: (i, 0)),
            pl.BlockSpec((HALO, d), lambda i, f: (jnp.maximum(i * halo_blocks - 1, 0), 0)),
            lay((1, d), lambda i, f: (layer, 0, 0)),
            lay((d, tf), lambda i, f: (layer, 0, f)),
            lay((d, tf), lambda i, f: (layer, 0, f)),
            lay((FFN_CONV, tf), lambda i, f: (layer, 0, f)),
            lay((1, tf), lambda i, f: (layer, 0, f)),
            lay((tf, d), lambda i, f: (layer, f, 0)),
        ],
        out_specs=pl.BlockSpec((tm, d), lambda i, f: (i, 0)),
        out_shape=jax.ShapeDtypeStruct((t, d), F32),
        scratch_shapes=[
            pltpu.VMEM((tm, d), BF16),
            pltpu.VMEM((HALO, d), BF16),
            pltpu.VMEM((HALO + tm, tf), F32),
        ],
        compiler_params=_cparams(("parallel", "arbitrary")),
        name="ffn",
    )(x2d, x2d, prm["ffn_norm"], prm["ffn_w_gate"], prm["ffn_w_up"], prm["ffn_dw_w"],
      prm["ffn_dw_b"], prm["ffn_w_down"])


def _final_norm_body(x_ref, g_ref, o_ref):
    o_ref[...] = _rms(x_ref[...], g_ref[...])


def _final_norm(x2d, gain, tm=512):
    t, d = x2d.shape
    tm = min(tm, t)
    return pl.pallas_call(
        _final_norm_body,
        grid=(t // tm,),
        in_specs=[pl.BlockSpec((tm, d), lambda i: (i, 0)), pl.BlockSpec((1, d), lambda i: (0, 0))],
        out_specs=pl.BlockSpec((tm, d), lambda i: (i, 0)),
        out_shape=jax.ShapeDtypeStruct((t, d), F32),
        compiler_params=_cparams(("parallel",)),
        name="final_norm",
    )(x2d, gain)


def _prepare(seq, attn_norm, w_in, mla_q_norm, mla_w_uq, mla_kv_norm, mla_w_ukv,
             ssd_conv_w, ssd_conv_b, ssd_dt_bias, ssd_a_log, ssd_d, ssd_norm,
             gla_w_gk, gla_b_gk, gla_norm, w_out,
             ffn_norm, ffn_w_gate, ffn_w_up, ffn_dw_w, ffn_dw_b, ffn_w_down):
    depth = w_in.shape[0]
    o_cq, o_ckv, o_kpe, o_z, o_xbc, o_dt, o_gq, o_gk, o_gv, o_glow, o_gr, o_end = (
        0, 512, 768, 832, 1344, 2368, 2376, 2632, 2888, 3400, 3416, 3928)
    sl = lambda a, b: w_in[:, :, a:b]
    kpe = sl(o_kpe, o_z)
    w_in_r = jnp.concatenate([
        sl(o_cq, o_ckv), sl(o_z, o_xbc), sl(o_gv, o_glow), sl(o_gr, o_end), sl(o_xbc, o_dt),
        sl(o_ckv, o_kpe), sl(o_gq, o_gk), sl(o_gk, o_gv), kpe, kpe, sl(o_dt, o_gq), sl(o_glow, o_gr),
        jnp.zeros((depth, D_MODEL, 128 - SSD_HEADS - GLA_GATE_RANK), w_in.dtype)],
        axis=-1).astype(BF16)

    wq = mla_w_uq.reshape(depth, MLA_Q_RANK, MLA_HEADS, MLA_NOPE + MLA_ROPE).transpose(0, 2, 1, 3)
    wq = jnp.concatenate([wq, wq[..., MLA_NOPE:]], axis=-1).astype(BF16)
    wkv = mla_w_ukv.reshape(depth, MLA_KV_RANK, MLA_HEADS, MLA_NOPE + MLA_V).transpose(0, 2, 1, 3)
    wkv = wkv.astype(BF16)

    inv_freq = ROPE_BASE ** (-jnp.arange(0, MLA_ROPE, 2, dtype=F32) / MLA_ROPE)
    ang = jnp.arange(seq, dtype=F32)[:, None] * inv_freq[None, :]
    cos, sin = jnp.cos(ang), jnp.sin(ang)
    zeros = jnp.zeros((seq, MLA_ROPE), F32)
    cc = jnp.concatenate([cos, cos, zeros], axis=1)
    ss = jnp.concatenate([-sin, sin, zeros], axis=1)

    pad_lanes = lambda a: jnp.pad(a, ((0, 0), (0, 128 - a.shape[1])))[:, None, :]
    wgk = jnp.zeros((depth, 128, GLA_KD), F32).at[:, SM_GLOW:SM_GLOW + GLA_GATE_RANK, :].set(gla_w_gk)

    rows = jnp.arange(GLA_SUB * GLA_KD)
    cols = jnp.arange(GLA_HEADS * GLA_CHUNK)
    seg = ((rows[:, None] % GLA_KD) // GLA_HEAD_K == cols[None, :] // GLA_CHUNK) & (
        rows[:, None] // GLA_KD == cols[None, :] % GLA_SUB)

    return dict(
        attn_norm=attn_norm[:, None, :], w_in=w_in_r,
        mla_q_norm=mla_q_norm[:, None, :], mla_kv_norm=mla_kv_norm[:, None, :], wq=wq, wkv=wkv,
        cc=cc, ss=ss,
        ssd_conv_w=ssd_conv_w, ssd_conv_b=ssd_conv_b[:, None, :],
        ssd_dtb=pad_lanes(ssd_dt_bias), ssd_dtb_t=ssd_dt_bias[:, :, None],
        ssd_alog=pad_lanes(ssd_a_log), ssd_alog_t=ssd_a_log[:, :, None],
        ssd_dexp=jnp.repeat(ssd_d, SSD_HEAD_DIM, axis=1)[:, None, :], ssd_norm=ssd_norm[:, None, :],
        gla_wgk=wgk.astype(BF16), gla_bgk=gla_b_gk[:, None, :], gla_norm=gla_norm[:, None, :],
        gla_seg=seg.astype(BF16),
        w_out=w_out.astype(BF16),
        ffn_norm=ffn_norm[:, None, :], ffn_w_gate=ffn_w_gate.astype(BF16),
        ffn_w_up=ffn_w_up.astype(BF16), ffn_dw_w=ffn_dw_w, ffn_dw_b=ffn_dw_b[:, None, :],
        ffn_w_down=ffn_w_down.astype(BF16),
    )


def kernel(x, attn_norm, w_in, mla_q_norm, mla_w_uq, mla_kv_norm, mla_w_ukv, ssd_conv_w, ssd_conv_b,
           ssd_dt_bias, ssd_a_log, ssd_d, ssd_norm, gla_w_gk, gla_b_gk, gla_norm, w_out, ffn_norm,
           ffn_w_gate, ffn_w_up, ffn_dw_w, ffn_dw_b, ffn_w_down, final_norm):
    batch, seq, d = x.shape
    prm = _prepare(seq, attn_norm, w_in, mla_q_norm, mla_w_uq, mla_kv_norm, mla_w_ukv,
                   ssd_conv_w, ssd_conv_b, ssd_dt_bias, ssd_a_log, ssd_d, ssd_norm,
                   gla_w_gk, gla_b_gk, gla_norm, w_out,
                   ffn_norm, ffn_w_gate, ffn_w_up, ffn_dw_w, ffn_dw_b, ffn_w_down)
    x2d = x.reshape(batch * seq, d)
    for layer in range(w_in.shape[0]):
        proj = _in_proj(x2d, prm["attn_norm"], prm["w_in"], layer)
        q, k, v = _mla_proj(proj, prm["mla_q_norm"], prm["mla_kv_norm"], prm["wq"], prm["wkv"],
                            prm["cc"], prm["ss"], layer, batch, seq)
        y_mla = _mla_attn(q, k, v).reshape(batch * seq, MLA_HEADS * MLA_V)
        y_ssd = _ssd(proj, prm, layer, batch, seq)
        y_gla = _gla(proj, prm, layer, batch, seq)
        x2d = _out_proj(x2d, y_mla, y_ssd, y_gla, prm["w_out"], layer)
        x2d = _ffn(x2d, prm, layer, seq)
    return _final_norm(x2d, final_norm[None, :]).reshape(batch, seq, d)
```

```python
import functools

import jax
import jax.numpy as jnp
from jax import lax
from jax.experimental import pallas as pl
from jax.experimental.pallas import tpu as pltpu

F32 = jnp.float32
BF16 = jnp.bfloat16
HIGHEST = lax.Precision.HIGHEST

D_MODEL = 2048
DEPTH = 4
EPS = 1e-6
MLA_HEADS = 8
MLA_Q_RANK = 512
MLA_KV_RANK = 256
MLA_NOPE = 128
MLA_ROPE = 64
MLA_V = 128
MLA_QK = 256
ROPE_BASE = 10000.0
SSD_HEADS = 8
SSD_HEAD_DIM = 64
SSD_STATE = 128
SSD_CONV = 4
SSD_CHUNK = 128
SSD_INNER = 512
SSD_CONV_DIM = 1024
GLA_HEADS = 4
GLA_HEAD_K = 64
GLA_HEAD_V = 128
GLA_GATE_RANK = 16
GLA_GATE_TAU = 16.0
GLA_CHUNK = 64
GLA_SUB = 16
GLA_KD = 256
GLA_VD = 512
D_FF = 5632
FFN_CONV = 3
HALO = 16

P_WIDTH = 4096
C_CQ, C_Z, C_GV, C_GR, C_XBC, C_CKV, C_GQ, C_GK, C_KPE, C_SM = (
    0, 512, 1024, 1536, 2048, 3072, 3328, 3584, 3840, 3968)
SM_DT = 0
SM_GLOW = 8

VMEM_LIMIT = 52 * 1024 * 1024


def _cparams(sem):
    return pltpu.CompilerParams(dimension_semantics=sem, vmem_limit_bytes=VMEM_LIMIT)


def _rms(x, g):
    var = jnp.mean(x * x, axis=-1, keepdims=True)
    return x * lax.rsqrt(var + EPS) * g


def _softplus(x):
    return jnp.maximum(x, 0.0) + jnp.log(1.0 + jnp.exp(-jnp.abs(x)))


def _silu(x):
    return x * jax.nn.sigmoid(x)


def _dot(a, b):
    return jnp.dot(a, b, preferred_element_type=F32)


def _dot_nt(a, b):
    return lax.dot_general(a, b, (((1,), (1,)), ((), ())), preferred_element_type=F32)


def _dot_tn(a, b):
    return lax.dot_general(a, b, (((0,), (0,)), ((), ())), preferred_element_type=F32)


def _iota(shape, axis):
    return lax.broadcasted_iota(jnp.int32, shape, axis)


def _in_proj_body(x_ref, g_ref, w_ref, o_ref, h_scr):
    @pl.when(pl.program_id(1) == 0)
    def _():
        h_scr[...] = _rms(x_ref[...], g_ref[...]).astype(BF16)

    o_ref[...] = _dot(h_scr[...], w_ref[...])


def _in_proj(x2d, gains, w_all, layer, tm=1024, tn=1024):
    t, d = x2d.shape
    tm = min(tm, t)
    return pl.pallas_call(
        _in_proj_body,
        grid=(t // tm, P_WIDTH // tn),
        in_specs=[
            pl.BlockSpec((tm, d), lambda i, j: (i, 0)),
            pl.BlockSpec((None, 1, d), lambda i, j: (layer, 0, 0)),
            pl.BlockSpec((None, d, tn), lambda i, j: (layer, 0, j)),
        ],
        out_specs=pl.BlockSpec((tm, tn), lambda i, j: (i, j)),
        out_shape=jax.ShapeDtypeStruct((t, P_WIDTH), F32),
        scratch_shapes=[pltpu.VMEM((tm, d), BF16)],
        compiler_params=_cparams(("parallel", "arbitrary")),
        name="in_proj",
    )(x2d, gains, w_all)


def _rope(blk, cc, ss):
    return blk * cc + pltpu.roll(blk, MLA_ROPE // 2, axis=1) * ss


def _mla_proj_body(cq_ref, ckv_ref, kpe_ref, qn_ref, kvn_ref, wq_ref, wkv_ref, cc_ref, ss_ref,
                   qt_out, k_out, vt_out, *, tk):
    cc = cc_ref[...]
    ss = ss_ref[...]
    hq = _rms(cq_ref[...], qn_ref[...]).astype(BF16)
    hkv = _rms(ckv_ref[...], kvn_ref[...]).astype(BF16)
    kpe = _rope(kpe_ref[...], cc, ss).astype(BF16)
    scale = (MLA_NOPE + MLA_ROPE) ** -0.5
    qf = _dot(hq, wq_ref[...])
    kvf = _dot(hkv, wkv_ref[...])
    tm = qf.shape[0]
    for h in range(MLA_HEADS):
        base = h * MLA_QK
        q_pe = _rope(qf[:, base + MLA_NOPE:base + MLA_QK], cc, ss)
        q_h = jnp.concatenate([qf[:, base:base + MLA_NOPE], q_pe], axis=1) * scale
        qt_out[h] = q_h.T.astype(BF16)
        k_out[h] = jnp.concatenate([kvf[:, base:base + MLA_NOPE].astype(BF16), kpe], axis=1)
        v_h = kvf[:, base + MLA_NOPE:base + MLA_QK]
        for j in range(tm // tk):
            vt_out[h, j] = v_h[j * tk:(j + 1) * tk, :].T.astype(BF16)


def _mla_proj(proj, q_norm, kv_norm, wq_all, wkv_all, cc, ss, layer, batch, seq, tm=512, tk=256):
    t = proj.shape[0]
    tm = min(tm, seq)
    tk = min(tk, tm)
    nsb = seq // tm
    H = MLA_HEADS
    return pl.pallas_call(
        functools.partial(_mla_proj_body, tk=tk),
        grid=(t // tm,),
        in_specs=[
            pl.BlockSpec((tm, MLA_Q_RANK), lambda i: (i, C_CQ // MLA_Q_RANK)),
            pl.BlockSpec((tm, MLA_KV_RANK), lambda i: (i, C_CKV // MLA_KV_RANK)),
            pl.BlockSpec((tm, 128), lambda i: (i, C_KPE // 128)),
            pl.BlockSpec((None, 1, MLA_Q_RANK), lambda i: (layer, 0, 0)),
            pl.BlockSpec((None, 1, MLA_KV_RANK), lambda i: (layer, 0, 0)),
            pl.BlockSpec((None, MLA_Q_RANK, H * MLA_QK), lambda i: (layer, 0, 0)),
            pl.BlockSpec((None, MLA_KV_RANK, H * MLA_QK), lambda i: (layer, 0, 0)),
            pl.BlockSpec((tm, 128), lambda i: (i % nsb, 0)),
            pl.BlockSpec((tm, 128), lambda i: (i % nsb, 0)),
        ],
        out_specs=[
            pl.BlockSpec((None, H, MLA_QK, tm), lambda i: (i // nsb, 0, 0, i % nsb)),
            pl.BlockSpec((None, H, tm, MLA_QK), lambda i: (i // nsb, 0, i % nsb, 0)),
            pl.BlockSpec((None, H, tm // tk, MLA_V, tk), lambda i: (i // nsb, 0, i % nsb, 0, 0)),
        ],
        out_shape=[
            jax.ShapeDtypeStruct((batch, H, MLA_QK, seq), BF16),
            jax.ShapeDtypeStruct((batch, H, seq, MLA_QK), BF16),
            jax.ShapeDtypeStruct((batch, H, seq // tk, MLA_V, tk), BF16),
        ],
        compiler_params=_cparams(("parallel",)),
        name="mla_proj",
    )(proj, proj, proj, q_norm, kv_norm, wq_all, wkv_all, cc, ss)


def _attn_body(qt_ref, k_ref, vt_ref, o_ref, acc_scr, *, tq, hp):
    qi = pl.program_id(2)
    qts = [qt_ref[h] for h in range(hp)]

    def step(kb, carry, diagonal):
        start = pl.multiple_of(kb * tq, tq)
        ss = [_dot(k_ref[h, pl.ds(start, tq), :], qts[h]) for h in range(hp)]
        out = []
        for h in range(hp):
            m, l = carry[h]
            s = ss[h]
            if diagonal:
                s = jnp.where(_iota((tq, tq), 0) <= _iota((tq, tq), 1), s, -jnp.inf)
            m_new = jnp.maximum(m, jnp.max(s, axis=0, keepdims=True))
            alpha = jnp.exp(m - m_new)
            p = jnp.exp(s - m_new)
            l = alpha * l + jnp.sum(p, axis=0, keepdims=True)
            acc_scr[h] = alpha * acc_scr[h] + _dot(vt_ref[h, kb], p.astype(BF16))
            out.append((m_new, l))
        return tuple(out)

    acc_scr[...] = jnp.zeros_like(acc_scr)
    init = tuple((jnp.full((1, tq), -jnp.inf, F32), jnp.zeros((1, tq), F32)) for _ in range(hp))
    carry = lax.fori_loop(0, qi, lambda kb, c: step(kb, c, False), init)
    carry = step(qi, carry, True)
    o_ref[...] = jnp.concatenate([(acc_scr[h] / carry[h][1]).T for h in range(hp)],
                                 axis=1).astype(BF16)


def _mla_attn(qt, k, vt, hp=4):
    batch, heads, _, seq = qt.shape
    tq = vt.shape[-1]
    return pl.pallas_call(
        functools.partial(_attn_body, tq=tq, hp=hp),
        grid=(batch, heads // hp, seq // tq),
        in_specs=[
            pl.BlockSpec((None, hp, MLA_QK, tq), lambda b, h, i: (b, h, 0, i)),
            pl.BlockSpec((None, hp, seq, MLA_QK), lambda b, h, i: (b, h, 0, 0)),
            pl.BlockSpec((None, hp, seq // tq, MLA_V, tq), lambda b, h, i: (b, h, 0, 0, 0)),
        ],
        out_specs=pl.BlockSpec((None, tq, hp * MLA_V), lambda b, h, i: (b, i, h)),
        out_shape=jax.ShapeDtypeStruct((batch, seq, heads * MLA_V), BF16),
        scratch_shapes=[pltpu.VMEM((hp, MLA_V, tq), F32)],
        compiler_params=_cparams(("parallel", "parallel", "arbitrary")),
        name="mla_attn",
    )(qt, k, vt)


def _ssd_body(z_ref, xbc_ref, sm_ref, cw_ref, cb_ref, dtb_ref, dtbt_ref, alog_ref, alogt_ref,
              dexp_ref, ng_ref, o_ref, xb_scr, st_scr):
    L, N, P = SSD_CHUNK, SSD_STATE, SSD_HEAD_DIM
    pad = 8

    @pl.when(pl.program_id(1) == 0)
    def _():
        xb_scr[0:pad, :] = jnp.zeros((pad, SSD_CONV_DIM), F32)
        st_scr[...] = jnp.zeros_like(st_scr)

    xb_scr[pad:pad + L, :] = xbc_ref[...]
    cw = cw_ref[...]
    conv = cb_ref[...] + cw[3:4] * xb_scr[pad:pad + L, :]
    for tap in range(1, SSD_CONV):
        conv = conv + cw[3 - tap:4 - tap] * xb_scr[pad - tap:pad - tap + L, :]
    xb_scr[0:pad, :] = xb_scr[L:L + pad, :]
    act = _silu(conv)
    xs = act[:, :SSD_INNER]
    bm = act[:, SSD_INNER:SSD_INNER + 2 * N]
    cm = act[:, SSD_INNER + 2 * N:]

    sm = sm_ref[...]
    lane = _iota((1, 128), 1)
    nega = jnp.where(lane < SSD_HEADS, -jnp.exp(alog_ref[...]), 0.0)
    dt = _softplus(sm + dtb_ref[...])
    a = dt * nega
    smt = sm.T[0:SSD_HEADS, :]
    dtt = _softplus(smt + dtbt_ref[...])
    at = dtt * (-jnp.exp(alogt_ref[...]))
    row = _iota((L, L), 0)
    col = _iota((L, L), 1)
    causal = row >= col
    acum = jnp.dot(causal.astype(F32), a, precision=HIGHEST, preferred_element_type=F32)
    acumt = jnp.dot(at, (row <= col).astype(F32), precision=HIGHEST, preferred_element_type=F32)

    colb = [jnp.broadcast_to(acum[:, h:h + 1], (L, L)) for h in range(SSD_HEADS)]
    dtb = [jnp.broadcast_to(dt[:, h:h + 1], (L, L)) for h in range(SSD_HEADS)]
    left = _iota((L, 2 * P), 1) < P

    def per_head_lanes(vals):
        return jnp.concatenate([jnp.where(left, vals[2 * p], vals[2 * p + 1])
                                for p in range(SSD_HEADS // 2)], axis=1)

    acum_e = per_head_lanes(colb)
    dt_e = per_head_lanes(dtb)
    alast_e = acum_e[L - 1:L, :]
    x_dt = xs * dt_e
    xd = (x_dt * jnp.exp(alast_e - acum_e)).astype(BF16)
    x_dt16 = x_dt.astype(BF16)

    ht = st_scr[...]
    ht16 = ht.astype(BF16)
    heads_per_group = SSD_HEADS // 2
    g_mats, c_mats, st_new = [], [], []
    for g in range(2):
        b_g = bm[:, g * N:(g + 1) * N]
        c_g = cm[:, g * N:(g + 1) * N]
        c_mats.append(c_g)
        g_mats.append(_dot_nt(c_g.astype(BF16), b_g.astype(BF16)))
        width = heads_per_group * P
        st_new.append(_dot(b_g.T.astype(BF16), xd[:, g * width:(g + 1) * width]))
    ys = []
    for p in range(SSD_HEADS // 2):
        g = (2 * p) // heads_per_group
        rhs = jnp.concatenate([x_dt16[:, 2 * p * P:(2 * p + 2) * P],
                               ht16[:, 2 * p * P:(2 * p + 2) * P]], axis=0)
        pair = []
        for h in (2 * p, 2 * p + 1):
            rowb = jnp.broadcast_to(acumt[h:h + 1, :], (L, L))
            decay = jnp.where(causal, jnp.exp(jnp.minimum(colb[h] - rowb, 0.0)), 0.0)
            lhs = jnp.concatenate([g_mats[g] * decay, c_mats[g] * jnp.exp(colb[h])], axis=1)
            pair.append(_dot(lhs.astype(BF16), rhs))
        ys.append(jnp.where(left, pair[0], pair[1]))
    y = jnp.concatenate(ys, axis=1) + xs * dexp_ref[...]
    st_scr[...] = ht * jnp.exp(alast_e) + jnp.concatenate(st_new, axis=1)
    o_ref[...] = _rms(y * _silu(z_ref[...]), ng_ref[...]).astype(BF16)


def _ssd(proj, prm, layer, batch, seq):
    L = SSD_CHUNK
    nc = seq // L
    t = proj.shape[0]
    row = lambda b, c: b * nc + c
    lay3 = lambda shape: pl.BlockSpec((None,) + shape, lambda b, c: (layer, 0, 0))
    return pl.pallas_call(
        _ssd_body,
        grid=(batch, nc),
        in_specs=[
            pl.BlockSpec((L, SSD_INNER), lambda b, c: (row(b, c), C_Z // SSD_INNER)),
            pl.BlockSpec((L, SSD_CONV_DIM), lambda b, c: (row(b, c), C_XBC // SSD_CONV_DIM)),
            pl.BlockSpec((L, 128), lambda b, c: (row(b, c), C_SM // 128)),
            lay3((SSD_CONV, SSD_CONV_DIM)),
            lay3((1, SSD_CONV_DIM)),
            lay3((1, 128)),
            lay3((SSD_HEADS, 1)),
            lay3((1, 128)),
            lay3((SSD_HEADS, 1)),
            lay3((1, SSD_INNER)),
            lay3((1, SSD_INNER)),
        ],
        out_specs=pl.BlockSpec((L, SSD_INNER), lambda b, c: (row(b, c), 0)),
        out_shape=jax.ShapeDtypeStruct((t, SSD_INNER), BF16),
        scratch_shapes=[
            pltpu.VMEM((L + 8, SSD_CONV_DIM), F32),
            pltpu.VMEM((SSD_STATE, SSD_INNER), F32),
        ],
        compiler_params=_cparams(("parallel", "arbitrary")),
        name="ssd",
    )(proj, proj, proj, prm["ssd_conv_w"], prm["ssd_conv_b"], prm["ssd_dtb"], prm["ssd_dtb_t"],
      prm["ssd_alog"], prm["ssd_alog_t"], prm["ssd_dexp"], prm["ssd_norm"])


def _gla_body(q_ref, k_ref, v_ref, r_ref, sm_ref, wgk_ref, bgk_ref, gn_ref, seg_ref, o_ref,
              g_scr, st_scr):
    L, C, H, K, V = GLA_CHUNK, GLA_SUB, GLA_HEADS, GLA_HEAD_K, GLA_HEAD_V
    nb = L // C

    @pl.when(pl.program_id(1) == 0)
    def _():
        st_scr[...] = jnp.zeros_like(st_scr)

    xg = _dot(sm_ref[...].astype(BF16), wgk_ref[...]) + bgk_ref[...]
    log_a = -_softplus(-xg) * (1.0 / GLA_GATE_TAU)
    tril = (_iota((L, L), 0) >= _iota((L, L), 1)).astype(F32)
    g = jnp.dot(tril, log_a, precision=HIGHEST, preferred_element_type=F32)
    g_scr[...] = g
    qs = q_ref[...] * (K ** -0.5)
    k = k_ref[...]
    v = v_ref[...]

    ends = [g_scr[C * j + C - 1:C * j + C, :] for j in range(nb)]
    end_b = jnp.concatenate([jnp.broadcast_to(e, (C, GLA_KD)) for e in ends], axis=0)
    kd = k * jnp.exp(end_b - g)
    k_exp = jnp.where(_iota((H * L, GLA_KD), 0) // L == _iota((H * L, GLA_KD), 1) // K,
                      jnp.concatenate([kd] * H, axis=0), 0.0).astype(BF16)
    rowblk = _iota((L, H * L), 0) // C
    colblk = (_iota((L, H * L), 1) % L) // C

    attn = jnp.zeros((L, H * L), F32)
    for j in range(nb - 1):
        qd = qs * jnp.exp(jnp.minimum(g - ends[j], 0.0))
        aj = _dot_nt(qd.astype(BF16), k_exp)
        attn = jnp.where(colblk == j, jnp.where(rowblk > j, aj, 0.0), attn)

    lrow = _iota((C, GLA_KD), 0)
    pieces = []
    for j in range(C):
        blocks = []
        for i in range(nb):
            r0 = C * i
            gj = g_scr[r0 + j:r0 + j + 1, :]
            kj = k_ref[r0 + j:r0 + j + 1, :]
            rel = jnp.exp(jnp.minimum(g[r0:r0 + C, :] - gj, 0.0))
            blocks.append(jnp.where(lrow >= j, qs[r0:r0 + C, :] * kj * rel, 0.0))
        pieces.append(jnp.concatenate(blocks, axis=0).astype(BF16))
    a_diag = _dot(jnp.concatenate(pieces, axis=1), seg_ref[...])
    attn = jnp.where(colblk == rowblk, a_diag, attn)

    v_bd = jnp.where(_iota((H * L, GLA_VD), 0) // L == _iota((H * L, GLA_VD), 1) // V,
                     jnp.concatenate([v] * H, axis=0), 0.0).astype(BF16)
    st = st_scr[...]
    o = _dot(attn.astype(BF16), v_bd)
    o = o + _dot_nt((qs * jnp.exp(g)).astype(BF16), st.astype(BF16))

    g_last = ends[nb - 1]
    k_dec = (k * jnp.exp(g_last - g)).astype(BF16)
    upd = _dot_tn(v.astype(BF16), k_dec)
    same_head = _iota((GLA_VD, GLA_KD), 0) // V == _iota((GLA_VD, GLA_KD), 1) // K
    st_scr[...] = st * jnp.exp(g_last) + jnp.where(same_head, upd, 0.0)

    normed = jnp.concatenate([_rms(o[:, h * V:(h + 1) * V], gn_ref[...]) for h in range(H)], axis=1)
    o_ref[...] = (normed * _silu(r_ref[...])).astype(BF16)


def _gla(proj, prm, layer, batch, seq):
    L = GLA_CHUNK
    nc = seq // L
    t = proj.shape[0]
    row = lambda b, c: b * nc + c
    lay3 = lambda shape: pl.BlockSpec((None,) + shape, lambda b, c: (layer, 0, 0))
    seg = prm["gla_seg"]
    return pl.pallas_call(
        _gla_body,
        grid=(batch, nc),
        in_specs=[
            pl.BlockSpec((L, GLA_KD), lambda b, c: (row(b, c), C_GQ // GLA_KD)),
            pl.BlockSpec((L, GLA_KD), lambda b, c: (row(b, c), C_GK // GLA_KD)),
            pl.BlockSpec((L, GLA_VD), lambda b, c: (row(b, c), C_GV // GLA_VD)),
            pl.BlockSpec((L, GLA_VD), lambda b, c: (row(b, c), C_GR // GLA_VD)),
            pl.BlockSpec((L, 128), lambda b, c: (row(b, c), C_SM // 128)),
            lay3((128, GLA_KD)),
            lay3((1, GLA_KD)),
            lay3((1, GLA_HEAD_V)),
            pl.BlockSpec(seg.shape, lambda b, c: (0, 0)),
        ],
        out_specs=pl.BlockSpec((L, GLA_VD), lambda b, c: (row(b, c), 0)),
        out_shape=jax.ShapeDtypeStruct((t, GLA_VD), BF16),
        scratch_shapes=[
            pltpu.VMEM((L, GLA_KD), F32),
            pltpu.VMEM((GLA_VD, GLA_KD), F32),
        ],
        compiler_params=_cparams(("parallel", "arbitrary")),
        name="gla",
    )(proj, proj, proj, proj, proj, prm["gla_wgk"], prm["gla_bgk"], prm["gla_norm"], seg)


def _out_proj_body(x_ref, ya_ref, yb_ref, yc_ref, wa_ref, wb_ref, wc_ref, o_ref):
    o_ref[...] = (x_ref[...] + _dot(ya_ref[...], wa_ref[...]) + _dot(yb_ref[...], wb_ref[...])
                  + _dot(yc_ref[...], wc_ref[...]))


def _out_proj(x2d, y_mla, y_ssd, y_gla, w_all, layer, tm=1024, tn=1024):
    t, d = x2d.shape
    da, db, dc = y_mla.shape[1], y_ssd.shape[1], y_gla.shape[1]
    tm = min(tm, t)
    return pl.pallas_call(
        _out_proj_body,
        grid=(t // tm, d // tn),
        in_specs=[
            pl.BlockSpec((tm, tn), lambda i, j: (i, j)),
            pl.BlockSpec((tm, da), lambda i, j: (i, 0)),
            pl.BlockSpec((tm, db), lambda i, j: (i, 0)),
            pl.BlockSpec((tm, dc), lambda i, j: (i, 0)),
            pl.BlockSpec((None, da, tn), lambda i, j: (layer, 0, j)),
            pl.BlockSpec((None, db, tn), lambda i, j: (layer, da // db, j)),
            pl.BlockSpec((None, dc, tn), lambda i, j: (layer, (da + db) // dc, j)),
        ],
        out_specs=pl.BlockSpec((tm, tn), lambda i, j: (i, j)),
        out_shape=jax.ShapeDtypeStruct((t, d), F32),
        compiler_params=_cparams(("parallel", "arbitrary")),
        name="out_proj",
    )(x2d, y_mla, y_ssd, y_gla, w_all, w_all, w_all)


def _ffn_body(x_ref, xh_ref, g_ref, wg_ref, wu_ref, cw_ref, cb_ref, wd_ref, o_ref,
              h_scr, hh_scr, gate_scr, *, tm, tiles_per_seq):
    i = pl.program_id(0)
    f = pl.program_id(1)

    @pl.when(f == 0)
    def _():
        h_scr[...] = _rms(x_ref[...], g_ref[...]).astype(BF16)
        hh_scr[...] = _rms(xh_ref[...], g_ref[...]).astype(BF16)

    wg = wg_ref[...]
    gate = _dot(h_scr[...], wg)
    halo = _dot(hh_scr[...], wg)
    gate_scr[0:HALO, :] = jnp.where(i % tiles_per_seq == 0, 0.0, halo)
    gate_scr[HALO:HALO + tm, :] = gate
    cw = cw_ref[...]
    conv = cb_ref[...] + cw[FFN_CONV - 1:FFN_CONV] * gate
    for tap in range(1, FFN_CONV):
        conv = conv + cw[FFN_CONV - 1 - tap:FFN_CONV - tap] * gate_scr[HALO - tap:HALO - tap + tm, :]
    act = (_silu(conv) * _dot(h_scr[...], wu_ref[...])).astype(BF16)
    contrib = _dot(act, wd_ref[...])

    @pl.when(f == 0)
    def _():
        o_ref[...] = x_ref[...] + contrib

    @pl.when(f > 0)
    def _():
        o_ref[...] += contrib


def _ffn(x2d, prm, layer, seq, tm=512, tf=512):
    t, d = x2d.shape
    tm = min(tm, seq)
    tiles_per_seq = seq // tm
    halo_blocks = tm // HALO
    lay = lambda shape, idx: pl.BlockSpec((None,) + shape, idx)
    return pl.pallas_call(
        functools.partial(_ffn_body, tm=tm, tiles_per_seq=tiles_per_seq),
        grid=(t // tm, D_FF // tf),
        in_specs=[
            pl.BlockSpec((tm, d), lambda i, f: (i, 0)),
            pl.BlockSpec((HALO, d), lambda i, f: (jnp.maximum(i * halo_blocks - 1, 0), 0)),
            lay((1, d), lambda i, f: (layer, 0, 0)),
            lay((d, tf), lambda i, f: (layer, 0, f)),
            lay((d, tf), lambda i, f: (layer, 0, f)),
            lay((FFN_CONV, tf), lambda i, f: (layer, 0, f)),
            lay((1, tf), lambda i, f: (layer, 0, f)),
            lay((tf, d), lambda i, f: (layer, f, 0)),
        ],
        out_specs=pl.BlockSpec((tm, d), lambda i, f: (i, 0)),
        out_shape=jax.ShapeDtypeStruct((t, d), F32),
        scratch_shapes=[
            pltpu.VMEM((tm, d), BF16),
            pltpu.VMEM((HALO, d), BF16),
            pltpu.VMEM((HALO + tm, tf), F32),
        ],
        compiler_params=_cparams(("parallel", "arbitrary")),
        name="ffn",
    )(x2d, x2d, prm["ffn_norm"], prm["ffn_w_gate"], prm["ffn_w_up"], prm["ffn_dw_w"],
      prm["ffn_dw_b"], prm["ffn_w_down"])


def _final_norm_body(x_ref, g_ref, o_ref):
    o_ref[...] = _rms(x_ref[...], g_ref[...])


def _final_norm(x2d, gain, tm=512):
    t, d = x2d.shape
    tm = min(tm, t)
    return pl.pallas_call(
        _final_norm_body,
        grid=(t // tm,),
        in_specs=[pl.BlockSpec((tm, d), lambda i: (i, 0)), pl.BlockSpec((1, d), lambda i: (0, 0))],
        out_specs=pl.BlockSpec((tm, d), lambda i: (i, 0)),
        out_shape=jax.ShapeDtypeStruct((t, d), F32),
        compiler_params=_cparams(("parallel",)),
        name="final_norm",
    )(x2d, gain)


def _prepare(seq, attn_norm, w_in, mla_q_norm, mla_w_uq, mla_kv_norm, mla_w_ukv,
             ssd_conv_w, ssd_conv_b, ssd_dt_bias, ssd_a_log, ssd_d, ssd_norm,
             gla_w_gk, gla_b_gk, gla_norm, w_out,
             ffn_norm, ffn_w_gate, ffn_w_up, ffn_dw_w, ffn_dw_b, ffn_w_down):
    depth = w_in.shape[0]
    o_cq, o_ckv, o_kpe, o_z, o_xbc, o_dt, o_gq, o_gk, o_gv, o_glow, o_gr, o_end = (
        0, 512, 768, 832, 1344, 2368, 2376, 2632, 2888, 3400, 3416, 3928)
    sl = lambda a, b: w_in[:, :, a:b]
    kpe = sl(o_kpe, o_z)
    w_in_r = jnp.concatenate([
        sl(o_cq, o_ckv), sl(o_z, o_xbc), sl(o_gv, o_glow), sl(o_gr, o_end), sl(o_xbc, o_dt),
        sl(o_ckv, o_kpe), sl(o_gq, o_gk), sl(o_gk, o_gv), kpe, kpe, sl(o_dt, o_gq), sl(o_glow, o_gr),
        jnp.zeros((depth, D_MODEL, 128 - SSD_HEADS - GLA_GATE_RANK), w_in.dtype)],
        axis=-1).astype(BF16)

    wq = mla_w_uq.reshape(depth, MLA_Q_RANK, MLA_HEADS, MLA_NOPE + MLA_ROPE)
    wq = jnp.concatenate([wq, wq[..., MLA_NOPE:]], axis=-1)
    wq = wq.reshape(depth, MLA_Q_RANK, MLA_HEADS * MLA_QK).astype(BF16)
    wkv = mla_w_ukv.astype(BF16)

    inv_freq = ROPE_BASE ** (-jnp.arange(0, MLA_ROPE, 2, dtype=F32) / MLA_ROPE)
    ang = jnp.arange(seq, dtype=F32)[:, None] * inv_freq[None, :]
    cos, sin = jnp.cos(ang), jnp.sin(ang)
    zeros = jnp.zeros((seq, MLA_ROPE), F32)
    cc = jnp.concatenate([cos, cos, zeros], axis=1)
    ss = jnp.concatenate([-sin, sin, zeros], axis=1)

    pad_lanes = lambda a: jnp.pad(a, ((0, 0), (0, 128 - a.shape[1])))[:, None, :]
    wgk = jnp.zeros((depth, 128, GLA_KD), F32).at[:, SM_GLOW:SM_GLOW + GLA_GATE_RANK, :].set(gla_w_gk)

    rows = jnp.arange(GLA_SUB * GLA_KD)
    cols = jnp.arange(GLA_HEADS * GLA_CHUNK)
    seg = ((rows[:, None] % GLA_KD) // GLA_HEAD_K == cols[None, :] // GLA_CHUNK) & (
        rows[:, None] // GLA_KD == cols[None, :] % GLA_SUB)

    return dict(
        attn_norm=attn_norm[:, None, :], w_in=w_in_r,
        mla_q_norm=mla_q_norm[:, None, :], mla_kv_norm=mla_kv_norm[:, None, :], wq=wq, wkv=wkv,
        cc=cc, ss=ss,
        ssd_conv_w=ssd_conv_w, ssd_conv_b=ssd_conv_b[:, None, :],
        ssd_dtb=pad_lanes(ssd_dt_bias), ssd_dtb_t=ssd_dt_bias[:, :, None],
        ssd_alog=pad_lanes(ssd_a_log), ssd_alog_t=ssd_a_log[:, :, None],
        ssd_dexp=jnp.repeat(ssd_d, SSD_HEAD_DIM, axis=1)[:, None, :], ssd_norm=ssd_norm[:, None, :],
        gla_wgk=wgk.astype(BF16), gla_bgk=gla_b_gk[:, None, :], gla_norm=gla_norm[:, None, :],
        gla_seg=seg.astype(BF16),
        w_out=w_out.astype(BF16),
        ffn_norm=ffn_norm[:, None, :], ffn_w_gate=ffn_w_gate.astype(BF16),
        ffn_w_up=ffn_w_up.astype(BF16), ffn_dw_w=ffn_dw_w, ffn_dw_b=ffn_dw_b[:, None, :],
        ffn_w_down=ffn_w_down.astype(BF16),
    )


def kernel(x, attn_norm, w_in, mla_q_norm, mla_w_uq, mla_kv_norm, mla_w_ukv, ssd_conv_w, ssd_conv_b,
           ssd_dt_bias, ssd_a_log, ssd_d, ssd_norm, gla_w_gk, gla_b_gk, gla_norm, w_out, ffn_norm,
           ffn_w_gate, ffn_w_up, ffn_dw_w, ffn_dw_b, ffn_w_down, final_norm):
    batch, seq, d = x.shape
    prm = _prepare(seq, attn_norm, w_in, mla_q_norm, mla_w_uq, mla_kv_norm, mla_w_ukv,
                   ssd_conv_w, ssd_conv_b, ssd_dt_bias, ssd_a_log, ssd_d, ssd_norm,
                   gla_w_gk, gla_b_gk, gla_norm, w_out,
                   ffn_norm, ffn_w_gate, ffn_w_up, ffn_dw_w, ffn_dw_b, ffn_w_down)
    x2d = x.reshape(batch * seq, d)
    for layer in range(w_in.shape[0]):
        proj = _in_proj(x2d, prm["attn_norm"], prm["w_in"], layer)
        qt, k, vt = _mla_proj(proj, prm["mla_q_norm"], prm["mla_kv_norm"], prm["wq"], prm["wkv"],
                              prm["cc"], prm["ss"], layer, batch, seq)
        y_mla = _mla_attn(qt, k, vt).reshape(batch * seq, MLA_HEADS * MLA_V)
        y_ssd = _ssd(proj, prm, layer, batch, seq)
        y_gla = _gla(proj, prm, layer, batch, seq)
        x2d = _out_proj(x2d, y_mla, y_ssd, y_gla, prm["w_out"], layer)
        x2d = _ffn(x2d, prm, layer, seq)
    return _final_norm(x2d, final_norm[None, :]).reshape(batch, seq, d)
```

```python
import functools

import jax
import jax.numpy as jnp
from jax import lax
from jax.experimental import pallas as pl
from jax.experimental.pallas import tpu as pltpu

F32 = jnp.float32
BF16 = jnp.bfloat16
HIGHEST = lax.Precision.HIGHEST

D_MODEL = 2048
DEPTH = 4
EPS = 1e-6
MLA_HEADS = 8
MLA_Q_RANK = 512
MLA_KV_RANK = 256
MLA_NOPE = 128
MLA_ROPE = 64
MLA_V = 128
MLA_QK = 256
MLA_VE = MLA_V + 16
LOG2_E = 1.4426950408889634
ROPE_BASE = 10000.0
SSD_HEADS = 8
SSD_HEAD_DIM = 64
SSD_STATE = 128
SSD_CONV = 4
SSD_CHUNK = 128
SSD_INNER = 512
SSD_CONV_DIM = 1024
GLA_HEADS = 4
GLA_HEAD_K = 64
GLA_HEAD_V = 128
GLA_GATE_RANK = 16
GLA_GATE_TAU = 16.0
GLA_CHUNK = 64
GLA_SUB = 16
GLA_KD = 256
GLA_VD = 512
D_FF = 5632
FFN_CONV = 3
HALO = 16

P_WIDTH = 4096
C_CQ, C_Z, C_GV, C_GR, C_XBC, C_CKV, C_GQ, C_GK, C_KPE, C_SM = (
    0, 512, 1024, 1536, 2048, 3072, 3328, 3584, 3840, 3968)
SM_DT = 0
SM_GLOW = 8

VMEM_LIMIT = 52 * 1024 * 1024


def _cparams(sem):
    return pltpu.CompilerParams(dimension_semantics=sem, vmem_limit_bytes=VMEM_LIMIT)


def _rms(x, g):
    var = jnp.mean(x * x, axis=-1, keepdims=True)
    return x * lax.rsqrt(var + EPS) * g


def _softplus(x):
    return jnp.maximum(x, 0.0) + jnp.log(1.0 + jnp.exp(-jnp.abs(x)))


def _silu(x):
    return x * jax.nn.sigmoid(x)


def _dot(a, b):
    return jnp.dot(a, b, preferred_element_type=F32)


def _dot_nt(a, b):
    return lax.dot_general(a, b, (((1,), (1,)), ((), ())), preferred_element_type=F32)


def _dot_tn(a, b):
    return lax.dot_general(a, b, (((0,), (0,)), ((), ())), preferred_element_type=F32)


def _iota(shape, axis):
    return lax.broadcasted_iota(jnp.int32, shape, axis)


def _in_proj_body(x_ref, g_ref, w_ref, o_ref, *, tn):
    h = _rms(x_ref[...], g_ref[...]).astype(BF16)
    for j in range(o_ref.shape[1] // tn):
        o_ref[:, j * tn:(j + 1) * tn] = _dot(h, w_ref[:, j * tn:(j + 1) * tn])


def _in_proj(x2d, gains, w_all, layer, tm=512, tn=1024):
    t, d = x2d.shape
    tm = min(tm, t)
    return pl.pallas_call(
        functools.partial(_in_proj_body, tn=tn),
        grid=(t // tm,),
        in_specs=[
            pl.BlockSpec((tm, d), lambda i: (i, 0)),
            pl.BlockSpec((None, 1, d), lambda i: (layer, 0, 0)),
            pl.BlockSpec((None, d, P_WIDTH), lambda i: (layer, 0, 0), pipeline_mode=pl.Buffered(1)),
        ],
        out_specs=pl.BlockSpec((tm, P_WIDTH), lambda i: (i, 0)),
        out_shape=jax.ShapeDtypeStruct((t, P_WIDTH), F32),
        compiler_params=_cparams(("parallel",)),
        name="in_proj",
    )(x2d, gains, w_all)


def _rope(blk, cc, ss):
    return blk * cc + pltpu.roll(blk, MLA_ROPE // 2, axis=1) * ss


def _mla_proj_body(cq_ref, ckv_ref, kpe_ref, qn_ref, kvn_ref, wq_ref, wkv_ref, cc_ref, ss_ref,
                   qt_out, k_out, vt_out, *, tk):
    cc = cc_ref[...]
    ss = ss_ref[...]
    hq = _rms(cq_ref[...], qn_ref[...]).astype(BF16)
    hkv = _rms(ckv_ref[...], kvn_ref[...]).astype(BF16)
    kpe = _rope(kpe_ref[...], cc, ss).astype(BF16)
    scale = (MLA_NOPE + MLA_ROPE) ** -0.5 * LOG2_E
    ones = jnp.ones((MLA_VE - MLA_V, tk), BF16)
    qf = _dot(hq, wq_ref[...])
    kvf = _dot(hkv, wkv_ref[...])
    tm = qf.shape[0]
    for h in range(MLA_HEADS):
        base = h * MLA_QK
        q_pe = _rope(qf[:, base + MLA_NOPE:base + MLA_QK], cc, ss)
        q_h = jnp.concatenate([qf[:, base:base + MLA_NOPE], q_pe], axis=1) * scale
        qt_out[h] = q_h.T.astype(BF16)
        k_out[h] = jnp.concatenate([kvf[:, base:base + MLA_NOPE].astype(BF16), kpe], axis=1)
        v_h = kvf[:, base + MLA_NOPE:base + MLA_QK]
        for j in range(tm // tk):
            vt_out[h, j, 0:MLA_V, :] = v_h[j * tk:(j + 1) * tk, :].T.astype(BF16)
            vt_out[h, j, MLA_V:MLA_VE, :] = ones


def _mla_proj(proj, q_norm, kv_norm, wq_all, wkv_all, cc, ss, layer, batch, seq, tm=512, tk=256):
    t = proj.shape[0]
    tm = min(tm, seq)
    tk = min(tk, tm)
    nsb = seq // tm
    H = MLA_HEADS
    return pl.pallas_call(
        functools.partial(_mla_proj_body, tk=tk),
        grid=(t // tm,),
        in_specs=[
            pl.BlockSpec((tm, MLA_Q_RANK), lambda i: (i, C_CQ // MLA_Q_RANK)),
            pl.BlockSpec((tm, MLA_KV_RANK), lambda i: (i, C_CKV // MLA_KV_RANK)),
            pl.BlockSpec((tm, 128), lambda i: (i, C_KPE // 128)),
            pl.BlockSpec((None, 1, MLA_Q_RANK), lambda i: (layer, 0, 0)),
            pl.BlockSpec((None, 1, MLA_KV_RANK), lambda i: (layer, 0, 0)),
            pl.BlockSpec((None, MLA_Q_RANK, H * MLA_QK), lambda i: (layer, 0, 0)),
            pl.BlockSpec((None, MLA_KV_RANK, H * MLA_QK), lambda i: (layer, 0, 0)),
            pl.BlockSpec((tm, 128), lambda i: (i % nsb, 0)),
            pl.BlockSpec((tm, 128), lambda i: (i % nsb, 0)),
        ],
        out_specs=[
            pl.BlockSpec((None, H, MLA_QK, tm), lambda i: (i // nsb, 0, 0, i % nsb)),
            pl.BlockSpec((None, H, tm, MLA_QK), lambda i: (i // nsb, 0, i % nsb, 0)),
            pl.BlockSpec((None, H, tm // tk, MLA_VE, tk), lambda i: (i // nsb, 0, i % nsb, 0, 0)),
        ],
        out_shape=[
            jax.ShapeDtypeStruct((batch, H, MLA_QK, seq), BF16),
            jax.ShapeDtypeStruct((batch, H, seq, MLA_QK), BF16),
            jax.ShapeDtypeStruct((batch, H, seq // tk, MLA_VE, tk), BF16),
        ],
        compiler_params=_cparams(("parallel",)),
        name="mla_proj",
    )(proj, proj, proj, q_norm, kv_norm, wq_all, wkv_all, cc, ss)


def _attn_body(qt_ref, k_ref, vt_ref, o_ref, acc_scr, *, tq, hp):
    qi = pl.program_id(2)
    qts = [qt_ref[h] for h in range(hp)]

    def step(kb, ms, diagonal):
        start = pl.multiple_of(kb * tq, tq)
        ss = [_dot(k_ref[h, pl.ds(start, tq), :], qts[h]) for h in range(hp)]
        out = []
        for h in range(hp):
            s = ss[h]
            if diagonal:
                s = jnp.where(_iota((tq, tq), 0) <= _iota((tq, tq), 1), s, -jnp.inf)
            m_new = jnp.maximum(ms[h], jnp.max(s, axis=0, keepdims=True))
            p = jnp.exp2(s - m_new).astype(BF16)
            acc_scr[h] = jnp.exp2(ms[h] - m_new) * acc_scr[h] + _dot(vt_ref[h, kb], p)
            out.append(m_new)
        return tuple(out)

    acc_scr[...] = jnp.zeros_like(acc_scr)
    init = tuple(jnp.full((1, tq), -jnp.inf, F32) for _ in range(hp))
    ms = lax.fori_loop(0, qi, lambda kb, c: step(kb, c, False), init)
    step(qi, ms, True)
    outs = []
    for h in range(hp):
        acc = acc_scr[h]
        outs.append((acc[0:MLA_V, :] / acc[MLA_V:MLA_V + 1, :]).T)
    o_ref[...] = jnp.concatenate(outs, axis=1).astype(BF16)


def _mla_attn(qt, k, vt, hp=8):
    batch, heads, _, seq = qt.shape
    tq = vt.shape[-1]
    return pl.pallas_call(
        functools.partial(_attn_body, tq=tq, hp=hp),
        grid=(batch, heads // hp, seq // tq),
        in_specs=[
            pl.BlockSpec((None, hp, MLA_QK, tq), lambda b, h, i: (b, h, 0, i)),
            pl.BlockSpec((None, hp, seq, MLA_QK), lambda b, h, i: (b, h, 0, 0)),
            pl.BlockSpec((None, hp, seq // tq, MLA_VE, tq), lambda b, h, i: (b, h, 0, 0, 0)),
        ],
        out_specs=pl.BlockSpec((None, tq, hp * MLA_V), lambda b, h, i: (b, i, h)),
        out_shape=jax.ShapeDtypeStruct((batch, seq, heads * MLA_V), BF16),
        scratch_shapes=[pltpu.VMEM((hp, MLA_VE, tq), F32)],
        compiler_params=_cparams(("parallel", "parallel", "arbitrary")),
        name="mla_attn",
    )(qt, k, vt)


def _ssd_body(z_ref, xbc_ref, sm_ref, cw_ref, cb_ref, dtb_ref, dtbt_ref, alog_ref, alogt_ref,
              dexp_ref, ng_ref, o_ref, xb_scr, st_scr):
    L, N, P = SSD_CHUNK, SSD_STATE, SSD_HEAD_DIM
    pad = 8

    @pl.when(pl.program_id(1) == 0)
    def _():
        xb_scr[0:pad, :] = jnp.zeros((pad, SSD_CONV_DIM), F32)
        st_scr[...] = jnp.zeros_like(st_scr)

    xb_scr[pad:pad + L, :] = xbc_ref[...]
    cw = cw_ref[...]
    conv = cb_ref[...] + cw[3:4] * xb_scr[pad:pad + L, :]
    for tap in range(1, SSD_CONV):
        conv = conv + cw[3 - tap:4 - tap] * xb_scr[pad - tap:pad - tap + L, :]
    xb_scr[0:pad, :] = xb_scr[L:L + pad, :]
    act = _silu(conv)
    xs = act[:, :SSD_INNER]
    bm = act[:, SSD_INNER:SSD_INNER + 2 * N]
    cm = act[:, SSD_INNER + 2 * N:]

    sm = sm_ref[...]
    lane = _iota((1, 128), 1)
    nega = jnp.where(lane < SSD_HEADS, -jnp.exp(alog_ref[...]), 0.0)
    dt = _softplus(sm + dtb_ref[...])
    a = dt * nega
    smt = sm.T[0:SSD_HEADS, :]
    dtt = _softplus(smt + dtbt_ref[...])
    at = dtt * (-jnp.exp(alogt_ref[...]))
    row = _iota((L, L), 0)
    col = _iota((L, L), 1)
    causal = row >= col
    acum = jnp.dot(causal.astype(F32), a, precision=HIGHEST, preferred_element_type=F32)
    acumt = jnp.dot(at, (row <= col).astype(F32), precision=HIGHEST, preferred_element_type=F32)

    colb = [jnp.broadcast_to(acum[:, h:h + 1], (L, L)) for h in range(SSD_HEADS)]
    dtb = [jnp.broadcast_to(dt[:, h:h + 1], (L, L)) for h in range(SSD_HEADS)]
    left = _iota((L, 2 * P), 1) < P

    def per_head_lanes(vals):
        return jnp.concatenate([jnp.where(left, vals[2 * p], vals[2 * p + 1])
                                for p in range(SSD_HEADS // 2)], axis=1)

    acum_e = per_head_lanes(colb)
    dt_e = per_head_lanes(dtb)
    alast_e = acum_e[L - 1:L, :]
    x_dt = xs * dt_e
    xd = (x_dt * jnp.exp(alast_e - acum_e)).astype(BF16)
    x_dt16 = x_dt.astype(BF16)

    ht = st_scr[...]
    ht16 = ht.astype(BF16)
    heads_per_group = SSD_HEADS // 2
    g_mats, c_mats, st_new = [], [], []
    for g in range(2):
        b_g = bm[:, g * N:(g + 1) * N]
        c_g = cm[:, g * N:(g + 1) * N]
        c_mats.append(c_g)
        g_mats.append(_dot_nt(c_g.astype(BF16), b_g.astype(BF16)))
        width = heads_per_group * P
        st_new.append(_dot(b_g.T.astype(BF16), xd[:, g * width:(g + 1) * width]))
    ys = []
    for p in range(SSD_HEADS // 2):
        g = (2 * p) // heads_per_group
        rhs = jnp.concatenate([x_dt16[:, 2 * p * P:(2 * p + 2) * P],
                               ht16[:, 2 * p * P:(2 * p + 2) * P]], axis=0)
        pair = []
        for h in (2 * p, 2 * p + 1):
            rowb = jnp.broadcast_to(acumt[h:h + 1, :], (L, L))
            decay = jnp.where(causal, jnp.exp(jnp.minimum(colb[h] - rowb, 0.0)), 0.0)
            lhs = jnp.concatenate([g_mats[g] * decay, c_mats[g] * jnp.exp(colb[h])], axis=1)
            pair.append(_dot(lhs.astype(BF16), rhs))
        ys.append(jnp.where(left, pair[0], pair[1]))
    y = jnp.concatenate(ys, axis=1) + xs * dexp_ref[...]
    st_scr[...] = ht * jnp.exp(alast_e) + jnp.concatenate(st_new, axis=1)
    o_ref[...] = _rms(y * _silu(z_ref[...]), ng_ref[...]).astype(BF16)


def _ssd(proj, prm, layer, batch, seq):
    L = SSD_CHUNK
    nc = seq // L
    t = proj.shape[0]
    row = lambda b, c: b * nc + c
    lay3 = lambda shape: pl.BlockSpec((None,) + shape, lambda b, c: (layer, 0, 0))
    return pl.pallas_call(
        _ssd_body,
        grid=(batch, nc),
        in_specs=[
            pl.BlockSpec((L, SSD_INNER), lambda b, c: (row(b, c), C_Z // SSD_INNER)),
            pl.BlockSpec((L, SSD_CONV_DIM), lambda b, c: (row(b, c), C_XBC // SSD_CONV_DIM)),
            pl.BlockSpec((L, 128), lambda b, c: (row(b, c), C_SM // 128)),
            lay3((SSD_CONV, SSD_CONV_DIM)),
            lay3((1, SSD_CONV_DIM)),
            lay3((1, 128)),
            lay3((SSD_HEADS, 1)),
            lay3((1, 128)),
            lay3((SSD_HEADS, 1)),
            lay3((1, SSD_INNER)),
            lay3((1, SSD_INNER)),
        ],
        out_specs=pl.BlockSpec((L, SSD_INNER), lambda b, c: (row(b, c), 0)),
        out_shape=jax.ShapeDtypeStruct((t, SSD_INNER), BF16),
        scratch_shapes=[
            pltpu.VMEM((L + 8, SSD_CONV_DIM), F32),
            pltpu.VMEM((SSD_STATE, SSD_INNER), F32),
        ],
        compiler_params=_cparams(("parallel", "arbitrary")),
        name="ssd",
    )(proj, proj, proj, prm["ssd_conv_w"], prm["ssd_conv_b"], prm["ssd_dtb"], prm["ssd_dtb_t"],
      prm["ssd_alog"], prm["ssd_alog_t"], prm["ssd_dexp"], prm["ssd_norm"])


def _gla_body(q_ref, k_ref, v_ref, r_ref, sm_ref, wgk_ref, bgk_ref, gn_ref, seg_ref, o_ref,
              g_scr, st_scr, *, nch):
    L, C, H, K, V = GLA_CHUNK, GLA_SUB, GLA_HEADS, GLA_HEAD_K, GLA_HEAD_V
    nb = L // C
    T = nch * L

    @pl.when(pl.program_id(1) == 0)
    def _():
        st_scr[...] = jnp.zeros_like(st_scr)

    xg = _dot(sm_ref[...].astype(BF16), wgk_ref[...]) + bgk_ref[...]
    log_a = -_softplus(-xg) * (1.0 / GLA_GATE_TAU)
    rr = _iota((T, T), 0)
    cc = _iota((T, T), 1)
    tril = jnp.where(rr // L == cc // L, (rr >= cc).astype(F32), 0.0)
    g_all = jnp.dot(tril, log_a, precision=HIGHEST, preferred_element_type=F32)
    g_scr[...] = g_all
    qs_all = q_ref[...] * (K ** -0.5)

    head_rows_k = _iota((H * L, GLA_KD), 0) // L == _iota((H * L, GLA_KD), 1) // K
    head_rows_v = _iota((H * L, GLA_VD), 0) // L == _iota((H * L, GLA_VD), 1) // V
    rowblk = _iota((L, H * L), 0) // C
    colblk = (_iota((L, H * L), 1) % L) // C
    lrow = _iota((C, GLA_KD), 0)

    o_intra, q_dec, st_upd, g_last = [], [], [], []
    for ch in range(nch):
        base = ch * L
        g = g_all[base:base + L, :]
        qs = qs_all[base:base + L, :]
        k = k_ref[base:base + L, :]
        ends = [g_scr[base + C * j + C - 1:base + C * j + C, :] for j in range(nb)]
        end_b = jnp.concatenate([jnp.broadcast_to(e, (C, GLA_KD)) for e in ends], axis=0)
        kd = k * jnp.exp(end_b - g)
        k_exp = jnp.where(head_rows_k, jnp.concatenate([kd] * H, axis=0), 0.0).astype(BF16)

        attn = jnp.zeros((L, H * L), F32)
        for j in range(nb - 1):
            qd = qs * jnp.exp(jnp.minimum(g - ends[j], 0.0))
            aj = _dot_nt(qd.astype(BF16), k_exp)
            attn = jnp.where(colblk == j, jnp.where(rowblk > j, aj, 0.0), attn)

        pieces = []
        for j in range(C):
            blocks = []
            for i in range(nb):
                r0 = C * i
                gj = g_scr[base + r0 + j:base + r0 + j + 1, :]
                kj = k_ref[base + r0 + j:base + r0 + j + 1, :]
                rel = jnp.exp(jnp.minimum(g[r0:r0 + C, :] - gj, 0.0))
                blocks.append(jnp.where(lrow >= j, qs[r0:r0 + C, :] * kj * rel, 0.0))
            pieces.append(jnp.concatenate(blocks, axis=0).astype(BF16))
        a_diag = _dot(jnp.concatenate(pieces, axis=1), seg_ref[...])
        attn = jnp.where(colblk == rowblk, a_diag, attn)

        v = v_ref[base:base + L, :]
        v_bd = jnp.where(head_rows_v, jnp.concatenate([v] * H, axis=0), 0.0).astype(BF16)
        o_intra.append(_dot(attn.astype(BF16), v_bd))
        q_dec.append((qs * jnp.exp(g)).astype(BF16))
        g_last.append(ends[nb - 1])
        k_dec = (k * jnp.exp(ends[nb - 1] - g)).astype(BF16)
        st_upd.append(_dot_tn(v.astype(BF16), k_dec))

    same_head = _iota((GLA_VD, GLA_KD), 0) // V == _iota((GLA_VD, GLA_KD), 1) // K
    st = st_scr[...]
    for ch in range(nch):
        base = ch * L
        o = o_intra[ch] + _dot_nt(q_dec[ch], st.astype(BF16))
        st = st * jnp.exp(g_last[ch]) + jnp.where(same_head, st_upd[ch], 0.0)
        normed = jnp.concatenate([_rms(o[:, h * V:(h + 1) * V], gn_ref[...]) for h in range(H)], axis=1)
        o_ref[base:base + L, :] = (normed * _silu(r_ref[base:base + L, :])).astype(BF16)
    st_scr[...] = st


def _gla(proj, prm, layer, batch, seq, nch=4):
    L = GLA_CHUNK * nch
    nc = seq // L
    t = proj.shape[0]
    row = lambda b, c: b * nc + c
    lay3 = lambda shape: pl.BlockSpec((None,) + shape, lambda b, c: (layer, 0, 0))
    seg = prm["gla_seg"]
    return pl.pallas_call(
        functools.partial(_gla_body, nch=nch),
        grid=(batch, nc),
        in_specs=[
            pl.BlockSpec((L, GLA_KD), lambda b, c: (row(b, c), C_GQ // GLA_KD)),
            pl.BlockSpec((L, GLA_KD), lambda b, c: (row(b, c), C_GK // GLA_KD)),
            pl.BlockSpec((L, GLA_VD), lambda b, c: (row(b, c), C_GV // GLA_VD)),
            pl.BlockSpec((L, GLA_VD), lambda b, c: (row(b, c), C_GR // GLA_VD)),
            pl.BlockSpec((L, 128), lambda b, c: (row(b, c), C_SM // 128)),
            lay3((128, GLA_KD)),
            lay3((1, GLA_KD)),
            lay3((1, GLA_HEAD_V)),
            pl.BlockSpec(seg.shape, lambda b, c: (0, 0)),
        ],
        out_specs=pl.BlockSpec((L, GLA_VD), lambda b, c: (row(b, c), 0)),
        out_shape=jax.ShapeDtypeStruct((t, GLA_VD), BF16),
        scratch_shapes=[
            pltpu.VMEM((L, GLA_KD), F32),
            pltpu.VMEM((GLA_VD, GLA_KD), F32),
        ],
        compiler_params=_cparams(("parallel", "arbitrary")),
        name="gla",
    )(proj, proj, proj, proj, proj, prm["gla_wgk"], prm["gla_bgk"], prm["gla_norm"], seg)


def _out_proj_body(x_ref, ya_ref, yb_ref, yc_ref, w_ref, o_ref, *, tn):
    y = jnp.concatenate([ya_ref[...], yb_ref[...], yc_ref[...]], axis=1)
    for j in range(o_ref.shape[1] // tn):
        cols = slice(j * tn, (j + 1) * tn)
        o_ref[:, cols] = x_ref[:, cols] + _dot(y, w_ref[:, cols])


def _out_proj(x2d, y_mla, y_ssd, y_gla, w_all, layer, tm=512, tn=1024):
    t, d = x2d.shape
    da, db, dc = y_mla.shape[1], y_ssd.shape[1], y_gla.shape[1]
    tm = min(tm, t)
    return pl.pallas_call(
        functools.partial(_out_proj_body, tn=tn),
        grid=(t // tm,),
        in_specs=[
            pl.BlockSpec((tm, d), lambda i: (i, 0)),
            pl.BlockSpec((tm, da), lambda i: (i, 0)),
            pl.BlockSpec((tm, db), lambda i: (i, 0)),
            pl.BlockSpec((tm, dc), lambda i: (i, 0)),
            pl.BlockSpec((None, da + db + dc, d), lambda i: (layer, 0, 0), pipeline_mode=pl.Buffered(1)),
        ],
        out_specs=pl.BlockSpec((tm, d), lambda i: (i, 0)),
        out_shape=jax.ShapeDtypeStruct((t, d), F32),
        compiler_params=_cparams(("parallel",)),
        name="out_proj",
    )(x2d, y_mla, y_ssd, y_gla, w_all)


def _ffn_body(x_ref, xh_ref, g_ref, wg_ref, wu_ref, cw_ref, cb_ref, wd_ref, o_ref,
              h_scr, gate_scr, *, tm, tiles_per_seq, n_out_chunks):
    i = pl.program_id(0)
    f = pl.program_id(1)

    @pl.when(f == 0)
    def _():
        x = x_ref[...]
        h_scr[0:HALO, :] = _rms(xh_ref[...], g_ref[...]).astype(BF16)
        h_scr[HALO:HALO + tm, :] = _rms(x, g_ref[...]).astype(BF16)
        o_ref[...] = x

    gate = _dot(h_scr[...], wg_ref[...])
    keep = (i % tiles_per_seq != 0).astype(F32)
    gate_scr[0:HALO, :] = gate[0:HALO, :] * keep
    gate_scr[HALO:HALO + tm, :] = gate[HALO:, :]
    cw = cw_ref[...]
    conv = cb_ref[...] + cw[FFN_CONV - 1:FFN_CONV] * gate[HALO:, :]
    for tap in range(1, FFN_CONV):
        conv = conv + cw[FFN_CONV - 1 - tap:FFN_CONV - tap] * gate_scr[HALO - tap:HALO - tap + tm, :]
    act = (_silu(conv) * _dot(h_scr[HALO:HALO + tm, :], wu_ref[...])).astype(BF16)
    dn = o_ref.shape[1] // n_out_chunks
    for c in range(n_out_chunks):
        o_ref[:, c * dn:(c + 1) * dn] += _dot(act, wd_ref[:, c * dn:(c + 1) * dn])


def _ffn(x2d, prm, layer, seq, tm=512, tf=512):
    t, d = x2d.shape
    tm = min(tm, seq)
    tiles_per_seq = seq // tm
    halo_blocks = tm // HALO
    lay = lambda shape, idx: pl.BlockSpec((None,) + shape, idx)
    return pl.pallas_call(
        functools.partial(_ffn_body, tm=tm, tiles_per_seq=tiles_per_seq, n_out_chunks=4),
        grid=(t // tm, D_FF // tf),
        in_specs=[
            pl.BlockSpec((tm, d), lambda i, f: (i, 0)),
            pl.BlockSpec((HALO, d), lambda i, f: (jnp.maximum(i * halo_blocks - 1, 0), 0)),
            lay((1, d), lambda i, f: (layer, 0, 0)),
            lay((d, tf), lambda i, f: (layer, 0, f)),
            lay((d, tf), lambda i, f: (layer, 0, f)),
            lay((FFN_CONV, tf), lambda i, f: (layer, 0, f)),
            lay((1, tf), lambda i, f: (layer, 0, f)),
            lay((tf, d), lambda i, f: (layer, f, 0)),
        ],
        out_specs=pl.BlockSpec((tm, d), lambda i, f: (i, 0)),
        out_shape=jax.ShapeDtypeStruct((t, d), F32),
        scratch_shapes=[
            pltpu.VMEM((HALO + tm, d), BF16),
            pltpu.VMEM((HALO + tm, tf), F32),
        ],
        compiler_params=_cparams(("parallel", "arbitrary")),
        name="ffn",
    )(x2d, x2d, prm["ffn_norm"], prm["ffn_w_gate"], prm["ffn_w_up"], prm["ffn_dw_w"],
      prm["ffn_dw_b"], prm["ffn_w_down"])


def _final_norm_body(x_ref, g_ref, o_ref):
    o_ref[...] = _rms(x_ref[...], g_ref[...])


def _final_norm(x2d, gain, tm=512):
    t, d = x2d.shape
    tm = min(tm, t)
    return pl.pallas_call(
        _final_norm_body,
        grid=(t // tm,),
        in_specs=[pl.BlockSpec((tm, d), lambda i: (i, 0)), pl.BlockSpec((1, d), lambda i: (0, 0))],
        out_specs=pl.BlockSpec((tm, d), lambda i: (i, 0)),
        out_shape=jax.ShapeDtypeStruct((t, d), F32),
        compiler_params=_cparams(("parallel",)),
        name="final_norm",
    )(x2d, gain)


_W_IN_RANGES = ((0, 512), (832, 1344), (2888, 3400), (3416, 3928), (1344, 2368), (512, 768),
                (2376, 2632), (2632, 2888), (768, 832), (768, 832), (2368, 2376), (3400, 3416))


def _reorder_w_in_body(w_ref, o_ref):
    w = w_ref[...]
    parts = [w[:, a:b] for a, b in _W_IN_RANGES]
    used = sum(b - a for a, b in _W_IN_RANGES)
    parts.append(jnp.zeros((w.shape[0], P_WIDTH - used), w.dtype))
    o_ref[...] = jnp.concatenate(parts, axis=1).astype(BF16)


def _reorder_w_in(w_in, tk=256):
    depth, d, d_in = w_in.shape
    return pl.pallas_call(
        _reorder_w_in_body,
        grid=(depth, d // tk),
        in_specs=[pl.BlockSpec((None, tk, d_in), lambda l, i: (l, i, 0))],
        out_specs=pl.BlockSpec((None, tk, P_WIDTH), lambda l, i: (l, i, 0)),
        out_shape=jax.ShapeDtypeStruct((depth, d, P_WIDTH), BF16),
        compiler_params=_cparams(("parallel", "parallel")),
        name="reorder_w_in",
    )(w_in)


def _prepare(seq, attn_norm, w_in, mla_q_norm, mla_w_uq, mla_kv_norm, mla_w_ukv,
             ssd_conv_w, ssd_conv_b, ssd_dt_bias, ssd_a_log, ssd_d, ssd_norm,
             gla_w_gk, gla_b_gk, gla_norm, w_out,
             ffn_norm, ffn_w_gate, ffn_w_up, ffn_dw_w, ffn_dw_b, ffn_w_down):
    depth = w_in.shape[0]
    w_in_r = _reorder_w_in(w_in)

    wq = mla_w_uq.reshape(depth, MLA_Q_RANK, MLA_HEADS, MLA_NOPE + MLA_ROPE)
    wq = jnp.concatenate([wq, wq[..., MLA_NOPE:]], axis=-1)
    wq = wq.reshape(depth, MLA_Q_RANK, MLA_HEADS * MLA_QK).astype(BF16)
    wkv = mla_w_ukv.astype(BF16)

    inv_freq = ROPE_BASE ** (-jnp.arange(0, MLA_ROPE, 2, dtype=F32) / MLA_ROPE)
    ang = jnp.arange(seq, dtype=F32)[:, None] * inv_freq[None, :]
    cos, sin = jnp.cos(ang), jnp.sin(ang)
    zeros = jnp.zeros((seq, MLA_ROPE), F32)
    cc = jnp.concatenate([cos, cos, zeros], axis=1)
    ss = jnp.concatenate([-sin, sin, zeros], axis=1)

    pad_lanes = lambda a: jnp.pad(a, ((0, 0), (0, 128 - a.shape[1])))[:, None, :]
    wgk = jnp.zeros((depth, 128, GLA_KD), F32).at[:, SM_GLOW:SM_GLOW + GLA_GATE_RANK, :].set(gla_w_gk)

    rows = jnp.arange(GLA_SUB * GLA_KD)
    cols = jnp.arange(GLA_HEADS * GLA_CHUNK)
    seg = ((rows[:, None] % GLA_KD) // GLA_HEAD_K == cols[None, :] // GLA_CHUNK) & (
        rows[:, None] // GLA_KD == cols[None, :] % GLA_SUB)

    return dict(
        attn_norm=attn_norm[:, None, :], w_in=w_in_r,
        mla_q_norm=mla_q_norm[:, None, :], mla_kv_norm=mla_kv_norm[:, None, :], wq=wq, wkv=wkv,
        cc=cc, ss=ss,
        ssd_conv_w=ssd_conv_w, ssd_conv_b=ssd_conv_b[:, None, :],
        ssd_dtb=pad_lanes(ssd_dt_bias), ssd_dtb_t=ssd_dt_bias[:, :, None],
        ssd_alog=pad_lanes(ssd_a_log), ssd_alog_t=ssd_a_log[:, :, None],
        ssd_dexp=jnp.repeat(ssd_d, SSD_HEAD_DIM, axis=1)[:, None, :], ssd_norm=ssd_norm[:, None, :],
        gla_wgk=wgk.astype(BF16), gla_bgk=gla_b_gk[:, None, :], gla_norm=gla_norm[:, None, :],
        gla_seg=seg.astype(BF16),
        w_out=w_out.astype(BF16),
        ffn_norm=ffn_norm[:, None, :], ffn_w_gate=ffn_w_gate.astype(BF16),
        ffn_w_up=ffn_w_up.astype(BF16), ffn_dw_w=ffn_dw_w, ffn_dw_b=ffn_dw_b[:, None, :],
        ffn_w_down=ffn_w_down.astype(BF16),
    )


def kernel(x, attn_norm, w_in, mla_q_norm, mla_w_uq, mla_kv_norm, mla_w_ukv, ssd_conv_w, ssd_conv_b,
           ssd_dt_bias, ssd_a_log, ssd_d, ssd_norm, gla_w_gk, gla_b_gk, gla_norm, w_out, ffn_norm,
           ffn_w_gate, ffn_w_up, ffn_dw_w, ffn_dw_b, ffn_w_down, final_norm):
    batch, seq, d = x.shape
    prm = _prepare(seq, attn_norm, w_in, mla_q_norm, mla_w_uq, mla_kv_norm, mla_w_ukv,
                   ssd_conv_w, ssd_conv_b, ssd_dt_bias, ssd_a_log, ssd_d, ssd_norm,
                   gla_w_gk, gla_b_gk, gla_norm, w_out,
                   ffn_norm, ffn_w_gate, ffn_w_up, ffn_dw_w, ffn_dw_b, ffn_w_down)
    x2d = x.reshape(batch * seq, d)
    for layer in range(w_in.shape[0]):
        proj = _in_proj(x2d, prm["attn_norm"], prm["w_in"], layer)
        qt, k, vt = _mla_proj(proj, prm["mla_q_norm"], prm["mla_kv_norm"], prm["wq"], prm["wkv"],
                              prm["cc"], prm["ss"], layer, batch, seq)
        y_mla = _mla_attn(qt, k, vt).reshape(batch * seq, MLA_HEADS * MLA_V)
        y_ssd = _ssd(proj, prm, layer, batch, seq)
        y_gla = _gla(proj, prm, layer, batch, seq)
        x2d = _out_proj(x2d, y_mla, y_ssd, y_gla, prm["w_out"], layer)
        x2d = _ffn(x2d, prm, layer, seq)
    return _final_norm(x2d, final_norm[None, :]).reshape(batch, seq, d)
```

```python
import functools

import jax
import jax.numpy as jnp
from jax import lax
from jax.experimental import pallas as pl
from jax.experimental.pallas import tpu as pltpu

F32 = jnp.float32
BF16 = jnp.bfloat16
HIGHEST = lax.Precision.HIGHEST

D_MODEL = 2048
DEPTH = 4
EPS = 1e-6
MLA_HEADS = 8
MLA_Q_RANK = 512
MLA_KV_RANK = 256
MLA_NOPE = 128
MLA_ROPE = 64
MLA_V = 128
MLA_QK = 256
MLA_VE = MLA_V + 16
LOG2_E = 1.4426950408889634
ROPE_BASE = 10000.0
SSD_HEADS = 8
SSD_HEAD_DIM = 64
SSD_STATE = 128
SSD_CONV = 4
SSD_CHUNK = 128
SSD_INNER = 512
SSD_CONV_DIM = 1024
GLA_HEADS = 4
GLA_HEAD_K = 64
GLA_HEAD_V = 128
GLA_GATE_RANK = 16
GLA_GATE_TAU = 16.0
GLA_CHUNK = 64
GLA_SUB = 16
GLA_KD = 256
GLA_VD = 512
D_FF = 5632
FFN_CONV = 3
HALO = 16

P_WIDTH = 4096
C_CQ, C_Z, C_GV, C_GR, C_XBC, C_CKV, C_GQ, C_GK, C_KPE, C_SM = (
    0, 512, 1024, 1536, 2048, 3072, 3328, 3584, 3840, 3968)
SM_DT = 0
SM_GLOW = 8

VMEM_LIMIT = 52 * 1024 * 1024


def _cparams(sem):
    return pltpu.CompilerParams(dimension_semantics=sem, vmem_limit_bytes=VMEM_LIMIT)


def _rms(x, g):
    var = jnp.mean(x * x, axis=-1, keepdims=True)
    return x * lax.rsqrt(var + EPS) * g


def _softplus(x):
    return jnp.maximum(x, 0.0) + jnp.log(1.0 + jnp.exp(-jnp.abs(x)))


def _silu(x):
    return x * jax.nn.sigmoid(x)


def _dot(a, b):
    return jnp.dot(a, b, preferred_element_type=F32)


def _dot_nt(a, b):
    return lax.dot_general(a, b, (((1,), (1,)), ((), ())), preferred_element_type=F32)


def _dot_tn(a, b):
    return lax.dot_general(a, b, (((0,), (0,)), ((), ())), preferred_element_type=F32)


def _iota(shape, axis):
    return lax.broadcasted_iota(jnp.int32, shape, axis)


def _in_proj_body(x_ref, g_ref, w_ref, o_ref, *, tn):
    h = _rms(x_ref[...], g_ref[...]).astype(BF16)
    for j in range(o_ref.shape[1] // tn):
        o_ref[:, j * tn:(j + 1) * tn] = _dot(h, w_ref[:, j * tn:(j + 1) * tn])


def _in_proj(x2d, gains, w_all, layer, tm=512, tn=1024):
    t, d = x2d.shape
    tm = min(tm, t)
    return pl.pallas_call(
        functools.partial(_in_proj_body, tn=tn),
        grid=(t // tm,),
        in_specs=[
            pl.BlockSpec((tm, d), lambda i: (i, 0)),
            pl.BlockSpec((None, 1, d), lambda i: (layer, 0, 0)),
            pl.BlockSpec((None, d, P_WIDTH), lambda i: (layer, 0, 0), pipeline_mode=pl.Buffered(1)),
        ],
        out_specs=pl.BlockSpec((tm, P_WIDTH), lambda i: (i, 0)),
        out_shape=jax.ShapeDtypeStruct((t, P_WIDTH), F32),
        compiler_params=_cparams(("parallel",)),
        name="in_proj",
    )(x2d, gains, w_all)


def _rope(blk, cc, ss):
    return blk * cc + pltpu.roll(blk, MLA_ROPE // 2, axis=1) * ss


def _mla_proj_body(cq_ref, ckv_ref, kpe_ref, qn_ref, kvn_ref, wq_ref, wkv_ref, cc_ref, ss_ref,
                   qt_out, k_out, vt_out, *, tk):
    cc = cc_ref[...]
    ss = ss_ref[...]
    hq = _rms(cq_ref[...], qn_ref[...]).astype(BF16)
    hkv = _rms(ckv_ref[...], kvn_ref[...]).astype(BF16)
    kpe = _rope(kpe_ref[...], cc, ss).astype(BF16)
    scale = (MLA_NOPE + MLA_ROPE) ** -0.5 * LOG2_E
    ones = jnp.ones((MLA_VE - MLA_V, tk), BF16)
    qf = _dot(hq, wq_ref[...])
    kvf = _dot(hkv, wkv_ref[...])
    tm = qf.shape[0]
    for h in range(MLA_HEADS):
        base = h * MLA_QK
        q_pe = _rope(qf[:, base + MLA_NOPE:base + MLA_QK], cc, ss)
        q_h = jnp.concatenate([qf[:, base:base + MLA_NOPE], q_pe], axis=1) * scale
        qt_out[h] = q_h.T.astype(BF16)
        k_out[h] = jnp.concatenate([kvf[:, base:base + MLA_NOPE].astype(BF16), kpe], axis=1)
        v_h = kvf[:, base + MLA_NOPE:base + MLA_QK]
        for j in range(tm // tk):
            vt_out[h, j, 0:MLA_V, :] = v_h[j * tk:(j + 1) * tk, :].T.astype(BF16)
            vt_out[h, j, MLA_V:MLA_VE, :] = ones


def _mla_proj(proj, q_norm, kv_norm, wq_all, wkv_all, cc, ss, layer, batch, seq, tm=512, tk=256):
    t = proj.shape[0]
    tm = min(tm, seq)
    tk = min(tk, tm)
    nsb = seq // tm
    H = MLA_HEADS
    return pl.pallas_call(
        functools.partial(_mla_proj_body, tk=tk),
        grid=(t // tm,),
        in_specs=[
            pl.BlockSpec((tm, MLA_Q_RANK), lambda i: (i, C_CQ // MLA_Q_RANK)),
            pl.BlockSpec((tm, MLA_KV_RANK), lambda i: (i, C_CKV // MLA_KV_RANK)),
            pl.BlockSpec((tm, 128), lambda i: (i, C_KPE // 128)),
            pl.BlockSpec((None, 1, MLA_Q_RANK), lambda i: (layer, 0, 0)),
            pl.BlockSpec((None, 1, MLA_KV_RANK), lambda i: (layer, 0, 0)),
            pl.BlockSpec((None, MLA_Q_RANK, H * MLA_QK), lambda i: (layer, 0, 0)),
            pl.BlockSpec((None, MLA_KV_RANK, H * MLA_QK), lambda i: (layer, 0, 0)),
            pl.BlockSpec((tm, 128), lambda i: (i % nsb, 0)),
            pl.BlockSpec((tm, 128), lambda i: (i % nsb, 0)),
        ],
        out_specs=[
            pl.BlockSpec((None, H, MLA_QK, tm), lambda i: (i // nsb, 0, 0, i % nsb)),
            pl.BlockSpec((None, H, tm, MLA_QK), lambda i: (i // nsb, 0, i % nsb, 0)),
            pl.BlockSpec((None, H, tm // tk, MLA_VE, tk), lambda i: (i // nsb, 0, i % nsb, 0, 0)),
        ],
        out_shape=[
            jax.ShapeDtypeStruct((batch, H, MLA_QK, seq), BF16),
            jax.ShapeDtypeStruct((batch, H, seq, MLA_QK), BF16),
            jax.ShapeDtypeStruct((batch, H, seq // tk, MLA_VE, tk), BF16),
        ],
        compiler_params=_cparams(("parallel",)),
        name="mla_proj",
    )(proj, proj, proj, q_norm, kv_norm, wq_all, wkv_all, cc, ss)


def _attn_body(qt_ref, k_ref, vt_ref, o_ref, acc_scr, *, tq, hp):
    qi = pl.program_id(2)
    qts = [qt_ref[h] for h in range(hp)]

    def step(kb, ms, diagonal):
        start = pl.multiple_of(kb * tq, tq)
        ss = [_dot(k_ref[h, pl.ds(start, tq), :], qts[h]) for h in range(hp)]
        out = []
        for h in range(hp):
            s = ss[h]
            if diagonal:
                s = jnp.where(_iota((tq, tq), 0) <= _iota((tq, tq), 1), s, -jnp.inf)
            m_new = jnp.maximum(ms[h], jnp.max(s, axis=0, keepdims=True))
            p = jnp.exp2(s - m_new).astype(BF16)
            acc_scr[h] = jnp.exp2(ms[h] - m_new) * acc_scr[h] + _dot(vt_ref[h, kb], p)
            out.append(m_new)
        return tuple(out)

    acc_scr[...] = jnp.zeros_like(acc_scr)
    init = tuple(jnp.full((1, tq), -jnp.inf, F32) for _ in range(hp))
    ms = lax.fori_loop(0, qi, lambda kb, c: step(kb, c, False), init)
    step(qi, ms, True)
    outs = []
    for h in range(hp):
        acc = acc_scr[h]
        outs.append((acc[0:MLA_V, :] / acc[MLA_V:MLA_V + 1, :]).T)
    o_ref[...] = jnp.concatenate(outs, axis=1).astype(BF16)


def _mla_attn(qt, k, vt, hp=8):
    batch, heads, _, seq = qt.shape
    tq = vt.shape[-1]
    return pl.pallas_call(
        functools.partial(_attn_body, tq=tq, hp=hp),
        grid=(batch, heads // hp, seq // tq),
        in_specs=[
            pl.BlockSpec((None, hp, MLA_QK, tq), lambda b, h, i: (b, h, 0, i)),
            pl.BlockSpec((None, hp, seq, MLA_QK), lambda b, h, i: (b, h, 0, 0)),
            pl.BlockSpec((None, hp, seq // tq, MLA_VE, tq), lambda b, h, i: (b, h, 0, 0, 0)),
        ],
        out_specs=pl.BlockSpec((None, tq, hp * MLA_V), lambda b, h, i: (b, i, h)),
        out_shape=jax.ShapeDtypeStruct((batch, seq, heads * MLA_V), BF16),
        scratch_shapes=[pltpu.VMEM((hp, MLA_VE, tq), F32)],
        compiler_params=_cparams(("parallel", "parallel", "arbitrary")),
        name="mla_attn",
    )(qt, k, vt)


def _ssd_body(z_ref, xbc_ref, sm_ref, cw_ref, cb_ref, dtb_ref, dtbt_ref, alog_ref, alogt_ref,
              dexp_ref, ng_ref, o_ref, xb_scr, st_scr):
    L, N, P = SSD_CHUNK, SSD_STATE, SSD_HEAD_DIM
    pad = 8

    @pl.when(pl.program_id(1) == 0)
    def _():
        xb_scr[0:pad, :] = jnp.zeros((pad, SSD_CONV_DIM), F32)
        st_scr[...] = jnp.zeros_like(st_scr)

    xb_scr[pad:pad + L, :] = xbc_ref[...]
    cw = cw_ref[...]
    conv = cb_ref[...] + cw[3:4] * xb_scr[pad:pad + L, :]
    for tap in range(1, SSD_CONV):
        conv = conv + cw[3 - tap:4 - tap] * xb_scr[pad - tap:pad - tap + L, :]
    xb_scr[0:pad, :] = xb_scr[L:L + pad, :]
    act = _silu(conv)
    xs = act[:, :SSD_INNER]
    bm = act[:, SSD_INNER:SSD_INNER + 2 * N]
    cm = act[:, SSD_INNER + 2 * N:]

    sm = sm_ref[...]
    lane = _iota((1, 128), 1)
    nega = jnp.where(lane < SSD_HEADS, -jnp.exp(alog_ref[...]), 0.0)
    dt = _softplus(sm + dtb_ref[...])
    a = dt * nega
    smt = sm.T[0:SSD_HEADS, :]
    dtt = _softplus(smt + dtbt_ref[...])
    at = dtt * (-jnp.exp(alogt_ref[...]))
    row = _iota((L, L), 0)
    col = _iota((L, L), 1)
    causal = row >= col
    acum = jnp.dot(causal.astype(F32), a, precision=HIGHEST, preferred_element_type=F32)
    acumt = jnp.dot(at, (row <= col).astype(F32), precision=HIGHEST, preferred_element_type=F32)

    colb = [jnp.broadcast_to(acum[:, h:h + 1], (L, L)) for h in range(SSD_HEADS)]
    dtb = [jnp.broadcast_to(dt[:, h:h + 1], (L, L)) for h in range(SSD_HEADS)]
    left = _iota((L, 2 * P), 1) < P

    def per_head_lanes(vals):
        return jnp.concatenate([jnp.where(left, vals[2 * p], vals[2 * p + 1])
                                for p in range(SSD_HEADS // 2)], axis=1)

    acum_e = per_head_lanes(colb)
    dt_e = per_head_lanes(dtb)
    alast_e = acum_e[L - 1:L, :]
    x_dt = xs * dt_e
    xd = (x_dt * jnp.exp(alast_e - acum_e)).astype(BF16)
    x_dt16 = x_dt.astype(BF16)

    ht = st_scr[...]
    ht16 = ht.astype(BF16)
    heads_per_group = SSD_HEADS // 2
    g_mats, c_mats, st_new = [], [], []
    for g in range(2):
        b_g = bm[:, g * N:(g + 1) * N]
        c_g = cm[:, g * N:(g + 1) * N]
        c_mats.append(c_g)
        g_mats.append(_dot_nt(c_g.astype(BF16), b_g.astype(BF16)))
        width = heads_per_group * P
        st_new.append(_dot(b_g.T.astype(BF16), xd[:, g * width:(g + 1) * width]))
    ys = []
    for p in range(SSD_HEADS // 2):
        g = (2 * p) // heads_per_group
        rhs = jnp.concatenate([x_dt16[:, 2 * p * P:(2 * p + 2) * P],
                               ht16[:, 2 * p * P:(2 * p + 2) * P]], axis=0)
        pair = []
        for h in (2 * p, 2 * p + 1):
            rowb = jnp.broadcast_to(acumt[h:h + 1, :], (L, L))
            decay = jnp.where(causal, jnp.exp(jnp.minimum(colb[h] - rowb, 0.0)), 0.0)
            lhs = jnp.concatenate([g_mats[g] * decay, c_mats[g] * jnp.exp(colb[h])], axis=1)
            pair.append(_dot(lhs.astype(BF16), rhs))
        ys.append(jnp.where(left, pair[0], pair[1]))
    y = jnp.concatenate(ys, axis=1) + xs * dexp_ref[...]
    st_scr[...] = ht * jnp.exp(alast_e) + jnp.concatenate(st_new, axis=1)
    o_ref[...] = _rms(y * _silu(z_ref[...]), ng_ref[...]).astype(BF16)


def _ssd(proj, prm, layer, batch, seq):
    L = SSD_CHUNK
    nc = seq // L
    t = proj.shape[0]
    row = lambda b, c: b * nc + c
    lay3 = lambda shape: pl.BlockSpec((None,) + shape, lambda b, c: (layer, 0, 0))
    return pl.pallas_call(
        _ssd_body,
        grid=(batch, nc),
        in_specs=[
            pl.BlockSpec((L, SSD_INNER), lambda b, c: (row(b, c), C_Z // SSD_INNER)),
            pl.BlockSpec((L, SSD_CONV_DIM), lambda b, c: (row(b, c), C_XBC // SSD_CONV_DIM)),
            pl.BlockSpec((L, 128), lambda b, c: (row(b, c), C_SM // 128)),
            lay3((SSD_CONV, SSD_CONV_DIM)),
            lay3((1, SSD_CONV_DIM)),
            lay3((1, 128)),
            lay3((SSD_HEADS, 1)),
            lay3((1, 128)),
            lay3((SSD_HEADS, 1)),
            lay3((1, SSD_INNER)),
            lay3((1, SSD_INNER)),
        ],
        out_specs=pl.BlockSpec((L, SSD_INNER), lambda b, c: (row(b, c), 0)),
        out_shape=jax.ShapeDtypeStruct((t, SSD_INNER), BF16),
        scratch_shapes=[
            pltpu.VMEM((L + 8, SSD_CONV_DIM), F32),
            pltpu.VMEM((SSD_STATE, SSD_INNER), F32),
        ],
        compiler_params=_cparams(("parallel", "arbitrary")),
        name="ssd",
    )(proj, proj, proj, prm["ssd_conv_w"], prm["ssd_conv_b"], prm["ssd_dtb"], prm["ssd_dtb_t"],
      prm["ssd_alog"], prm["ssd_alog_t"], prm["ssd_dexp"], prm["ssd_norm"])


def _gla_body(q_ref, k_ref, v_ref, r_ref, sm_ref, wgk_ref, bgk_ref, gn_ref, seg_ref, o_ref,
              g_scr, st_scr, *, nch):
    L, C, H, K, V = GLA_CHUNK, GLA_SUB, GLA_HEADS, GLA_HEAD_K, GLA_HEAD_V
    nb = L // C
    T = nch * L

    @pl.when(pl.program_id(1) == 0)
    def _():
        st_scr[...] = jnp.zeros_like(st_scr)

    xg = _dot(sm_ref[...].astype(BF16), wgk_ref[...]) + bgk_ref[...]
    log_a = -_softplus(-xg) * (1.0 / GLA_GATE_TAU)
    rr = _iota((T, T), 0)
    cc = _iota((T, T), 1)
    tril = jnp.where(rr // L == cc // L, (rr >= cc).astype(F32), 0.0)
    g_all = jnp.dot(tril, log_a, precision=HIGHEST, preferred_element_type=F32)
    g_scr[...] = g_all
    qs_all = q_ref[...] * (K ** -0.5)

    head_rows_k = _iota((H * L, GLA_KD), 0) // L == _iota((H * L, GLA_KD), 1) // K
    head_rows_v = _iota((H * L, GLA_VD), 0) // L == _iota((H * L, GLA_VD), 1) // V
    rowblk = _iota((L, H * L), 0) // C
    colblk = (_iota((L, H * L), 1) % L) // C
    lrow = _iota((C, GLA_KD), 0)

    o_intra, q_dec, st_upd, g_last = [], [], [], []
    for ch in range(nch):
        base = ch * L
        g = g_all[base:base + L, :]
        qs = qs_all[base:base + L, :]
        k = k_ref[base:base + L, :]
        ends = [g_scr[base + C * j + C - 1:base + C * j + C, :] for j in range(nb)]
        end_b = jnp.concatenate([jnp.broadcast_to(e, (C, GLA_KD)) for e in ends], axis=0)
        kd = k * jnp.exp(end_b - g)
        k_exp = jnp.where(head_rows_k, jnp.concatenate([kd] * H, axis=0), 0.0).astype(BF16)

        attn = jnp.zeros((L, H * L), F32)
        for j in range(nb - 1):
            qd = qs * jnp.exp(jnp.minimum(g - ends[j], 0.0))
            aj = _dot_nt(qd.astype(BF16), k_exp)
            attn = jnp.where(colblk == j, jnp.where(rowblk > j, aj, 0.0), attn)

        pieces = []
        for j in range(C):
            blocks = []
            for i in range(nb):
                r0 = C * i
                gj = g_scr[base + r0 + j:base + r0 + j + 1, :]
                kj = k_ref[base + r0 + j:base + r0 + j + 1, :]
                rel = jnp.exp(jnp.minimum(g[r0:r0 + C, :] - gj, 0.0))
                blocks.append(jnp.where(lrow >= j, qs[r0:r0 + C, :] * kj * rel, 0.0))
            pieces.append(jnp.concatenate(blocks, axis=0).astype(BF16))
        a_diag = _dot(jnp.concatenate(pieces, axis=1), seg_ref[...])
        attn = jnp.where(colblk == rowblk, a_diag, attn)

        v = v_ref[base:base + L, :]
        v_bd = jnp.where(head_rows_v, jnp.concatenate([v] * H, axis=0), 0.0).astype(BF16)
        o_intra.append(_dot(attn.astype(BF16), v_bd))
        q_dec.append((qs * jnp.exp(g)).astype(BF16))
        g_last.append(ends[nb - 1])
        k_dec = (k * jnp.exp(ends[nb - 1] - g)).astype(BF16)
        st_upd.append(_dot_tn(v.astype(BF16), k_dec))

    same_head = _iota((GLA_VD, GLA_KD), 0) // V == _iota((GLA_VD, GLA_KD), 1) // K
    st = st_scr[...]
    for ch in range(nch):
        base = ch * L
        o = o_intra[ch] + _dot_nt(q_dec[ch], st.astype(BF16))
        st = st * jnp.exp(g_last[ch]) + jnp.where(same_head, st_upd[ch], 0.0)
        normed = jnp.concatenate([_rms(o[:, h * V:(h + 1) * V], gn_ref[...]) for h in range(H)], axis=1)
        o_ref[base:base + L, :] = (normed * _silu(r_ref[base:base + L, :])).astype(BF16)
    st_scr[...] = st


def _gla(proj, prm, layer, batch, seq, nch=4):
    L = GLA_CHUNK * nch
    nc = seq // L
    t = proj.shape[0]
    row = lambda b, c: b * nc + c
    lay3 = lambda shape: pl.BlockSpec((None,) + shape, lambda b, c: (layer, 0, 0))
    seg = prm["gla_seg"]
    return pl.pallas_call(
        functools.partial(_gla_body, nch=nch),
        grid=(batch, nc),
        in_specs=[
            pl.BlockSpec((L, GLA_KD), lambda b, c: (row(b, c), C_GQ // GLA_KD)),
            pl.BlockSpec((L, GLA_KD), lambda b, c: (row(b, c), C_GK // GLA_KD)),
            pl.BlockSpec((L, GLA_VD), lambda b, c: (row(b, c), C_GV // GLA_VD)),
            pl.BlockSpec((L, GLA_VD), lambda b, c: (row(b, c), C_GR // GLA_VD)),
            pl.BlockSpec((L, 128), lambda b, c: (row(b, c), C_SM // 128)),
            lay3((128, GLA_KD)),
            lay3((1, GLA_KD)),
            lay3((1, GLA_HEAD_V)),
            pl.BlockSpec(seg.shape, lambda b, c: (0, 0)),
        ],
        out_specs=pl.BlockSpec((L, GLA_VD), lambda b, c: (row(b, c), 0)),
        out_shape=jax.ShapeDtypeStruct((t, GLA_VD), BF16),
        scratch_shapes=[
            pltpu.VMEM((L, GLA_KD), F32),
            pltpu.VMEM((GLA_VD, GLA_KD), F32),
        ],
        compiler_params=_cparams(("parallel", "arbitrary")),
        name="gla",
    )(proj, proj, proj, proj, proj, prm["gla_wgk"], prm["gla_bgk"], prm["gla_norm"], seg)


def _out_proj_body(x_ref, ya_ref, yb_ref, yc_ref, w_ref, o_ref, *, tn):
    y = jnp.concatenate([ya_ref[...], yb_ref[...], yc_ref[...]], axis=1)
    for j in range(o_ref.shape[1] // tn):
        cols = slice(j * tn, (j + 1) * tn)
        o_ref[:, cols] = x_ref[:, cols] + _dot(y, w_ref[:, cols])


def _out_proj(x2d, y_mla, y_ssd, y_gla, w_all, layer, tm=512, tn=1024):
    t, d = x2d.shape
    da, db, dc = y_mla.shape[1], y_ssd.shape[1], y_gla.shape[1]
    tm = min(tm, t)
    return pl.pallas_call(
        functools.partial(_out_proj_body, tn=tn),
        grid=(t // tm,),
        in_specs=[
            pl.BlockSpec((tm, d), lambda i: (i, 0)),
            pl.BlockSpec((tm, da), lambda i: (i, 0)),
            pl.BlockSpec((tm, db), lambda i: (i, 0)),
            pl.BlockSpec((tm, dc), lambda i: (i, 0)),
            pl.BlockSpec((None, da + db + dc, d), lambda i: (layer, 0, 0), pipeline_mode=pl.Buffered(1)),
        ],
        out_specs=pl.BlockSpec((tm, d), lambda i: (i, 0)),
        out_shape=jax.ShapeDtypeStruct((t, d), F32),
        compiler_params=_cparams(("parallel",)),
        name="out_proj",
    )(x2d, y_mla, y_ssd, y_gla, w_all)


def _ffn_body(x_ref, xh_ref, g_ref, wg_ref, wu_ref, cw_ref, cb_ref, wd_ref, fg_ref, o_ref,
              h_scr, gate_scr, *, tm, tiles_per_seq, n_out_chunks, last_layer):
    i = pl.program_id(0)
    f = pl.program_id(1)

    @pl.when(f == 0)
    def _():
        x = x_ref[...]
        h_scr[0:HALO, :] = _rms(xh_ref[...], g_ref[...]).astype(BF16)
        h_scr[HALO:HALO + tm, :] = _rms(x, g_ref[...]).astype(BF16)
        o_ref[...] = x

    gate = _dot(h_scr[...], wg_ref[...])
    keep = (i % tiles_per_seq != 0).astype(F32)
    gate_scr[0:HALO, :] = gate[0:HALO, :] * keep
    gate_scr[HALO:HALO + tm, :] = gate[HALO:, :]
    cw = cw_ref[...]
    conv = cb_ref[...] + cw[FFN_CONV - 1:FFN_CONV] * gate[HALO:, :]
    for tap in range(1, FFN_CONV):
        conv = conv + cw[FFN_CONV - 1 - tap:FFN_CONV - tap] * gate_scr[HALO - tap:HALO - tap + tm, :]
    act = (_silu(conv) * _dot(h_scr[HALO:HALO + tm, :], wu_ref[...])).astype(BF16)
    dn = o_ref.shape[1] // n_out_chunks
    for c in range(n_out_chunks):
        o_ref[:, c * dn:(c + 1) * dn] += _dot(act, wd_ref[:, c * dn:(c + 1) * dn])

    if last_layer:
        @pl.when(f == pl.num_programs(1) - 1)
        def _():
            o_ref[...] = _rms(o_ref[...], fg_ref[...])


def _ffn(x2d, prm, final_gain, layer, seq, tm=1024, tf=512):
    t, d = x2d.shape
    tm = min(tm, seq)
    tiles_per_seq = seq // tm
    halo_blocks = tm // HALO
    last_layer = layer == prm["ffn_norm"].shape[0] - 1
    lay = lambda shape, idx: pl.BlockSpec((None,) + shape, idx)
    return pl.pallas_call(
        functools.partial(_ffn_body, tm=tm, tiles_per_seq=tiles_per_seq, n_out_chunks=4,
                          last_layer=last_layer),
        grid=(t // tm, D_FF // tf),
        in_specs=[
            pl.BlockSpec((tm, d), lambda i, f: (i, 0), pipeline_mode=pl.Buffered(1)),
            pl.BlockSpec((HALO, d), lambda i, f: (jnp.maximum(i * halo_blocks - 1, 0), 0)),
            lay((1, d), lambda i, f: (layer, 0, 0)),
            lay((d, tf), lambda i, f: (layer, 0, f)),
            lay((d, tf), lambda i, f: (layer, 0, f)),
            lay((FFN_CONV, tf), lambda i, f: (layer, 0, f)),
            lay((1, tf), lambda i, f: (layer, 0, f)),
            lay((tf, d), lambda i, f: (layer, f, 0)),
            pl.BlockSpec((1, d), lambda i, f: (0, 0)),
        ],
        out_specs=pl.BlockSpec((tm, d), lambda i, f: (i, 0)),
        out_shape=jax.ShapeDtypeStruct((t, d), F32),
        scratch_shapes=[
            pltpu.VMEM((HALO + tm, d), BF16),
            pltpu.VMEM((HALO + tm, tf), F32),
        ],
        compiler_params=_cparams(("parallel", "arbitrary")),
        name="ffn",
    )(x2d, x2d, prm["ffn_norm"], prm["ffn_w_gate"], prm["ffn_w_up"], prm["ffn_dw_w"],
      prm["ffn_dw_b"], prm["ffn_w_down"], final_gain)


_W_IN_RANGES = ((0, 512), (832, 1344), (2888, 3400), (3416, 3928), (1344, 1856), (1856, 2368),
                (512, 768), (2376, 2632), (2632, 2888), (768, 832), (768, 832), (2368, 2376), (3400, 3416))
_TRANSPOSE_ROWS = 128


def _reorder_w_in_body(wt_ref, o_ref):
    tk = wt_ref.shape[1]
    col, group, rows = 0, [], 0
    for a, b in _W_IN_RANGES + ((None, None),):
        if a is None:
            if rows:
                group.append(jnp.zeros((-rows % _TRANSPOSE_ROWS, tk), F32))
                rows += -rows % _TRANSPOSE_ROWS
        else:
            group.append(wt_ref[a:b, :])
            rows += b - a
        if rows and rows % _TRANSPOSE_ROWS == 0:
            piece = group[0] if len(group) == 1 else jnp.concatenate(group, axis=0)
            o_ref[:, col:col + rows] = piece.T.astype(BF16)
            col, group, rows = col + rows, [], 0


def _reorder_w_in(w_in, tk=512):
    depth, d, d_in = w_in.shape
    return pl.pallas_call(
        _reorder_w_in_body,
        grid=(depth, d // tk),
        in_specs=[pl.BlockSpec((None, d_in, tk), lambda l, i: (l, 0, i))],
        out_specs=pl.BlockSpec((None, tk, P_WIDTH), lambda l, i: (l, i, 0)),
        out_shape=jax.ShapeDtypeStruct((depth, d, P_WIDTH), BF16),
        compiler_params=_cparams(("parallel", "parallel")),
        name="reorder_w_in",
    )(jnp.swapaxes(w_in, 1, 2))


def _prepare(seq, attn_norm, w_in, mla_q_norm, mla_w_uq, mla_kv_norm, mla_w_ukv,
             ssd_conv_w, ssd_conv_b, ssd_dt_bias, ssd_a_log, ssd_d, ssd_norm,
             gla_w_gk, gla_b_gk, gla_norm, w_out,
             ffn_norm, ffn_w_gate, ffn_w_up, ffn_dw_w, ffn_dw_b, ffn_w_down):
    depth = w_in.shape[0]
    w_in_r = _reorder_w_in(w_in)

    wq = mla_w_uq.reshape(depth, MLA_Q_RANK, MLA_HEADS, MLA_NOPE + MLA_ROPE)
    wq = jnp.concatenate([wq, wq[..., MLA_NOPE:]], axis=-1)
    wq = wq.reshape(depth, MLA_Q_RANK, MLA_HEADS * MLA_QK).astype(BF16)
    wkv = mla_w_ukv.astype(BF16)

    inv_freq = ROPE_BASE ** (-jnp.arange(0, MLA_ROPE, 2, dtype=F32) / MLA_ROPE)
    ang = jnp.arange(seq, dtype=F32)[:, None] * inv_freq[None, :]
    cos, sin = jnp.cos(ang), jnp.sin(ang)
    zeros = jnp.zeros((seq, MLA_ROPE), F32)
    cc = jnp.concatenate([cos, cos, zeros], axis=1)
    ss = jnp.concatenate([-sin, sin, zeros], axis=1)

    pad_lanes = lambda a: jnp.pad(a, ((0, 0), (0, 128 - a.shape[1])))[:, None, :]
    wgk = jnp.zeros((depth, 128, GLA_KD), F32).at[:, SM_GLOW:SM_GLOW + GLA_GATE_RANK, :].set(gla_w_gk)

    rows = jnp.arange(GLA_SUB * GLA_KD)
    cols = jnp.arange(GLA_HEADS * GLA_CHUNK)
    seg = ((rows[:, None] % GLA_KD) // GLA_HEAD_K == cols[None, :] // GLA_CHUNK) & (
        rows[:, None] // GLA_KD == cols[None, :] % GLA_SUB)

    return dict(
        attn_norm=attn_norm[:, None, :], w_in=w_in_r,
        mla_q_norm=mla_q_norm[:, None, :], mla_kv_norm=mla_kv_norm[:, None, :], wq=wq, wkv=wkv,
        cc=cc, ss=ss,
        ssd_conv_w=ssd_conv_w, ssd_conv_b=ssd_conv_b[:, None, :],
        ssd_dtb=pad_lanes(ssd_dt_bias), ssd_dtb_t=ssd_dt_bias[:, :, None],
        ssd_alog=pad_lanes(ssd_a_log), ssd_alog_t=ssd_a_log[:, :, None],
        ssd_dexp=jnp.repeat(ssd_d, SSD_HEAD_DIM, axis=1)[:, None, :], ssd_norm=ssd_norm[:, None, :],
        gla_wgk=wgk.astype(BF16), gla_bgk=gla_b_gk[:, None, :], gla_norm=gla_norm[:, None, :],
        gla_seg=seg.astype(BF16),
        w_out=w_out.astype(BF16),
        ffn_norm=ffn_norm[:, None, :], ffn_w_gate=ffn_w_gate.astype(BF16),
        ffn_w_up=ffn_w_up.astype(BF16), ffn_dw_w=ffn_dw_w, ffn_dw_b=ffn_dw_b[:, None, :],
        ffn_w_down=ffn_w_down.astype(BF16),
    )


def kernel(x, attn_norm, w_in, mla_q_norm, mla_w_uq, mla_kv_norm, mla_w_ukv, ssd_conv_w, ssd_conv_b,
           ssd_dt_bias, ssd_a_log, ssd_d, ssd_norm, gla_w_gk, gla_b_gk, gla_norm, w_out, ffn_norm,
           ffn_w_gate, ffn_w_up, ffn_dw_w, ffn_dw_b, ffn_w_down, final_norm):
    batch, seq, d = x.shape
    prm = _prepare(seq, attn_norm, w_in, mla_q_norm, mla_w_uq, mla_kv_norm, mla_w_ukv,
                   ssd_conv_w, ssd_conv_b, ssd_dt_bias, ssd_a_log, ssd_d, ssd_norm,
                   gla_w_gk, gla_b_gk, gla_norm, w_out,
                   ffn_norm, ffn_w_gate, ffn_w_up, ffn_dw_w, ffn_dw_b, ffn_w_down)
    x2d = x.reshape(batch * seq, d)
    for layer in range(w_in.shape[0]):
        proj = _in_proj(x2d, prm["attn_norm"], prm["w_in"], layer)
        qt, k, vt = _mla_proj(proj, prm["mla_q_norm"], prm["mla_kv_norm"], prm["wq"], prm["wkv"],
                              prm["cc"], prm["ss"], layer, batch, seq)
        y_mla = _mla_attn(qt, k, vt).reshape(batch * seq, MLA_HEADS * MLA_V)
        y_ssd = _ssd(proj, prm, layer, batch, seq)
        y_gla = _gla(proj, prm, layer, batch, seq)
        x2d = _out_proj(x2d, y_mla, y_ssd, y_gla, prm["w_out"], layer)
        x2d = _ffn(x2d, prm, final_norm[None, :], layer, seq)
    return x2d.reshape(batch, seq, d)
```

```python
import functools

import jax
import jax.numpy as jnp
from jax import lax
from jax.experimental import pallas as pl
from jax.experimental.pallas import tpu as pltpu

F32 = jnp.float32
BF16 = jnp.bfloat16
HIGHEST = lax.Precision.HIGHEST

D_MODEL = 2048
DEPTH = 4
EPS = 1e-6
MLA_HEADS = 8
MLA_Q_RANK = 512
MLA_KV_RANK = 256
MLA_NOPE = 128
MLA_ROPE = 64
MLA_V = 128
MLA_QK = 256
MLA_VE = MLA_V + 16
LOG2_E = 1.4426950408889634
ROPE_BASE = 10000.0
SSD_HEADS = 8
SSD_HEAD_DIM = 64
SSD_STATE = 128
SSD_CONV = 4
SSD_CHUNK = 128
SSD_INNER = 512
SSD_CONV_DIM = 1024
GLA_HEADS = 4
GLA_HEAD_K = 64
GLA_HEAD_V = 128
GLA_GATE_RANK = 16
GLA_GATE_TAU = 16.0
GLA_CHUNK = 64
GLA_SUB = 16
GLA_KD = 256
GLA_VD = 512
D_FF = 5632
FFN_CONV = 3
HALO = 16

P_WIDTH = 4096
C_CQ, C_Z, C_GV, C_GR, C_XBC, C_CKV, C_GQ, C_GK, C_KPE, C_SM = (
    0, 512, 1024, 1536, 2048, 3072, 3328, 3584, 3840, 3968)
SM_DT = 0
SM_GLOW = 8

VMEM_LIMIT = 52 * 1024 * 1024


def _cparams(sem):
    return pltpu.CompilerParams(dimension_semantics=sem, vmem_limit_bytes=VMEM_LIMIT)


def _rms(x, g):
    var = jnp.mean(x * x, axis=-1, keepdims=True)
    return x * lax.rsqrt(var + EPS) * g


def _softplus(x):
    return jnp.maximum(x, 0.0) + jnp.log(1.0 + jnp.exp(-jnp.abs(x)))


def _silu(x):
    return x * jax.nn.sigmoid(x)


def _dot(a, b):
    return jnp.dot(a, b, preferred_element_type=F32)


def _dot_nt(a, b):
    return lax.dot_general(a, b, (((1,), (1,)), ((), ())), preferred_element_type=F32)


def _dot_tn(a, b):
    return lax.dot_general(a, b, (((0,), (0,)), ((), ())), preferred_element_type=F32)


def _iota(shape, axis):
    return lax.broadcasted_iota(jnp.int32, shape, axis)


def _cast_specs(weights, layer, steps, step_index):
    in_specs, out_specs, out_shapes = [], [], []
    for w in weights:
        _, rows, cols = w.shape
        slab = rows // steps
        assert slab * steps == rows and slab % 16 == 0, (rows, steps)
        in_specs.append(pl.BlockSpec((None, slab, cols), lambda *g: (layer, step_index(*g), 0)))
        out_specs.append(pl.BlockSpec((slab, cols), lambda *g: (step_index(*g), 0)))
        out_shapes.append(jax.ShapeDtypeStruct((rows, cols), BF16))
    return in_specs, out_specs, out_shapes


def _cast_slabs(src_refs, dst_refs):
    for src, dst in zip(src_refs, dst_refs):
        dst[...] = src[...].astype(BF16)


def _in_proj_body(x_ref, g_ref, w_ref, *refs, tn):
    n_cast = (len(refs) - 1) // 2
    o_ref = refs[n_cast]
    h = _rms(x_ref[...], g_ref[...]).astype(BF16)
    for j in range(o_ref.shape[1] // tn):
        o_ref[:, j * tn:(j + 1) * tn] = _dot(h, w_ref[:, j * tn:(j + 1) * tn])
    _cast_slabs(refs[:n_cast], refs[n_cast + 1:])


def _in_proj(x2d, gains, w_all, layer, cast_weights, tm=256, tn=1024):
    t, d = x2d.shape
    tm = min(tm, t)
    steps = t // tm
    c_in, c_out, c_shapes = _cast_specs(cast_weights, layer, steps, lambda i: i)
    return pl.pallas_call(
        functools.partial(_in_proj_body, tn=tn),
        grid=(steps,),
        in_specs=[
            pl.BlockSpec((tm, d), lambda i: (i, 0)),
            pl.BlockSpec((None, 1, d), lambda i: (layer, 0, 0)),
            pl.BlockSpec((None, d, P_WIDTH), lambda i: (layer, 0, 0), pipeline_mode=pl.Buffered(1)),
        ] + c_in,
        out_specs=[pl.BlockSpec((tm, P_WIDTH), lambda i: (i, 0))] + c_out,
        out_shape=[jax.ShapeDtypeStruct((t, P_WIDTH), F32)] + c_shapes,
        compiler_params=_cparams(("arbitrary",)),
        name="in_proj",
    )(x2d, gains, w_all, *cast_weights)


def _rope(blk, cc, ss):
    return blk * cc + pltpu.roll(blk, MLA_ROPE // 2, axis=1) * ss


def _mla_proj_body(cq_ref, ckv_ref, kpe_ref, qn_ref, kvn_ref, wq_ref, wkv_ref, cc_ref, ss_ref,
                   qt_out, k_out, vt_out, *, tk):
    cc = cc_ref[...]
    ss = ss_ref[...]
    hq = _rms(cq_ref[...], qn_ref[...]).astype(BF16)
    hkv = _rms(ckv_ref[...], kvn_ref[...]).astype(BF16)
    kpe = _rope(kpe_ref[...], cc, ss).astype(BF16)
    scale = (MLA_NOPE + MLA_ROPE) ** -0.5 * LOG2_E
    ones = jnp.ones((MLA_VE - MLA_V, tk), BF16)
    qf = _dot(hq, wq_ref[...])
    kvf = _dot(hkv, wkv_ref[...])
    tm = qf.shape[0]
    for h in range(MLA_HEADS):
        base = h * MLA_QK
        q_pe = _rope(qf[:, base + MLA_NOPE:base + MLA_QK], cc, ss)
        q_h = jnp.concatenate([qf[:, base:base + MLA_NOPE], q_pe], axis=1) * scale
        qt_out[h] = q_h.T.astype(BF16)
        k_out[h] = jnp.concatenate([kvf[:, base:base + MLA_NOPE].astype(BF16), kpe], axis=1)
        v_h = kvf[:, base + MLA_NOPE:base + MLA_QK]
        for j in range(tm // tk):
            vt_out[h, j, 0:MLA_V, :] = v_h[j * tk:(j + 1) * tk, :].T.astype(BF16)
            vt_out[h, j, MLA_V:MLA_VE, :] = ones


def _mla_proj(proj, q_norm, kv_norm, wq_all, wkv_all, cc, ss, layer, batch, seq, tm=512, tk=256):
    t = proj.shape[0]
    tm = min(tm, seq)
    tk = min(tk, tm)
    nsb = seq // tm
    H = MLA_HEADS
    return pl.pallas_call(
        functools.partial(_mla_proj_body, tk=tk),
        grid=(t // tm,),
        in_specs=[
            pl.BlockSpec((tm, MLA_Q_RANK), lambda i: (i, C_CQ // MLA_Q_RANK)),
            pl.BlockSpec((tm, MLA_KV_RANK), lambda i: (i, C_CKV // MLA_KV_RANK)),
            pl.BlockSpec((tm, 128), lambda i: (i, C_KPE // 128)),
            pl.BlockSpec((None, 1, MLA_Q_RANK), lambda i: (layer, 0, 0)),
            pl.BlockSpec((None, 1, MLA_KV_RANK), lambda i: (layer, 0, 0)),
            pl.BlockSpec((None, MLA_Q_RANK, H * MLA_QK), lambda i: (layer, 0, 0)),
            pl.BlockSpec((None, MLA_KV_RANK, H * MLA_QK), lambda i: (layer, 0, 0)),
            pl.BlockSpec((tm, 128), lambda i: (i % nsb, 0)),
            pl.BlockSpec((tm, 128), lambda i: (i % nsb, 0)),
        ],
        out_specs=[
            pl.BlockSpec((None, H, MLA_QK, tm), lambda i: (i // nsb, 0, 0, i % nsb)),
            pl.BlockSpec((None, H, tm, MLA_QK), lambda i: (i // nsb, 0, i % nsb, 0)),
            pl.BlockSpec((None, H, tm // tk, MLA_VE, tk), lambda i: (i // nsb, 0, i % nsb, 0, 0)),
        ],
        out_shape=[
            jax.ShapeDtypeStruct((batch, H, MLA_QK, seq), BF16),
            jax.ShapeDtypeStruct((batch, H, seq, MLA_QK), BF16),
            jax.ShapeDtypeStruct((batch, H, seq // tk, MLA_VE, tk), BF16),
        ],
        compiler_params=_cparams(("parallel",)),
        name="mla_proj",
    )(proj, proj, proj, q_norm, kv_norm, wq_all, wkv_all, cc, ss)


def _attn_body(qt_ref, k_ref, vt_ref, *refs, tq, hp):
    n_cast = (len(refs) - 2) // 2
    o_ref, acc_scr = refs[n_cast], refs[-1]
    _cast_slabs(refs[:n_cast], refs[n_cast + 1:-1])
    qi = pl.program_id(2)
    qts = [qt_ref[h] for h in range(hp)]

    def step(kb, ms, diagonal):
        start = pl.multiple_of(kb * tq, tq)
        ss = [_dot(k_ref[h, pl.ds(start, tq), :], qts[h]) for h in range(hp)]
        out = []
        for h in range(hp):
            s = ss[h]
            if diagonal:
                s = jnp.where(_iota((tq, tq), 0) <= _iota((tq, tq), 1), s, -jnp.inf)
            m_new = jnp.maximum(ms[h], jnp.max(s, axis=0, keepdims=True))
            p = jnp.exp2(s - m_new).astype(BF16)
            acc_scr[h] = jnp.exp2(ms[h] - m_new) * acc_scr[h] + _dot(vt_ref[h, kb], p)
            out.append(m_new)
        return tuple(out)

    acc_scr[...] = jnp.zeros_like(acc_scr)
    init = tuple(jnp.full((1, tq), -jnp.inf, F32) for _ in range(hp))
    ms = lax.fori_loop(0, qi, lambda kb, c: step(kb, c, False), init)
    step(qi, ms, True)
    outs = []
    for h in range(hp):
        acc = acc_scr[h]
        outs.append((acc[0:MLA_V, :] / acc[MLA_V:MLA_V + 1, :]).T)
    o_ref[...] = jnp.concatenate(outs, axis=1).astype(BF16)


def _mla_attn(qt, k, vt, layer, cast_weights, hp=8):
    batch, heads, _, seq = qt.shape
    tq = vt.shape[-1]
    nh, nq = heads // hp, seq // tq
    c_in, c_out, c_shapes = _cast_specs(cast_weights, layer, batch * nh * nq,
                                        lambda b, h, i: (b * nh + h) * nq + i)
    return pl.pallas_call(
        functools.partial(_attn_body, tq=tq, hp=hp),
        grid=(batch, nh, nq),
        in_specs=[
            pl.BlockSpec((None, hp, MLA_QK, tq), lambda b, h, i: (b, h, 0, i)),
            pl.BlockSpec((None, hp, seq, MLA_QK), lambda b, h, i: (b, h, 0, 0)),
            pl.BlockSpec((None, hp, seq // tq, MLA_VE, tq), lambda b, h, i: (b, h, 0, 0, 0)),
        ] + c_in,
        out_specs=[pl.BlockSpec((None, tq, hp * MLA_V), lambda b, h, i: (b, i, h))] + c_out,
        out_shape=[jax.ShapeDtypeStruct((batch, seq, heads * MLA_V), BF16)] + c_shapes,
        scratch_shapes=[pltpu.VMEM((hp, MLA_VE, tq), F32)],
        compiler_params=_cparams(("arbitrary", "arbitrary", "arbitrary")),
        name="mla_attn",
    )(qt, k, vt, *cast_weights)


def _ssd_body(z_ref, xbc_ref, sm_ref, cw_ref, cb_ref, dtb_ref, dtbt_ref, alog_ref, alogt_ref,
              dexp_ref, ng_ref, o_ref, xb_scr, st_scr):
    L, N, P = SSD_CHUNK, SSD_STATE, SSD_HEAD_DIM
    pad = 8

    @pl.when(pl.program_id(1) == 0)
    def _():
        xb_scr[0:pad, :] = jnp.zeros((pad, SSD_CONV_DIM), F32)
        st_scr[...] = jnp.zeros_like(st_scr)

    xb_scr[pad:pad + L, :] = xbc_ref[...]
    cw = cw_ref[...]
    conv = cb_ref[...] + cw[3:4] * xb_scr[pad:pad + L, :]
    for tap in range(1, SSD_CONV):
        conv = conv + cw[3 - tap:4 - tap] * xb_scr[pad - tap:pad - tap + L, :]
    xb_scr[0:pad, :] = xb_scr[L:L + pad, :]
    act = _silu(conv)
    xs = act[:, :SSD_INNER]
    bm = act[:, SSD_INNER:SSD_INNER + 2 * N]
    cm = act[:, SSD_INNER + 2 * N:]

    sm = sm_ref[...]
    lane = _iota((1, 128), 1)
    nega = jnp.where(lane < SSD_HEADS, -jnp.exp(alog_ref[...]), 0.0)
    dt = _softplus(sm + dtb_ref[...])
    a = dt * nega
    smt = sm.T[0:SSD_HEADS, :]
    dtt = _softplus(smt + dtbt_ref[...])
    at = dtt * (-jnp.exp(alogt_ref[...]))
    row = _iota((L, L), 0)
    col = _iota((L, L), 1)
    causal = row >= col
    acum = jnp.dot(causal.astype(F32), a, precision=HIGHEST, preferred_element_type=F32)
    acumt = jnp.dot(at, (row <= col).astype(F32), precision=HIGHEST, preferred_element_type=F32)

    colb = [jnp.broadcast_to(acum[:, h:h + 1], (L, L)) for h in range(SSD_HEADS)]
    dtb = [jnp.broadcast_to(dt[:, h:h + 1], (L, L)) for h in range(SSD_HEADS)]
    left = _iota((L, 2 * P), 1) < P

    def per_head_lanes(vals):
        return jnp.concatenate([jnp.where(left, vals[2 * p], vals[2 * p + 1])
                                for p in range(SSD_HEADS // 2)], axis=1)

    acum_e = per_head_lanes(colb)
    dt_e = per_head_lanes(dtb)
    alast_e = acum_e[L - 1:L, :]
    x_dt = xs * dt_e
    xd = (x_dt * jnp.exp(alast_e - acum_e)).astype(BF16)
    x_dt16 = x_dt.astype(BF16)

    ht = st_scr[...]
    ht16 = ht.astype(BF16)
    heads_per_group = SSD_HEADS // 2
    g_mats, c_mats, st_new = [], [], []
    for g in range(2):
        b_g = bm[:, g * N:(g + 1) * N]
        c_g = cm[:, g * N:(g + 1) * N]
        c_mats.append(c_g)
        g_mats.append(_dot_nt(c_g.astype(BF16), b_g.astype(BF16)))
        width = heads_per_group * P
        st_new.append(_dot(b_g.T.astype(BF16), xd[:, g * width:(g + 1) * width]))
    ys = []
    for p in range(SSD_HEADS // 2):
        g = (2 * p) // heads_per_group
        rhs = jnp.concatenate([x_dt16[:, 2 * p * P:(2 * p + 2) * P],
                               ht16[:, 2 * p * P:(2 * p + 2) * P]], axis=0)
        pair = []
        for h in (2 * p, 2 * p + 1):
            rowb = jnp.broadcast_to(acumt[h:h + 1, :], (L, L))
            decay = jnp.where(causal, jnp.exp(jnp.minimum(colb[h] - rowb, 0.0)), 0.0)
            lhs = jnp.concatenate([g_mats[g] * decay, c_mats[g] * jnp.exp(colb[h])], axis=1)
            pair.append(_dot(lhs.astype(BF16), rhs))
        ys.append(jnp.where(left, pair[0], pair[1]))
    y = jnp.concatenate(ys, axis=1) + xs * dexp_ref[...]
    st_scr[...] = ht * jnp.exp(alast_e) + jnp.concatenate(st_new, axis=1)
    o_ref[...] = _rms(y * _silu(z_ref[...]), ng_ref[...]).astype(BF16)


def _ssd(proj, prm, layer, batch, seq):
    L = SSD_CHUNK
    nc = seq // L
    t = proj.shape[0]
    row = lambda b, c: b * nc + c
    lay3 = lambda shape: pl.BlockSpec((None,) + shape, lambda b, c: (layer, 0, 0))
    return pl.pallas_call(
        _ssd_body,
        grid=(batch, nc),
        in_specs=[
            pl.BlockSpec((L, SSD_INNER), lambda b, c: (row(b, c), C_Z // SSD_INNER)),
            pl.BlockSpec((L, SSD_CONV_DIM), lambda b, c: (row(b, c), C_XBC // SSD_CONV_DIM)),
            pl.BlockSpec((L, 128), lambda b, c: (row(b, c), C_SM // 128)),
            lay3((SSD_CONV, SSD_CONV_DIM)),
            lay3((1, SSD_CONV_DIM)),
            lay3((1, 128)),
            lay3((SSD_HEADS, 1)),
            lay3((1, 128)),
            lay3((SSD_HEADS, 1)),
            lay3((1, SSD_INNER)),
            lay3((1, SSD_INNER)),
        ],
        out_specs=pl.BlockSpec((L, SSD_INNER), lambda b, c: (row(b, c), 0)),
        out_shape=jax.ShapeDtypeStruct((t, SSD_INNER), BF16),
        scratch_shapes=[
            pltpu.VMEM((L + 8, SSD_CONV_DIM), F32),
            pltpu.VMEM((SSD_STATE, SSD_INNER), F32),
        ],
        compiler_params=_cparams(("parallel", "arbitrary")),
        name="ssd",
    )(proj, proj, proj, prm["ssd_conv_w"], prm["ssd_conv_b"], prm["ssd_dtb"], prm["ssd_dtb_t"],
      prm["ssd_alog"], prm["ssd_alog_t"], prm["ssd_dexp"], prm["ssd_norm"])


def _gla_body(q_ref, k_ref, v_ref, r_ref, sm_ref, wgk_ref, bgk_ref, gn_ref, seg_ref, o_ref,
              g_scr, st_scr, *, nch):
    L, C, H, K, V = GLA_CHUNK, GLA_SUB, GLA_HEADS, GLA_HEAD_K, GLA_HEAD_V
    nb = L // C
    T = nch * L

    @pl.when(pl.program_id(1) == 0)
    def _():
        st_scr[...] = jnp.zeros_like(st_scr)

    xg = _dot(sm_ref[...].astype(BF16), wgk_ref[...]) + bgk_ref[...]
    log_a = -_softplus(-xg) * (1.0 / GLA_GATE_TAU)
    rr = _iota((T, T), 0)
    cc = _iota((T, T), 1)
    tril = jnp.where(rr // L == cc // L, (rr >= cc).astype(F32), 0.0)
    g_all = jnp.dot(tril, log_a, precision=HIGHEST, preferred_element_type=F32)
    g_scr[...] = g_all
    qs_all = q_ref[...] * (K ** -0.5)

    head_rows_k = _iota((H * L, GLA_KD), 0) // L == _iota((H * L, GLA_KD), 1) // K
    head_rows_v = _iota((H * L, GLA_VD), 0) // L == _iota((H * L, GLA_VD), 1) // V
    rowblk = _iota((L, H * L), 0) // C
    colblk = (_iota((L, H * L), 1) % L) // C
    lrow = _iota((C, GLA_KD), 0)

    o_intra, q_dec, st_upd, g_last = [], [], [], []
    for ch in range(nch):
        base = ch * L
        g = g_all[base:base + L, :]
        qs = qs_all[base:base + L, :]
        k = k_ref[base:base + L, :]
        ends = [g_scr[base + C * j + C - 1:base + C * j + C, :] for j in range(nb)]
        end_b = jnp.concatenate([jnp.broadcast_to(e, (C, GLA_KD)) for e in ends], axis=0)
        kd = k * jnp.exp(end_b - g)
        k_exp = jnp.where(head_rows_k, jnp.concatenate([kd] * H, axis=0), 0.0).astype(BF16)

        attn = jnp.zeros((L, H * L), F32)
        for j in range(nb - 1):
            qd = qs * jnp.exp(jnp.minimum(g - ends[j], 0.0))
            aj = _dot_nt(qd.astype(BF16), k_exp)
            attn = jnp.where(colblk == j, jnp.where(rowblk > j, aj, 0.0), attn)

        pieces = []
        for j in range(C):
            blocks = []
            for i in range(nb):
                r0 = C * i
                gj = g_scr[base + r0 + j:base + r0 + j + 1, :]
                kj = k_ref[base + r0 + j:base + r0 + j + 1, :]
                rel = jnp.exp(jnp.minimum(g[r0:r0 + C, :] - gj, 0.0))
                blocks.append(jnp.where(lrow >= j, qs[r0:r0 + C, :] * kj * rel, 0.0))
            pieces.append(jnp.concatenate(blocks, axis=0).astype(BF16))
        a_diag = _dot(jnp.concatenate(pieces, axis=1), seg_ref[...])
        attn = jnp.where(colblk == rowblk, a_diag, attn)

        v = v_ref[base:base + L, :]
        v_bd = jnp.where(head_rows_v, jnp.concatenate([v] * H, axis=0), 0.0).astype(BF16)
        o_intra.append(_dot(attn.astype(BF16), v_bd))
        q_dec.append((qs * jnp.exp(g)).astype(BF16))
        g_last.append(ends[nb - 1])
        k_dec = (k * jnp.exp(ends[nb - 1] - g)).astype(BF16)
        st_upd.append(_dot_tn(v.astype(BF16), k_dec))

    same_head = _iota((GLA_VD, GLA_KD), 0) // V == _iota((GLA_VD, GLA_KD), 1) // K
    st = st_scr[...]
    for ch in range(nch):
        base = ch * L
        o = o_intra[ch] + _dot_nt(q_dec[ch], st.astype(BF16))
        st = st * jnp.exp(g_last[ch]) + jnp.where(same_head, st_upd[ch], 0.0)
        normed = jnp.concatenate([_rms(o[:, h * V:(h + 1) * V], gn_ref[...]) for h in range(H)], axis=1)
        o_ref[base:base + L, :] = (normed * _silu(r_ref[base:base + L, :])).astype(BF16)
    st_scr[...] = st


def _gla(proj, prm, layer, batch, seq, nch=4):
    L = GLA_CHUNK * nch
    nc = seq // L
    t = proj.shape[0]
    row = lambda b, c: b * nc + c
    lay3 = lambda shape: pl.BlockSpec((None,) + shape, lambda b, c: (layer, 0, 0))
    seg = prm["gla_seg"]
    return pl.pallas_call(
        functools.partial(_gla_body, nch=nch),
        grid=(batch, nc),
        in_specs=[
            pl.BlockSpec((L, GLA_KD), lambda b, c: (row(b, c), C_GQ // GLA_KD)),
            pl.BlockSpec((L, GLA_KD), lambda b, c: (row(b, c), C_GK // GLA_KD)),
            pl.BlockSpec((L, GLA_VD), lambda b, c: (row(b, c), C_GV // GLA_VD)),
            pl.BlockSpec((L, GLA_VD), lambda b, c: (row(b, c), C_GR // GLA_VD)),
            pl.BlockSpec((L, 128), lambda b, c: (row(b, c), C_SM // 128)),
            lay3((128, GLA_KD)),
            lay3((1, GLA_KD)),
            lay3((1, GLA_HEAD_V)),
            pl.BlockSpec(seg.shape, lambda b, c: (0, 0)),
        ],
        out_specs=pl.BlockSpec((L, GLA_VD), lambda b, c: (row(b, c), 0)),
        out_shape=jax.ShapeDtypeStruct((t, GLA_VD), BF16),
        scratch_shapes=[
            pltpu.VMEM((L, GLA_KD), F32),
            pltpu.VMEM((GLA_VD, GLA_KD), F32),
        ],
        compiler_params=_cparams(("parallel", "arbitrary")),
        name="gla",
    )(proj, proj, proj, proj, proj, prm["gla_wgk"], prm["gla_bgk"], prm["gla_norm"], seg)


def _out_proj_body(x_ref, ya_ref, yb_ref, yc_ref, w_ref, o_ref, *, tn):
    y = jnp.concatenate([ya_ref[...], yb_ref[...], yc_ref[...]], axis=1)
    for j in range(o_ref.shape[1] // tn):
        cols = slice(j * tn, (j + 1) * tn)
        o_ref[:, cols] = x_ref[:, cols] + _dot(y, w_ref[:, cols])


def _out_proj(x2d, y_mla, y_ssd, y_gla, w16, tm=512, tn=1024):
    t, d = x2d.shape
    da, db, dc = y_mla.shape[1], y_ssd.shape[1], y_gla.shape[1]
    tm = min(tm, t)
    return pl.pallas_call(
        functools.partial(_out_proj_body, tn=tn),
        grid=(t // tm,),
        in_specs=[
            pl.BlockSpec((tm, d), lambda i: (i, 0)),
            pl.BlockSpec((tm, da), lambda i: (i, 0)),
            pl.BlockSpec((tm, db), lambda i: (i, 0)),
            pl.BlockSpec((tm, dc), lambda i: (i, 0)),
            pl.BlockSpec((da + db + dc, d), lambda i: (0, 0), pipeline_mode=pl.Buffered(1)),
        ],
        out_specs=pl.BlockSpec((tm, d), lambda i: (i, 0)),
        out_shape=jax.ShapeDtypeStruct((t, d), F32),
        compiler_params=_cparams(("parallel",)),
        name="out_proj",
    )(x2d, y_mla, y_ssd, y_gla, w16)


def _ffn_body(x_ref, xh_ref, g_ref, wg_ref, wu_ref, cw_ref, cb_ref, wd_ref, fg_ref, o_ref,
              h_scr, gate_scr, *, tm, tiles_per_seq, n_out_chunks, last_layer):
    i = pl.program_id(0)
    f = pl.program_id(1)

    @pl.when(f == 0)
    def _():
        x = x_ref[...]
        h_scr[0:HALO, :] = _rms(xh_ref[...], g_ref[...]).astype(BF16)
        h_scr[HALO:HALO + tm, :] = _rms(x, g_ref[...]).astype(BF16)
        o_ref[...] = x

    gate = _dot(h_scr[...], wg_ref[...])
    keep = (i % tiles_per_seq != 0).astype(F32)
    gate_scr[0:HALO, :] = gate[0:HALO, :] * keep
    gate_scr[HALO:HALO + tm, :] = gate[HALO:, :]
    cw = cw_ref[...]
    conv = cb_ref[...] + cw[FFN_CONV - 1:FFN_CONV] * gate[HALO:, :]
    for tap in range(1, FFN_CONV):
        conv = conv + cw[FFN_CONV - 1 - tap:FFN_CONV - tap] * gate_scr[HALO - tap:HALO - tap + tm, :]
    act = (_silu(conv) * _dot(h_scr[HALO:HALO + tm, :], wu_ref[...])).astype(BF16)
    dn = o_ref.shape[1] // n_out_chunks
    for c in range(n_out_chunks):
        o_ref[:, c * dn:(c + 1) * dn] += _dot(act, wd_ref[:, c * dn:(c + 1) * dn])

    if last_layer:
        @pl.when(f == pl.num_programs(1) - 1)
        def _():
            o_ref[...] = _rms(o_ref[...], fg_ref[...])


def _ffn(x2d, prm, wg16, wu16, wd16, final_gain, layer, seq, tm=512, tf=512):
    t, d = x2d.shape
    tm = min(tm, seq)
    tiles_per_seq = seq // tm
    halo_blocks = tm // HALO
    last_layer = layer == prm["ffn_norm"].shape[0] - 1
    lay = lambda shape, idx: pl.BlockSpec((None,) + shape, idx)
    return pl.pallas_call(
        functools.partial(_ffn_body, tm=tm, tiles_per_seq=tiles_per_seq, n_out_chunks=4,
                          last_layer=last_layer),
        grid=(t // tm, D_FF // tf),
        in_specs=[
            pl.BlockSpec((tm, d), lambda i, f: (i, 0)),
            pl.BlockSpec((HALO, d), lambda i, f: (jnp.maximum(i * halo_blocks - 1, 0), 0)),
            lay((1, d), lambda i, f: (layer, 0, 0)),
            pl.BlockSpec((d, tf), lambda i, f: (0, f)),
            pl.BlockSpec((d, tf), lambda i, f: (0, f)),
            lay((FFN_CONV, tf), lambda i, f: (layer, 0, f)),
            lay((1, tf), lambda i, f: (layer, 0, f)),
            pl.BlockSpec((tf, d), lambda i, f: (f, 0)),
            pl.BlockSpec((1, d), lambda i, f: (0, 0)),
        ],
        out_specs=pl.BlockSpec((tm, d), lambda i, f: (i, 0)),
        out_shape=jax.ShapeDtypeStruct((t, d), F32),
        scratch_shapes=[
            pltpu.VMEM((HALO + tm, d), BF16),
            pltpu.VMEM((HALO + tm, tf), F32),
        ],
        compiler_params=_cparams(("parallel", "arbitrary")),
        name="ffn",
    )(x2d, x2d, prm["ffn_norm"], wg16, wu16, prm["ffn_dw_w"], prm["ffn_dw_b"], wd16, final_gain)


_W_IN_RANGES = ((0, 512), (832, 1344), (2888, 3400), (3416, 3928), (1344, 1856), (1856, 2368),
                (512, 768), (2376, 2632), (2632, 2888), (768, 832), (768, 832), (2368, 2376), (3400, 3416))
_TRANSPOSE_ROWS = 128


def _reorder_w_in_body(wt_ref, o_ref):
    tk = wt_ref.shape[1]
    col, group, rows = 0, [], 0
    for a, b in _W_IN_RANGES + ((None, None),):
        if a is None:
            if rows:
                group.append(jnp.zeros((-rows % _TRANSPOSE_ROWS, tk), F32))
                rows += -rows % _TRANSPOSE_ROWS
        else:
            group.append(wt_ref[a:b, :])
            rows += b - a
        if rows and rows % _TRANSPOSE_ROWS == 0:
            piece = group[0] if len(group) == 1 else jnp.concatenate(group, axis=0)
            o_ref[:, col:col + rows] = piece.T.astype(BF16)
            col, group, rows = col + rows, [], 0


def _reorder_w_in(w_in, tk=512):
    depth, d, d_in = w_in.shape
    return pl.pallas_call(
        _reorder_w_in_body,
        grid=(depth, d // tk),
        in_specs=[pl.BlockSpec((None, d_in, tk), lambda l, i: (l, 0, i))],
        out_specs=pl.BlockSpec((None, tk, P_WIDTH), lambda l, i: (l, i, 0)),
        out_shape=jax.ShapeDtypeStruct((depth, d, P_WIDTH), BF16),
        compiler_params=_cparams(("parallel", "parallel")),
        name="reorder_w_in",
    )(jnp.swapaxes(w_in, 1, 2))


def _prepare(seq, attn_norm, w_in, mla_q_norm, mla_w_uq, mla_kv_norm, mla_w_ukv,
             ssd_conv_w, ssd_conv_b, ssd_dt_bias, ssd_a_log, ssd_d, ssd_norm,
             gla_w_gk, gla_b_gk, gla_norm, ffn_norm, ffn_dw_w, ffn_dw_b):
    depth = w_in.shape[0]
    w_in_r = _reorder_w_in(w_in)

    wq = mla_w_uq.reshape(depth, MLA_Q_RANK, MLA_HEADS, MLA_NOPE + MLA_ROPE)
    wq = jnp.concatenate([wq, wq[..., MLA_NOPE:]], axis=-1)
    wq = wq.reshape(depth, MLA_Q_RANK, MLA_HEADS * MLA_QK).astype(BF16)
    wkv = mla_w_ukv.astype(BF16)

    inv_freq = ROPE_BASE ** (-jnp.arange(0, MLA_ROPE, 2, dtype=F32) / MLA_ROPE)
    ang = jnp.arange(seq, dtype=F32)[:, None] * inv_freq[None, :]
    cos, sin = jnp.cos(ang), jnp.sin(ang)
    zeros = jnp.zeros((seq, MLA_ROPE), F32)
    cc = jnp.concatenate([cos, cos, zeros], axis=1)
    ss = jnp.concatenate([-sin, sin, zeros], axis=1)

    pad_lanes = lambda a: jnp.pad(a, ((0, 0), (0, 128 - a.shape[1])))[:, None, :]
    wgk = jnp.zeros((depth, 128, GLA_KD), F32).at[:, SM_GLOW:SM_GLOW + GLA_GATE_RANK, :].set(gla_w_gk)

    rows = jnp.arange(GLA_SUB * GLA_KD)
    cols = jnp.arange(GLA_HEADS * GLA_CHUNK)
    seg = ((rows[:, None] % GLA_KD) // GLA_HEAD_K == cols[None, :] // GLA_CHUNK) & (
        rows[:, None] // GLA_KD == cols[None, :] % GLA_SUB)

    return dict(
        attn_norm=attn_norm[:, None, :], w_in=w_in_r,
        mla_q_norm=mla_q_norm[:, None, :], mla_kv_norm=mla_kv_norm[:, None, :], wq=wq, wkv=wkv,
        cc=cc, ss=ss,
        ssd_conv_w=ssd_conv_w, ssd_conv_b=ssd_conv_b[:, None, :],
        ssd_dtb=pad_lanes(ssd_dt_bias), ssd_dtb_t=ssd_dt_bias[:, :, None],
        ssd_alog=pad_lanes(ssd_a_log), ssd_alog_t=ssd_a_log[:, :, None],
        ssd_dexp=jnp.repeat(ssd_d, SSD_HEAD_DIM, axis=1)[:, None, :], ssd_norm=ssd_norm[:, None, :],
        gla_wgk=wgk.astype(BF16), gla_bgk=gla_b_gk[:, None, :], gla_norm=gla_norm[:, None, :],
        gla_seg=seg.astype(BF16),
        ffn_norm=ffn_norm[:, None, :], ffn_dw_w=ffn_dw_w, ffn_dw_b=ffn_dw_b[:, None, :],
    )


def kernel(x, attn_norm, w_in, mla_q_norm, mla_w_uq, mla_kv_norm, mla_w_ukv, ssd_conv_w, ssd_conv_b,
           ssd_dt_bias, ssd_a_log, ssd_d, ssd_norm, gla_w_gk, gla_b_gk, gla_norm, w_out, ffn_norm,
           ffn_w_gate, ffn_w_up, ffn_dw_w, ffn_dw_b, ffn_w_down, final_norm):
    batch, seq, d = x.shape
    prm = _prepare(seq, attn_norm, w_in, mla_q_norm, mla_w_uq, mla_kv_norm, mla_w_ukv,
                   ssd_conv_w, ssd_conv_b, ssd_dt_bias, ssd_a_log, ssd_d, ssd_norm,
                   gla_w_gk, gla_b_gk, gla_norm, ffn_norm, ffn_dw_w, ffn_dw_b)
    x2d = x.reshape(batch * seq, d)
    for layer in range(w_in.shape[0]):
        proj, wg16, wu16 = _in_proj(x2d, prm["attn_norm"], prm["w_in"], layer, (ffn_w_gate, ffn_w_up))
        qt, k, vt = _mla_proj(proj, prm["mla_q_norm"], prm["mla_kv_norm"], prm["wq"], prm["wkv"],
                              prm["cc"], prm["ss"], layer, batch, seq)
        y_mla, wd16, wo16 = _mla_attn(qt, k, vt, layer, (ffn_w_down, w_out))
        y_mla = y_mla.reshape(batch * seq, MLA_HEADS * MLA_V)
        y_ssd = _ssd(proj, prm, layer, batch, seq)
        y_gla = _gla(proj, prm, layer, batch, seq)
        x2d = _out_proj(x2d, y_mla, y_ssd, y_gla, wo16)
        x2d = _ffn(x2d, prm, wg16, wu16, wd16, final_norm[None, :], layer, seq)
    return x2d.reshape(batch, seq, d)
```

```python
import functools

import jax
import jax.numpy as jnp
from jax import lax
from jax.experimental import pallas as pl
from jax.experimental.pallas import tpu as pltpu

F32 = jnp.float32
BF16 = jnp.bfloat16
HIGHEST = lax.Precision.HIGHEST

D_MODEL = 2048
DEPTH = 4
EPS = 1e-6
MLA_HEADS = 8
MLA_Q_RANK = 512
MLA_KV_RANK = 256
MLA_NOPE = 128
MLA_ROPE = 64
MLA_V = 128
MLA_QK = 256
MLA_VE = MLA_V + 16
LOG2_E = 1.4426950408889634
ROPE_BASE = 10000.0
SSD_HEADS = 8
SSD_HEAD_DIM = 64
SSD_STATE = 128
SSD_CONV = 4
SSD_CHUNK = 128
SSD_INNER = 512
SSD_CONV_DIM = 1024
GLA_HEADS = 4
GLA_HEAD_K = 64
GLA_HEAD_V = 128
GLA_GATE_RANK = 16
GLA_GATE_TAU = 16.0
GLA_CHUNK = 64
GLA_SUB = 16
GLA_KD = 256
GLA_VD = 512
D_FF = 5632
FFN_CONV = 3
HALO = 16

P_WIDTH = 4096
P_OUT = 3200
C_XBC, C_Z, C_GV, C_GR, C_GQ, C_GK, C_SM, C_CQ, C_CKV, C_KPE = (
    0, 1024, 1536, 2048, 2560, 2816, 3072, 3200, 3712, 3968)
SM_DT = 0
SM_GLOW = 8

VMEM_LIMIT = 52 * 1024 * 1024


def _cparams(sem):
    return pltpu.CompilerParams(dimension_semantics=sem, vmem_limit_bytes=VMEM_LIMIT)


def _rms(x, g):
    var = jnp.mean(x * x, axis=-1, keepdims=True)
    return x * lax.rsqrt(var + EPS) * g


def _softplus(x):
    return jnp.maximum(x, 0.0) + jnp.log(1.0 + jnp.exp(-jnp.abs(x)))


def _silu(x):
    return x * jax.nn.sigmoid(x)


def _dot(a, b):
    return jnp.dot(a, b, preferred_element_type=F32)


def _dot_nt(a, b):
    return lax.dot_general(a, b, (((1,), (1,)), ((), ())), preferred_element_type=F32)


def _dot_tn(a, b):
    return lax.dot_general(a, b, (((0,), (0,)), ((), ())), preferred_element_type=F32)


def _iota(shape, axis):
    return lax.broadcasted_iota(jnp.int32, shape, axis)


def _cast_specs(weights, layer, steps, step_index):
    in_specs, out_specs, out_shapes = [], [], []
    for w in weights:
        _, rows, cols = w.shape
        slab = rows // steps
        assert slab * steps == rows and slab % 16 == 0, (rows, steps)
        in_specs.append(pl.BlockSpec((None, slab, cols), lambda *g: (layer, step_index(*g), 0)))
        out_specs.append(pl.BlockSpec((slab, cols), lambda *g: (step_index(*g), 0)))
        out_shapes.append(jax.ShapeDtypeStruct((rows, cols), BF16))
    return in_specs, out_specs, out_shapes


def _cast_slabs(src_refs, dst_refs):
    for src, dst in zip(src_refs, dst_refs):
        dst[...] = src[...].astype(BF16)


def _rope(blk, cc, ss):
    return blk * cc + pltpu.roll(blk, MLA_ROPE // 2, axis=1) * ss


def _in_proj_body(x_ref, g_ref, w_ref, qn_ref, kvn_ref, wq_ref, wkv_ref, cc_ref, ss_ref, *refs, tn):
    n_cast = (len(refs) - 4) // 2
    o_ref, qt_out, k_out, vt_out = refs[n_cast:n_cast + 4]
    tm = x_ref.shape[0]
    h = _rms(x_ref[...], g_ref[...]).astype(BF16)
    mla = _dot(h, w_ref[:, P_OUT:])
    cq = mla[:, C_CQ - P_OUT:C_CKV - P_OUT]
    ckv = mla[:, C_CKV - P_OUT:C_KPE - P_OUT]
    cc = cc_ref[...]
    ss = ss_ref[...]
    kpe = _rope(mla[:, C_KPE - P_OUT:], cc, ss).astype(BF16)
    qf = _dot(_rms(cq, qn_ref[...]).astype(BF16), wq_ref[...])
    kvf = _dot(_rms(ckv, kvn_ref[...]).astype(BF16), wkv_ref[...])
    scale = (MLA_NOPE + MLA_ROPE) ** -0.5 * LOG2_E
    ones = jnp.ones((MLA_VE - MLA_V, tm), BF16)

    def head_outputs(hd):
        base = hd * MLA_QK
        q_pe = _rope(qf[:, base + MLA_NOPE:base + MLA_QK], cc, ss)
        q_h = jnp.concatenate([qf[:, base:base + MLA_NOPE], q_pe], axis=1) * scale
        qt_out[hd] = q_h.T.astype(BF16)
        k_out[hd] = jnp.concatenate([kvf[:, base:base + MLA_NOPE].astype(BF16), kpe], axis=1)
        vt_out[hd, 0:MLA_V, :] = kvf[:, base + MLA_NOPE:base + MLA_QK].T.astype(BF16)
        vt_out[hd, MLA_V:MLA_VE, :] = ones

    starts = list(range(0, P_OUT, tn))
    heads_per_chunk = -(-MLA_HEADS // len(starts))
    for idx, c0 in enumerate(starts):
        cols = slice(c0, min(c0 + tn, P_OUT))
        o_ref[:, cols] = _dot(h, w_ref[:, cols])
        for hd in range(idx * heads_per_chunk, min((idx + 1) * heads_per_chunk, MLA_HEADS)):
            head_outputs(hd)
    _cast_slabs(refs[:n_cast], refs[n_cast + 4:])


def _in_proj(x2d, prm, layer, batch, seq, cast_weights, tm=256, tn=1024):
    t, d = x2d.shape
    tm = min(tm, seq)
    steps = t // tm
    nsb = seq // tm
    H = MLA_HEADS
    resident = lambda shape: pl.BlockSpec((None,) + shape, lambda i: (layer, 0, 0),
                                          pipeline_mode=pl.Buffered(1))
    c_in, c_out, c_shapes = _cast_specs(cast_weights, layer, steps, lambda i: i)
    return pl.pallas_call(
        functools.partial(_in_proj_body, tn=tn),
        grid=(steps,),
        in_specs=[
            pl.BlockSpec((tm, d), lambda i: (i, 0)),
            pl.BlockSpec((None, 1, d), lambda i: (layer, 0, 0)),
            resident((d, P_WIDTH)),
            pl.BlockSpec((None, 1, MLA_Q_RANK), lambda i: (layer, 0, 0)),
            pl.BlockSpec((None, 1, MLA_KV_RANK), lambda i: (layer, 0, 0)),
            resident((MLA_Q_RANK, H * MLA_QK)),
            resident((MLA_KV_RANK, H * MLA_QK)),
            pl.BlockSpec((tm, 128), lambda i: (i % nsb, 0)),
            pl.BlockSpec((tm, 128), lambda i: (i % nsb, 0)),
        ] + c_in,
        out_specs=[
            pl.BlockSpec((tm, P_OUT), lambda i: (i, 0)),
            pl.BlockSpec((None, H, MLA_QK, tm), lambda i: (i // nsb, 0, 0, i % nsb)),
            pl.BlockSpec((None, H, tm, MLA_QK), lambda i: (i // nsb, 0, i % nsb, 0)),
            pl.BlockSpec((None, H, None, MLA_VE, tm), lambda i: (i // nsb, 0, i % nsb, 0, 0)),
        ] + c_out,
        out_shape=[
            jax.ShapeDtypeStruct((t, P_OUT), F32),
            jax.ShapeDtypeStruct((batch, H, MLA_QK, seq), BF16),
            jax.ShapeDtypeStruct((batch, H, seq, MLA_QK), BF16),
            jax.ShapeDtypeStruct((batch, H, nsb, MLA_VE, tm), BF16),
        ] + c_shapes,
        compiler_params=_cparams(("arbitrary",)),
        name="in_proj",
    )(x2d, prm["attn_norm"], prm["w_in"], prm["mla_q_norm"], prm["mla_kv_norm"], prm["wq"],
      prm["wkv"], prm["cc"], prm["ss"], *cast_weights)


def _attn_body(qt_ref, k_ref, vt_ref, *refs, tq, hp):
    n_cast = (len(refs) - 2) // 2
    o_ref, acc_scr = refs[n_cast], refs[-1]
    _cast_slabs(refs[:n_cast], refs[n_cast + 1:-1])
    qi = pl.program_id(2)
    qts = [qt_ref[h] for h in range(hp)]

    def step(kb, ms, diagonal):
        start = pl.multiple_of(kb * tq, tq)
        ss = [_dot(k_ref[h, pl.ds(start, tq), :], qts[h]) for h in range(hp)]
        out = []
        for h in range(hp):
            s = ss[h]
            if diagonal:
                s = jnp.where(_iota((tq, tq), 0) <= _iota((tq, tq), 1), s, -jnp.inf)
            m_new = jnp.maximum(ms[h], jnp.max(s, axis=0, keepdims=True))
            p = jnp.exp2(s - m_new).astype(BF16)
            acc_scr[h] = jnp.exp2(ms[h] - m_new) * acc_scr[h] + _dot(vt_ref[h, kb], p)
            out.append(m_new)
        return tuple(out)

    acc_scr[...] = jnp.zeros_like(acc_scr)
    init = tuple(jnp.full((1, tq), -jnp.inf, F32) for _ in range(hp))
    ms = lax.fori_loop(0, qi, lambda kb, c: step(kb, c, False), init)
    step(qi, ms, True)
    outs = []
    for h in range(hp):
        acc = acc_scr[h]
        outs.append((acc[0:MLA_V, :] / acc[MLA_V:MLA_V + 1, :]).T)
    o_ref[...] = jnp.concatenate(outs, axis=1).astype(BF16)


def _mla_attn(qt, k, vt, layer, cast_weights, hp=8):
    batch, heads, _, seq = qt.shape
    tq = vt.shape[-1]
    nh, nq = heads // hp, seq // tq
    c_in, c_out, c_shapes = _cast_specs(cast_weights, layer, batch * nh * nq,
                                        lambda b, h, i: (b * nh + h) * nq + i)
    return pl.pallas_call(
        functools.partial(_attn_body, tq=tq, hp=hp),
        grid=(batch, nh, nq),
        in_specs=[
            pl.BlockSpec((None, hp, MLA_QK, tq), lambda b, h, i: (b, h, 0, i)),
            pl.BlockSpec((None, hp, seq, MLA_QK), lambda b, h, i: (b, h, 0, 0)),
            pl.BlockSpec((None, hp, seq // tq, MLA_VE, tq), lambda b, h, i: (b, h, 0, 0, 0)),
        ] + c_in,
        out_specs=[pl.BlockSpec((None, tq, hp * MLA_V), lambda b, h, i: (b, i, h))] + c_out,
        out_shape=[jax.ShapeDtypeStruct((batch, seq, heads * MLA_V), BF16)] + c_shapes,
        scratch_shapes=[pltpu.VMEM((hp, MLA_VE, tq), F32)],
        compiler_params=_cparams(("arbitrary", "arbitrary", "arbitrary")),
        name="mla_attn",
    )(qt, k, vt, *cast_weights)


def _ssd_body(z_ref, xbc_ref, sm_ref, cw_ref, cb_ref, dtb_ref, dtbt_ref, alog_ref, alogt_ref,
              dexp_ref, ng_ref, o_ref, xb_scr, st_scr):
    L, N, P = SSD_CHUNK, SSD_STATE, SSD_HEAD_DIM
    pad = 8

    @pl.when(pl.program_id(1) == 0)
    def _():
        xb_scr[0:pad, :] = jnp.zeros((pad, SSD_CONV_DIM), F32)
        st_scr[...] = jnp.zeros_like(st_scr)

    xb_scr[pad:pad + L, :] = xbc_ref[...]
    cw = cw_ref[...]
    conv = cb_ref[...] + cw[3:4] * xb_scr[pad:pad + L, :]
    for tap in range(1, SSD_CONV):
        conv = conv + cw[3 - tap:4 - tap] * xb_scr[pad - tap:pad - tap + L, :]
    xb_scr[0:pad, :] = xb_scr[L:L + pad, :]
    act = _silu(conv)
    xs = act[:, :SSD_INNER]
    bm = act[:, SSD_INNER:SSD_INNER + 2 * N]
    cm = act[:, SSD_INNER + 2 * N:]

    sm = sm_ref[...]
    lane = _iota((1, 128), 1)
    nega = jnp.where(lane < SSD_HEADS, -LOG2_E * jnp.exp(alog_ref[...]), 0.0)
    dt = _softplus(sm + dtb_ref[...])
    a = dt * nega
    smt = sm.T[0:SSD_HEADS, :]
    dtt = _softplus(smt + dtbt_ref[...])
    at = dtt * (-LOG2_E * jnp.exp(alogt_ref[...]))
    row = _iota((L, L), 0)
    col = _iota((L, L), 1)
    causal = row >= col
    acum = jnp.dot(causal.astype(F32), a, precision=HIGHEST, preferred_element_type=F32)
    acumt = jnp.dot(at, (row <= col).astype(F32), precision=HIGHEST, preferred_element_type=F32)

    colb = [jnp.broadcast_to(acum[:, h:h + 1], (L, L)) for h in range(SSD_HEADS)]
    dtb = [jnp.broadcast_to(dt[:, h:h + 1], (L, L)) for h in range(SSD_HEADS)]
    left = _iota((L, 2 * P), 1) < P

    def per_head_lanes(vals):
        return jnp.concatenate([jnp.where(left, vals[2 * p], vals[2 * p + 1])
                                for p in range(SSD_HEADS // 2)], axis=1)

    acum_e = per_head_lanes(colb)
    dt_e = per_head_lanes(dtb)
    alast_e = acum_e[L - 1:L, :]
    x_dt = xs * dt_e
    xd = (x_dt * jnp.exp2(alast_e - acum_e)).astype(BF16)
    x_dt16 = x_dt.astype(BF16)

    ht = st_scr[...]
    ht16 = ht.astype(BF16)
    heads_per_group = SSD_HEADS // 2
    g_mats, c_mats, st_new = [], [], []
    for g in range(2):
        b_g = bm[:, g * N:(g + 1) * N]
        c_g = cm[:, g * N:(g + 1) * N]
        c_mats.append(c_g)
        g_mats.append(_dot_nt(c_g.astype(BF16), b_g.astype(BF16)))
        width = heads_per_group * P
        st_new.append(_dot(b_g.T.astype(BF16), xd[:, g * width:(g + 1) * width]))
    ys = []
    for p in range(SSD_HEADS // 2):
        g = (2 * p) // heads_per_group
        rhs = jnp.concatenate([x_dt16[:, 2 * p * P:(2 * p + 2) * P],
                               ht16[:, 2 * p * P:(2 * p + 2) * P]], axis=0)
        pair = []
        for h in (2 * p, 2 * p + 1):
            rowb = jnp.broadcast_to(acumt[h:h + 1, :], (L, L))
            decay = jnp.exp2(jnp.where(causal, colb[h] - rowb, -jnp.inf))
            lhs = jnp.concatenate([g_mats[g] * decay, c_mats[g] * jnp.exp2(colb[h])], axis=1)
            pair.append(_dot(lhs.astype(BF16), rhs))
        ys.append(jnp.where(left, pair[0], pair[1]))
    y = jnp.concatenate(ys, axis=1) + xs * dexp_ref[...]
    st_scr[...] = ht * jnp.exp2(alast_e) + jnp.concatenate(st_new, axis=1)
    o_ref[...] = _rms(y * _silu(z_ref[...]), ng_ref[...]).astype(BF16)


def _ssd(proj, prm, layer, batch, seq):
    L = SSD_CHUNK
    nc = seq // L
    t = proj.shape[0]
    row = lambda b, c: b * nc + c
    lay3 = lambda shape: pl.BlockSpec((None,) + shape, lambda b, c: (layer, 0, 0))
    return pl.pallas_call(
        _ssd_body,
        grid=(batch, nc),
        in_specs=[
            pl.BlockSpec((L, SSD_INNER), lambda b, c: (row(b, c), C_Z // SSD_INNER)),
            pl.BlockSpec((L, SSD_CONV_DIM), lambda b, c: (row(b, c), C_XBC // SSD_CONV_DIM)),
            pl.BlockSpec((L, 128), lambda b, c: (row(b, c), C_SM // 128)),
            lay3((SSD_CONV, SSD_CONV_DIM)),
            lay3((1, SSD_CONV_DIM)),
            lay3((1, 128)),
            lay3((SSD_HEADS, 1)),
            lay3((1, 128)),
            lay3((SSD_HEADS, 1)),
            lay3((1, SSD_INNER)),
            lay3((1, SSD_INNER)),
        ],
        out_specs=pl.BlockSpec((L, SSD_INNER), lambda b, c: (row(b, c), 0)),
        out_shape=jax.ShapeDtypeStruct((t, SSD_INNER), BF16),
        scratch_shapes=[
            pltpu.VMEM((L + 8, SSD_CONV_DIM), F32),
            pltpu.VMEM((SSD_STATE, SSD_INNER), F32),
        ],
        compiler_params=_cparams(("parallel", "arbitrary")),
        name="ssd",
    )(proj, proj, proj, prm["ssd_conv_w"], prm["ssd_conv_b"], prm["ssd_dtb"], prm["ssd_dtb_t"],
      prm["ssd_alog"], prm["ssd_alog_t"], prm["ssd_dexp"], prm["ssd_norm"])


def _gla_body(q_ref, k_ref, v_ref, r_ref, sm_ref, wgk_ref, bgk_ref, gn_ref, seg_ref, o_ref,
              g_scr, st_scr, *, nch):
    L, C, H, K, V = GLA_CHUNK, GLA_SUB, GLA_HEADS, GLA_HEAD_K, GLA_HEAD_V
    nb = L // C
    T = nch * L

    @pl.when(pl.program_id(1) == 0)
    def _():
        st_scr[...] = jnp.zeros_like(st_scr)

    xg = _dot(sm_ref[...].astype(BF16), wgk_ref[...]) + bgk_ref[...]
    log_a = -_softplus(-xg) * (LOG2_E / GLA_GATE_TAU)
    rr = _iota((T, T), 0)
    cc = _iota((T, T), 1)
    tril = jnp.where(rr // L == cc // L, (rr >= cc).astype(F32), 0.0)
    g_all = jnp.dot(tril, log_a, precision=HIGHEST, preferred_element_type=F32)
    g_scr[...] = g_all
    qs_all = q_ref[...] * (K ** -0.5)

    head_rows_k = _iota((H * L, GLA_KD), 0) // L == _iota((H * L, GLA_KD), 1) // K
    head_rows_v = _iota((H * L, GLA_VD), 0) // L == _iota((H * L, GLA_VD), 1) // V
    rowblk = _iota((L, H * L), 0) // C
    colblk = (_iota((L, H * L), 1) % L) // C
    lrow = _iota((C, GLA_KD), 0)

    attn_off, rel_prod, q_dec, st_upd, g_last = [], [], [], [], []
    for ch in range(nch):
        base = ch * L
        g = g_all[base:base + L, :]
        qs = qs_all[base:base + L, :]
        k = k_ref[base:base + L, :]
        ends = [g_scr[base + C * j + C - 1:base + C * j + C, :] for j in range(nb)]
        end_b = jnp.concatenate([jnp.broadcast_to(e, (C, GLA_KD)) for e in ends], axis=0)
        kd = k * jnp.exp2(end_b - g)
        k_exp = jnp.where(head_rows_k, jnp.concatenate([kd] * H, axis=0), 0.0).astype(BF16)

        attn = jnp.zeros((L, H * L), F32)
        for j in range(nb - 1):
            qd = qs * jnp.exp2(jnp.minimum(g - ends[j], 0.0))
            aj = _dot_nt(qd.astype(BF16), k_exp)
            attn = jnp.where(colblk == j, jnp.where(rowblk > j, aj, 0.0), attn)

        pieces = []
        for j in range(C):
            blocks = []
            for i in range(nb):
                r0 = C * i
                gj = g_scr[base + r0 + j:base + r0 + j + 1, :]
                kj = k_ref[base + r0 + j:base + r0 + j + 1, :]
                rel = jnp.exp2(jnp.where(lrow >= j, g[r0:r0 + C, :] - gj, -jnp.inf))
                blocks.append(qs[r0:r0 + C, :] * kj * rel)
            pieces.append(jnp.concatenate(blocks, axis=0).astype(BF16))
        attn_off.append(attn)
        rel_prod.append(jnp.concatenate(pieces, axis=1))
        q_dec.append((qs * jnp.exp2(g)).astype(BF16))
        g_last.append(ends[nb - 1])
        k_dec = (k * jnp.exp2(ends[nb - 1] - g)).astype(BF16)
        st_upd.append(_dot_tn(v_ref[base:base + L, :].astype(BF16), k_dec))

    a_diag = _dot(jnp.concatenate(rel_prod, axis=0), seg_ref[...])
    o_intra = []
    for ch in range(nch):
        base = ch * L
        attn = jnp.where(colblk == rowblk, a_diag[base:base + L, :], attn_off[ch])
        v = v_ref[base:base + L, :]
        v_bd = jnp.where(head_rows_v, jnp.concatenate([v] * H, axis=0), 0.0).astype(BF16)
        o_intra.append(_dot(attn.astype(BF16), v_bd))

    same_head = _iota((GLA_VD, GLA_KD), 0) // V == _iota((GLA_VD, GLA_KD), 1) // K
    st = st_scr[...]
    for ch in range(nch):
        base = ch * L
        o = o_intra[ch] + _dot_nt(q_dec[ch], st.astype(BF16))
        st = st * jnp.exp2(g_last[ch]) + jnp.where(same_head, st_upd[ch], 0.0)
        normed = jnp.concatenate([_rms(o[:, h * V:(h + 1) * V], gn_ref[...]) for h in range(H)], axis=1)
        o_ref[base:base + L, :] = (normed * _silu(r_ref[base:base + L, :])).astype(BF16)
    st_scr[...] = st


def _gla(proj, prm, layer, batch, seq, nch=4):
    L = GLA_CHUNK * nch
    nc = seq // L
    t = proj.shape[0]
    row = lambda b, c: b * nc + c
    lay3 = lambda shape: pl.BlockSpec((None,) + shape, lambda b, c: (layer, 0, 0))
    seg = prm["gla_seg"]
    return pl.pallas_call(
        functools.partial(_gla_body, nch=nch),
        grid=(batch, nc),
        in_specs=[
            pl.BlockSpec((L, GLA_KD), lambda b, c: (row(b, c), C_GQ // GLA_KD)),
            pl.BlockSpec((L, GLA_KD), lambda b, c: (row(b, c), C_GK // GLA_KD)),
            pl.BlockSpec((L, GLA_VD), lambda b, c: (row(b, c), C_GV // GLA_VD)),
            pl.BlockSpec((L, GLA_VD), lambda b, c: (row(b, c), C_GR // GLA_VD)),
            pl.BlockSpec((L, 128), lambda b, c: (row(b, c), C_SM // 128)),
            lay3((128, GLA_KD)),
            lay3((1, GLA_KD)),
            lay3((1, GLA_HEAD_V)),
            pl.BlockSpec(seg.shape, lambda b, c: (0, 0)),
        ],
        out_specs=pl.BlockSpec((L, GLA_VD), lambda b, c: (row(b, c), 0)),
        out_shape=jax.ShapeDtypeStruct((t, GLA_VD), BF16),
        scratch_shapes=[
            pltpu.VMEM((L, GLA_KD), F32),
            pltpu.VMEM((GLA_VD, GLA_KD), F32),
        ],
        compiler_params=_cparams(("parallel", "arbitrary")),
        name="gla",
    )(proj, proj, proj, proj, proj, prm["gla_wgk"], prm["gla_bgk"], prm["gla_norm"], seg)


def _out_proj_body(x_ref, ya_ref, yb_ref, yc_ref, w_ref, o_ref, *, tn):
    y = jnp.concatenate([ya_ref[...], yb_ref[...], yc_ref[...]], axis=1)
    for j in range(o_ref.shape[1] // tn):
        cols = slice(j * tn, (j + 1) * tn)
        o_ref[:, cols] = x_ref[:, cols] + _dot(y, w_ref[:, cols])


def _out_proj(x2d, y_mla, y_ssd, y_gla, w16, tm=512, tn=1024):
    t, d = x2d.shape
    da, db, dc = y_mla.shape[1], y_ssd.shape[1], y_gla.shape[1]
    tm = min(tm, t)
    return pl.pallas_call(
        functools.partial(_out_proj_body, tn=tn),
        grid=(t // tm,),
        in_specs=[
            pl.BlockSpec((tm, d), lambda i: (i, 0)),
            pl.BlockSpec((tm, da), lambda i: (i, 0)),
            pl.BlockSpec((tm, db), lambda i: (i, 0)),
            pl.BlockSpec((tm, dc), lambda i: (i, 0)),
            pl.BlockSpec((da + db + dc, d), lambda i: (0, 0), pipeline_mode=pl.Buffered(1)),
        ],
        out_specs=pl.BlockSpec((tm, d), lambda i: (i, 0)),
        out_shape=jax.ShapeDtypeStruct((t, d), F32),
        compiler_params=_cparams(("parallel",)),
        name="out_proj",
    )(x2d, y_mla, y_ssd, y_gla, w16)


def _ffn_body(x_ref, xh_ref, g_ref, wg_ref, wu_ref, cw_ref, cb_ref, wd_ref, fg_ref, o_ref,
              h_scr, gate_scr, *, tm, tiles_per_seq, n_out_chunks, last_layer):
    i = pl.program_id(0)
    f = pl.program_id(1)

    @pl.when(f == 0)
    def _():
        x = x_ref[...]
        h_scr[0:HALO, :] = _rms(xh_ref[...], g_ref[...]).astype(BF16)
        h_scr[HALO:HALO + tm, :] = _rms(x, g_ref[...]).astype(BF16)
        o_ref[...] = x

    gate = _dot(h_scr[...], wg_ref[...])
    keep = (i % tiles_per_seq != 0).astype(F32)
    gate_scr[0:HALO, :] = gate[0:HALO, :] * keep
    gate_scr[HALO:HALO + tm, :] = gate[HALO:, :]
    cw = cw_ref[...]
    conv = cb_ref[...] + cw[FFN_CONV - 1:FFN_CONV] * gate[HALO:, :]
    for tap in range(1, FFN_CONV):
        conv = conv + cw[FFN_CONV - 1 - tap:FFN_CONV - tap] * gate_scr[HALO - tap:HALO - tap + tm, :]
    act = (_silu(conv) * _dot(h_scr[HALO:HALO + tm, :], wu_ref[...])).astype(BF16)
    dn = o_ref.shape[1] // n_out_chunks
    for c in range(n_out_chunks):
        o_ref[:, c * dn:(c + 1) * dn] += _dot(act, wd_ref[:, c * dn:(c + 1) * dn])

    if last_layer:
        @pl.when(f == pl.num_programs(1) - 1)
        def _():
            o_ref[...] = _rms(o_ref[...], fg_ref[...])


def _ffn(x2d, prm, wg16, wu16, wd16, final_gain, layer, seq, tm=512, tf=512):
    t, d = x2d.shape
    tm = min(tm, seq)
    tiles_per_seq = seq // tm
    halo_blocks = tm // HALO
    last_layer = layer == prm["ffn_norm"].shape[0] - 1
    lay = lambda shape, idx: pl.BlockSpec((None,) + shape, idx)
    return pl.pallas_call(
        functools.partial(_ffn_body, tm=tm, tiles_per_seq=tiles_per_seq, n_out_chunks=4,
                          last_layer=last_layer),
        grid=(t // tm, D_FF // tf),
        in_specs=[
            pl.BlockSpec((tm, d), lambda i, f: (i, 0)),
            pl.BlockSpec((HALO, d), lambda i, f: (jnp.maximum(i * halo_blocks - 1, 0), 0)),
            lay((1, d), lambda i, f: (layer, 0, 0)),
            pl.BlockSpec((d, tf), lambda i, f: (0, f)),
            pl.BlockSpec((d, tf), lambda i, f: (0, f)),
            lay((FFN_CONV, tf), lambda i, f: (layer, 0, f)),
            lay((1, tf), lambda i, f: (layer, 0, f)),
            pl.BlockSpec((tf, d), lambda i, f: (f, 0)),
            pl.BlockSpec((1, d), lambda i, f: (0, 0)),
        ],
        out_specs=pl.BlockSpec((tm, d), lambda i, f: (i, 0)),
        out_shape=jax.ShapeDtypeStruct((t, d), F32),
        scratch_shapes=[
            pltpu.VMEM((HALO + tm, d), BF16),
            pltpu.VMEM((HALO + tm, tf), F32),
        ],
        compiler_params=_cparams(("parallel", "arbitrary")),
        name="ffn",
    )(x2d, x2d, prm["ffn_norm"], wg16, wu16, prm["ffn_dw_w"], prm["ffn_dw_b"], wd16, final_gain)


_W_IN_PIECES = (
    ((1344, 1856),), ((1856, 2368),),
    ((832, 1344),), ((2888, 3400),), ((3416, 3928),), ((2376, 2632),), ((2632, 2888),),
    ((2368, 2376), (3400, 3416)),
    ((0, 512),), ((512, 768),),
    ((768, 832), (768, 832)),
)


def _reorder_w_in_body(wt_ref, o_ref):
    tk = wt_ref.shape[1]
    col = 0
    for ranges in _W_IN_PIECES:
        parts = [wt_ref[a:b, :] for a, b in ranges]
        rows = sum(b - a for a, b in ranges)
        if rows % 128:
            parts.append(jnp.zeros((-rows % 128, tk), F32))
            rows += -rows % 128
        piece = parts[0] if len(parts) == 1 else jnp.concatenate(parts, axis=0)
        o_ref[:, col:col + rows] = piece.T.astype(BF16)
        col += rows
    assert col == P_WIDTH


def _reorder_w_in(w_in, tk=512):
    depth, d, d_in = w_in.shape
    return pl.pallas_call(
        _reorder_w_in_body,
        grid=(depth, d // tk),
        in_specs=[pl.BlockSpec((None, d_in, tk), lambda l, i: (l, 0, i))],
        out_specs=pl.BlockSpec((None, tk, P_WIDTH), lambda l, i: (l, i, 0)),
        out_shape=jax.ShapeDtypeStruct((depth, d, P_WIDTH), BF16),
        compiler_params=_cparams(("parallel", "parallel")),
        name="reorder_w_in",
    )(jnp.swapaxes(w_in, 1, 2))


def _prepare(seq, attn_norm, w_in, mla_q_norm, mla_w_uq, mla_kv_norm, mla_w_ukv,
             ssd_conv_w, ssd_conv_b, ssd_dt_bias, ssd_a_log, ssd_d, ssd_norm,
             gla_w_gk, gla_b_gk, gla_norm, ffn_norm, ffn_dw_w, ffn_dw_b):
    depth = w_in.shape[0]
    w_in_r = _reorder_w_in(w_in)

    wq = mla_w_uq.reshape(depth, MLA_Q_RANK, MLA_HEADS, MLA_NOPE + MLA_ROPE)
    wq = jnp.concatenate([wq, wq[..., MLA_NOPE:]], axis=-1)
    wq = wq.reshape(depth, MLA_Q_RANK, MLA_HEADS * MLA_QK).astype(BF16)
    wkv = mla_w_ukv.astype(BF16)

    inv_freq = ROPE_BASE ** (-jnp.arange(0, MLA_ROPE, 2, dtype=F32) / MLA_ROPE)
    ang = jnp.arange(seq, dtype=F32)[:, None] * inv_freq[None, :]
    cos, sin = jnp.cos(ang), jnp.sin(ang)
    zeros = jnp.zeros((seq, MLA_ROPE), F32)
    cc = jnp.concatenate([cos, cos, zeros], axis=1)
    ss = jnp.concatenate([-sin, sin, zeros], axis=1)

    pad_lanes = lambda a: jnp.pad(a, ((0, 0), (0, 128 - a.shape[1])))[:, None, :]
    wgk = jnp.zeros((depth, 128, GLA_KD), F32).at[:, SM_GLOW:SM_GLOW + GLA_GATE_RANK, :].set(gla_w_gk)

    rows = jnp.arange(GLA_SUB * GLA_KD)
    cols = jnp.arange(GLA_HEADS * GLA_CHUNK)
    seg = ((rows[:, None] % GLA_KD) // GLA_HEAD_K == cols[None, :] // GLA_CHUNK) & (
        rows[:, None] // GLA_KD == cols[None, :] % GLA_SUB)

    return dict(
        attn_norm=attn_norm[:, None, :], w_in=w_in_r,
        mla_q_norm=mla_q_norm[:, None, :], mla_kv_norm=mla_kv_norm[:, None, :], wq=wq, wkv=wkv,
        cc=cc, ss=ss,
        ssd_conv_w=ssd_conv_w, ssd_conv_b=ssd_conv_b[:, None, :],
        ssd_dtb=pad_lanes(ssd_dt_bias), ssd_dtb_t=ssd_dt_bias[:, :, None],
        ssd_alog=pad_lanes(ssd_a_log), ssd_alog_t=ssd_a_log[:, :, None],
        ssd_dexp=jnp.repeat(ssd_d, SSD_HEAD_DIM, axis=1)[:, None, :], ssd_norm=ssd_norm[:, None, :],
        gla_wgk=wgk.astype(BF16), gla_bgk=gla_b_gk[:, None, :], gla_norm=gla_norm[:, None, :],
        gla_seg=seg.astype(BF16),
        ffn_norm=ffn_norm[:, None, :], ffn_dw_w=ffn_dw_w, ffn_dw_b=ffn_dw_b[:, None, :],
    )


def kernel(x, attn_norm, w_in, mla_q_norm, mla_w_uq, mla_kv_norm, mla_w_ukv, ssd_conv_w, ssd_conv_b,
           ssd_dt_bias, ssd_a_log, ssd_d, ssd_norm, gla_w_gk, gla_b_gk, gla_norm, w_out, ffn_norm,
           ffn_w_gate, ffn_w_up, ffn_dw_w, ffn_dw_b, ffn_w_down, final_norm):
    batch, seq, d = x.shape
    prm = _prepare(seq, attn_norm, w_in, mla_q_norm, mla_w_uq, mla_kv_norm, mla_w_ukv,
                   ssd_conv_w, ssd_conv_b, ssd_dt_bias, ssd_a_log, ssd_d, ssd_norm,
                   gla_w_gk, gla_b_gk, gla_norm, ffn_norm, ffn_dw_w, ffn_dw_b)
    x2d = x.reshape(batch * seq, d)
    for layer in range(w_in.shape[0]):
        proj, qt, k, vt, wg16, wu16 = _in_proj(x2d, prm, layer, batch, seq, (ffn_w_gate, ffn_w_up))
        y_mla, wd16, wo16 = _mla_attn(qt, k, vt, layer, (ffn_w_down, w_out))
        y_mla = y_mla.reshape(batch * seq, MLA_HEADS * MLA_V)
        y_ssd = _ssd(proj, prm, layer, batch, seq)
        y_gla = _gla(proj, prm, layer, batch, seq)
        x2d = _out_proj(x2d, y_mla, y_ssd, y_gla, wo16)
        x2d = _ffn(x2d, prm, wg16, wu16, wd16, final_norm[None, :], layer, seq)
    return x2d.reshape(batch, seq, d)
```

```python
import functools

import jax
import jax.numpy as jnp
from jax import lax
from jax.experimental import pallas as pl
from jax.experimental.pallas import tpu as pltpu

F32 = jnp.float32
BF16 = jnp.bfloat16
HIGHEST = lax.Precision.HIGHEST

D_MODEL = 2048
DEPTH = 4
EPS = 1e-6
MLA_HEADS = 8
MLA_Q_RANK = 512
MLA_KV_RANK = 256
MLA_NOPE = 128
MLA_ROPE = 64
MLA_V = 128
MLA_QK = 256
MLA_VE = MLA_V + 16
LOG2_E = 1.4426950408889634
ROPE_BASE = 10000.0
SSD_HEADS = 8
SSD_HEAD_DIM = 64
SSD_STATE = 128
SSD_CONV = 4
SSD_CHUNK = 128
SSD_INNER = 512
SSD_CONV_DIM = 1024
GLA_HEADS = 4
GLA_HEAD_K = 64
GLA_HEAD_V = 128
GLA_GATE_RANK = 16
GLA_GATE_TAU = 16.0
GLA_CHUNK = 64
GLA_SUB = 16
GLA_KD = 256
GLA_VD = 512
D_FF = 5632
FFN_CONV = 3
HALO = 16

P_WIDTH = 4096
P_OUT = 3200
C_XBC, C_Z, C_GV, C_GR, C_GQ, C_GK, C_SM, C_CQ, C_CKV, C_KPE = (
    0, 1024, 1536, 2048, 2560, 2816, 3072, 3200, 3712, 3968)
SM_DT = 0
SM_GLOW = 8

VMEM_LIMIT = 52 * 1024 * 1024


def _cparams(sem):
    return pltpu.CompilerParams(dimension_semantics=sem, vmem_limit_bytes=VMEM_LIMIT)


def _rms(x, g):
    var = jnp.mean(x * x, axis=-1, keepdims=True)
    return x * lax.rsqrt(var + EPS) * g


def _softplus(x):
    return jnp.maximum(x, 0.0) + jnp.log(1.0 + jnp.exp(-jnp.abs(x)))


def _silu(x):
    return x * jax.nn.sigmoid(x)


def _dot(a, b):
    return jnp.dot(a, b, preferred_element_type=F32)


def _dot_nt(a, b):
    return lax.dot_general(a, b, (((1,), (1,)), ((), ())), preferred_element_type=F32)


def _dot_tn(a, b):
    return lax.dot_general(a, b, (((0,), (0,)), ((), ())), preferred_element_type=F32)


def _iota(shape, axis):
    return lax.broadcasted_iota(jnp.int32, shape, axis)


def _cast_specs(weights, layer, steps, step_index):
    in_specs, out_specs, out_shapes = [], [], []
    for w in weights:
        _, rows, cols = w.shape
        slab = rows // steps
        assert slab * steps == rows and slab % 16 == 0, (rows, steps)
        in_specs.append(pl.BlockSpec((None, slab, cols), lambda *g: (layer, step_index(*g), 0)))
        out_specs.append(pl.BlockSpec((slab, cols), lambda *g: (step_index(*g), 0)))
        out_shapes.append(jax.ShapeDtypeStruct((rows, cols), BF16))
    return in_specs, out_specs, out_shapes


def _cast_slabs(src_refs, dst_refs):
    for src, dst in zip(src_refs, dst_refs):
        dst[...] = src[...].astype(BF16)


def _rope(blk, cc, ss):
    return blk * cc + pltpu.roll(blk, MLA_ROPE // 2, axis=1) * ss


def _in_proj_body(x_ref, g_ref, w_ref, qn_ref, kvn_ref, wq_ref, wkv_ref, cc_ref, ss_ref, *refs, tn):
    n_cast = (len(refs) - 4) // 2
    o_ref, qt_out, k_out, vt_out = refs[n_cast:n_cast + 4]
    tm = x_ref.shape[0]
    h = _rms(x_ref[...], g_ref[...]).astype(BF16)
    mla = _dot(h, w_ref[:, P_OUT:])
    cq = mla[:, C_CQ - P_OUT:C_CKV - P_OUT]
    ckv = mla[:, C_CKV - P_OUT:C_KPE - P_OUT]
    cc = cc_ref[...]
    ss = ss_ref[...]
    kpe = _rope(mla[:, C_KPE - P_OUT:], cc, ss).astype(BF16)
    qf = _dot(_rms(cq, qn_ref[...]).astype(BF16), wq_ref[...])
    kvf = _dot(_rms(ckv, kvn_ref[...]).astype(BF16), wkv_ref[...])
    scale = (MLA_NOPE + MLA_ROPE) ** -0.5 * LOG2_E
    ones = jnp.ones((MLA_VE - MLA_V, tm), BF16)

    def head_outputs(hd):
        base = hd * MLA_QK
        q_pe = _rope(qf[:, base + MLA_NOPE:base + MLA_QK], cc, ss)
        q_h = jnp.concatenate([qf[:, base:base + MLA_NOPE], q_pe], axis=1) * scale
        qt_out[hd] = q_h.T.astype(BF16)
        k_out[hd] = jnp.concatenate([kvf[:, base:base + MLA_NOPE].astype(BF16), kpe], axis=1)
        vt_out[hd, 0:MLA_V, :] = kvf[:, base + MLA_NOPE:base + MLA_QK].T.astype(BF16)
        vt_out[hd, MLA_V:MLA_VE, :] = ones

    starts = list(range(0, P_OUT, tn))
    heads_per_chunk = -(-MLA_HEADS // len(starts))
    for idx, c0 in enumerate(starts):
        cols = slice(c0, min(c0 + tn, P_OUT))
        o_ref[:, cols] = _dot(h, w_ref[:, cols])
        for hd in range(idx * heads_per_chunk, min((idx + 1) * heads_per_chunk, MLA_HEADS)):
            head_outputs(hd)
    _cast_slabs(refs[:n_cast], refs[n_cast + 4:])


def _in_proj(x2d, prm, w_in16, layer, batch, seq, cast_weights, tm=256, tn=1024):
    t, d = x2d.shape
    tm = min(tm, seq)
    steps = t // tm
    nsb = seq // tm
    H = MLA_HEADS
    resident = lambda shape: pl.BlockSpec((None,) + shape, lambda i: (layer, 0, 0),
                                          pipeline_mode=pl.Buffered(1))
    c_in, c_out, c_shapes = _cast_specs(cast_weights, layer, steps, lambda i: i)
    return pl.pallas_call(
        functools.partial(_in_proj_body, tn=tn),
        grid=(steps,),
        in_specs=[
            pl.BlockSpec((tm, d), lambda i: (i, 0)),
            pl.BlockSpec((None, 1, d), lambda i: (layer, 0, 0)),
            pl.BlockSpec((d, P_WIDTH), lambda i: (0, 0), pipeline_mode=pl.Buffered(1)),
            pl.BlockSpec((None, 1, MLA_Q_RANK), lambda i: (layer, 0, 0)),
            pl.BlockSpec((None, 1, MLA_KV_RANK), lambda i: (layer, 0, 0)),
            resident((MLA_Q_RANK, H * MLA_QK)),
            resident((MLA_KV_RANK, H * MLA_QK)),
            pl.BlockSpec((tm, 128), lambda i: (i % nsb, 0)),
            pl.BlockSpec((tm, 128), lambda i: (i % nsb, 0)),
        ] + c_in,
        out_specs=[
            pl.BlockSpec((tm, P_OUT), lambda i: (i, 0)),
            pl.BlockSpec((None, H, MLA_QK, tm), lambda i: (i // nsb, 0, 0, i % nsb)),
            pl.BlockSpec((None, H, tm, MLA_QK), lambda i: (i // nsb, 0, i % nsb, 0)),
            pl.BlockSpec((None, H, None, MLA_VE, tm), lambda i: (i // nsb, 0, i % nsb, 0, 0)),
        ] + c_out,
        out_shape=[
            jax.ShapeDtypeStruct((t, P_OUT), F32),
            jax.ShapeDtypeStruct((batch, H, MLA_QK, seq), BF16),
            jax.ShapeDtypeStruct((batch, H, seq, MLA_QK), BF16),
            jax.ShapeDtypeStruct((batch, H, nsb, MLA_VE, tm), BF16),
        ] + c_shapes,
        compiler_params=_cparams(("arbitrary",)),
        name="in_proj",
    )(x2d, prm["attn_norm"], w_in16, prm["mla_q_norm"], prm["mla_kv_norm"], prm["wq"],
      prm["wkv"], prm["cc"], prm["ss"], *cast_weights)


def _attn_body(qt_ref, k_ref, vt_ref, *refs, tq, hp):
    n_cast = (len(refs) - 2) // 2
    o_ref, acc_scr = refs[n_cast], refs[-1]
    _cast_slabs(refs[:n_cast], refs[n_cast + 1:-1])
    qi = pl.program_id(2)
    qts = [qt_ref[h] for h in range(hp)]

    def scores(kb, h):
        start = pl.multiple_of(kb * tq, tq)
        return _dot(k_ref[h, pl.ds(start, tq), :], qts[h])

    def update(kb, h, s, m, diagonal):
        if diagonal:
            s = jnp.where(_iota((tq, tq), 0) <= _iota((tq, tq), 1), s, -jnp.inf)
        m_new = jnp.maximum(m, jnp.max(s, axis=0, keepdims=True))
        p = jnp.exp2(s - m_new).astype(BF16)
        acc_scr[h] = jnp.exp2(m - m_new) * acc_scr[h] + _dot(vt_ref[h, kb], p)
        return m_new

    def step(kb, ms, diagonal):
        ss = [scores(kb, h) for h in range(hp)]
        return tuple(update(kb, h, ss[h], ms[h], diagonal) for h in range(hp))

    def pair_step(kp, ms):
        ka, kb = 2 * kp, 2 * kp + 1
        sa = [scores(ka, h) for h in range(hp)]
        sb, ma = [], []
        for h in range(hp):
            ma.append(update(ka, h, sa[h], ms[h], False))
            sb.append(scores(kb, h))
        return tuple(update(kb, h, sb[h], ma[h], False) for h in range(hp))

    acc_scr[...] = jnp.zeros_like(acc_scr)
    init = tuple(jnp.full((1, tq), -jnp.inf, F32) for _ in range(hp))
    ms = lax.fori_loop(0, qi // 2, pair_step, init)
    ms = lax.cond(qi % 2 == 1, lambda c: step(qi - 1, c, False), lambda c: c, ms)
    step(qi, ms, True)
    outs = []
    for h in range(hp):
        acc = acc_scr[h]
        outs.append((acc[0:MLA_V, :] / acc[MLA_V:MLA_V + 1, :]).T)
    o_ref[...] = jnp.concatenate(outs, axis=1).astype(BF16)


def _mla_attn(qt, k, vt, layer, cast_weights, hp=8):
    batch, heads, _, seq = qt.shape
    tq = vt.shape[-1]
    nh, nq = heads // hp, seq // tq
    c_in, c_out, c_shapes = _cast_specs(cast_weights, layer, batch * nh * nq,
                                        lambda b, h, i: (b * nh + h) * nq + i)
    return pl.pallas_call(
        functools.partial(_attn_body, tq=tq, hp=hp),
        grid=(batch, nh, nq),
        in_specs=[
            pl.BlockSpec((None, hp, MLA_QK, tq), lambda b, h, i: (b, h, 0, i)),
            pl.BlockSpec((None, hp, seq, MLA_QK), lambda b, h, i: (b, h, 0, 0)),
            pl.BlockSpec((None, hp, seq // tq, MLA_VE, tq), lambda b, h, i: (b, h, 0, 0, 0)),
        ] + c_in,
        out_specs=[pl.BlockSpec((None, tq, hp * MLA_V), lambda b, h, i: (b, i, h))] + c_out,
        out_shape=[jax.ShapeDtypeStruct((batch, seq, heads * MLA_V), BF16)] + c_shapes,
        scratch_shapes=[pltpu.VMEM((hp, MLA_VE, tq), F32)],
        compiler_params=_cparams(("arbitrary", "arbitrary", "arbitrary")),
        name="mla_attn",
    )(qt, k, vt, *cast_weights)


def _ssd_body(z_ref, xbc_ref, sm_ref, cw_ref, cb_ref, dtb_ref, dtbt_ref, alog_ref, alogt_ref,
              dexp_ref, ng_ref, o_ref, xb_scr, st_scr):
    L, N, P = SSD_CHUNK, SSD_STATE, SSD_HEAD_DIM
    pad = 8

    @pl.when(pl.program_id(1) == 0)
    def _():
        xb_scr[0:pad, :] = jnp.zeros((pad, SSD_CONV_DIM), F32)
        st_scr[...] = jnp.zeros_like(st_scr)

    xb_scr[pad:pad + L, :] = xbc_ref[...]
    cw = cw_ref[...]
    conv = cb_ref[...] + cw[3:4] * xb_scr[pad:pad + L, :]
    for tap in range(1, SSD_CONV):
        conv = conv + cw[3 - tap:4 - tap] * xb_scr[pad - tap:pad - tap + L, :]
    xb_scr[0:pad, :] = xb_scr[L:L + pad, :]
    act = _silu(conv)
    xs = act[:, :SSD_INNER]
    bm = act[:, SSD_INNER:SSD_INNER + 2 * N]
    cm = act[:, SSD_INNER + 2 * N:]

    sm = sm_ref[...]
    lane = _iota((1, 128), 1)
    nega = jnp.where(lane < SSD_HEADS, -LOG2_E * jnp.exp(alog_ref[...]), 0.0)
    dt = _softplus(sm + dtb_ref[...])
    a = dt * nega
    smt = sm.T[0:SSD_HEADS, :]
    dtt = _softplus(smt + dtbt_ref[...])
    at = dtt * (-LOG2_E * jnp.exp(alogt_ref[...]))
    row = _iota((L, L), 0)
    col = _iota((L, L), 1)
    causal = row >= col
    acum = jnp.dot(causal.astype(F32), a, precision=HIGHEST, preferred_element_type=F32)
    acumt = jnp.dot(at, (row <= col).astype(F32), precision=HIGHEST, preferred_element_type=F32)

    colb = [jnp.broadcast_to(acum[:, h:h + 1], (L, L)) for h in range(SSD_HEADS)]
    dtb = [jnp.broadcast_to(dt[:, h:h + 1], (L, L)) for h in range(SSD_HEADS)]
    left = _iota((L, 2 * P), 1) < P

    def per_head_lanes(vals):
        return jnp.concatenate([jnp.where(left, vals[2 * p], vals[2 * p + 1])
                                for p in range(SSD_HEADS // 2)], axis=1)

    acum_e = per_head_lanes(colb)
    dt_e = per_head_lanes(dtb)
    alast_e = acum_e[L - 1:L, :]
    x_dt = xs * dt_e
    xd = (x_dt * jnp.exp2(alast_e - acum_e)).astype(BF16)
    x_dt16 = x_dt.astype(BF16)

    ht = st_scr[...]
    ht16 = ht.astype(BF16)
    heads_per_group = SSD_HEADS // 2
    g_mats, c_mats, st_new = [], [], []
    for g in range(2):
        b_g = bm[:, g * N:(g + 1) * N]
        c_g = cm[:, g * N:(g + 1) * N]
        c_mats.append(c_g)
        g_mats.append(_dot_nt(c_g.astype(BF16), b_g.astype(BF16)))
        width = heads_per_group * P
        st_new.append(_dot(b_g.T.astype(BF16), xd[:, g * width:(g + 1) * width]))
    ys = []
    for p in range(SSD_HEADS // 2):
        g = (2 * p) // heads_per_group
        rhs = jnp.concatenate([x_dt16[:, 2 * p * P:(2 * p + 2) * P],
                               ht16[:, 2 * p * P:(2 * p + 2) * P]], axis=0)
        pair = []
        for h in (2 * p, 2 * p + 1):
            rowb = jnp.broadcast_to(acumt[h:h + 1, :], (L, L))
            decay = jnp.exp2(jnp.where(causal, colb[h] - rowb, -jnp.inf))
            lhs = jnp.concatenate([g_mats[g] * decay, c_mats[g] * jnp.exp2(colb[h])], axis=1)
            pair.append(_dot(lhs.astype(BF16), rhs))
        ys.append(jnp.where(left, pair[0], pair[1]))
    y = jnp.concatenate(ys, axis=1) + xs * dexp_ref[...]
    st_scr[...] = ht * jnp.exp2(alast_e) + jnp.concatenate(st_new, axis=1)
    o_ref[...] = _rms(y * _silu(z_ref[...]), ng_ref[...]).astype(BF16)


def _ssd(proj, prm, layer, batch, seq):
    L = SSD_CHUNK
    nc = seq // L
    t = proj.shape[0]
    row = lambda b, c: b * nc + c
    lay3 = lambda shape: pl.BlockSpec((None,) + shape, lambda b, c: (layer, 0, 0))
    return pl.pallas_call(
        _ssd_body,
        grid=(batch, nc),
        in_specs=[
            pl.BlockSpec((L, SSD_INNER), lambda b, c: (row(b, c), C_Z // SSD_INNER)),
            pl.BlockSpec((L, SSD_CONV_DIM), lambda b, c: (row(b, c), C_XBC // SSD_CONV_DIM)),
            pl.BlockSpec((L, 128), lambda b, c: (row(b, c), C_SM // 128)),
            lay3((SSD_CONV, SSD_CONV_DIM)),
            lay3((1, SSD_CONV_DIM)),
            lay3((1, 128)),
            lay3((SSD_HEADS, 1)),
            lay3((1, 128)),
            lay3((SSD_HEADS, 1)),
            lay3((1, SSD_INNER)),
            lay3((1, SSD_INNER)),
        ],
        out_specs=pl.BlockSpec((L, SSD_INNER), lambda b, c: (row(b, c), 0)),
        out_shape=jax.ShapeDtypeStruct((t, SSD_INNER), BF16),
        scratch_shapes=[
            pltpu.VMEM((L + 8, SSD_CONV_DIM), F32),
            pltpu.VMEM((SSD_STATE, SSD_INNER), F32),
        ],
        compiler_params=_cparams(("parallel", "arbitrary")),
        name="ssd",
    )(proj, proj, proj, prm["ssd_conv_w"], prm["ssd_conv_b"], prm["ssd_dtb"], prm["ssd_dtb_t"],
      prm["ssd_alog"], prm["ssd_alog_t"], prm["ssd_dexp"], prm["ssd_norm"])


def _gla_body(q_ref, k_ref, v_ref, r_ref, sm_ref, wgk_ref, bgk_ref, gn_ref, seg_ref, o_ref,
              g_scr, st_scr, *, nch):
    L, C, H, K, V = GLA_CHUNK, GLA_SUB, GLA_HEADS, GLA_HEAD_K, GLA_HEAD_V
    nb = L // C
    T = nch * L

    @pl.when(pl.program_id(1) == 0)
    def _():
        st_scr[...] = jnp.zeros_like(st_scr)

    xg = _dot(sm_ref[...].astype(BF16), wgk_ref[...]) + bgk_ref[...]
    log_a = -_softplus(-xg) * (LOG2_E / GLA_GATE_TAU)
    rr = _iota((T, T), 0)
    cc = _iota((T, T), 1)
    tril = jnp.where(rr // L == cc // L, (rr >= cc).astype(F32), 0.0)
    g_all = jnp.dot(tril, log_a, precision=HIGHEST, preferred_element_type=F32)
    g_scr[...] = g_all
    qs_all = q_ref[...] * (K ** -0.5)

    head_rows_k = _iota((H * L, GLA_KD), 0) // L == _iota((H * L, GLA_KD), 1) // K
    head_rows_v = _iota((H * L, GLA_VD), 0) // L == _iota((H * L, GLA_VD), 1) // V
    rowblk = _iota((L, H * L), 0) // C
    colblk = (_iota((L, H * L), 1) % L) // C
    lrow = _iota((C, GLA_KD), 0)

    attn_off, rel_prod, q_dec, st_upd, g_last = [], [], [], [], []
    for ch in range(nch):
        base = ch * L
        g = g_all[base:base + L, :]
        qs = qs_all[base:base + L, :]
        k = k_ref[base:base + L, :]
        ends = [g_scr[base + C * j + C - 1:base + C * j + C, :] for j in range(nb)]
        end_b = jnp.concatenate([jnp.broadcast_to(e, (C, GLA_KD)) for e in ends], axis=0)
        kd = k * jnp.exp2(end_b - g)
        k_exp = jnp.where(head_rows_k, jnp.concatenate([kd] * H, axis=0), 0.0).astype(BF16)

        attn = jnp.zeros((L, H * L), F32)
        for j in range(nb - 1):
            qd = qs * jnp.exp2(jnp.minimum(g - ends[j], 0.0))
            aj = _dot_nt(qd.astype(BF16), k_exp)
            attn = jnp.where(colblk == j, jnp.where(rowblk > j, aj, 0.0), attn)

        pieces = []
        for j in range(C):
            blocks = []
            for i in range(nb):
                r0 = C * i
                gj = g_scr[base + r0 + j:base + r0 + j + 1, :]
                kj = k_ref[base + r0 + j:base + r0 + j + 1, :]
                rel = jnp.exp2(jnp.where(lrow >= j, g[r0:r0 + C, :] - gj, -jnp.inf))
                blocks.append(qs[r0:r0 + C, :] * kj * rel)
            pieces.append(jnp.concatenate(blocks, axis=0).astype(BF16))
        attn_off.append(attn)
        rel_prod.append(jnp.concatenate(pieces, axis=1))
        q_dec.append((qs * jnp.exp2(g)).astype(BF16))
        g_last.append(ends[nb - 1])
        k_dec = (k * jnp.exp2(ends[nb - 1] - g)).astype(BF16)
        st_upd.append(_dot_tn(v_ref[base:base + L, :].astype(BF16), k_dec))

    a_diag = _dot(jnp.concatenate(rel_prod, axis=0), seg_ref[...])
    o_intra = []
    for ch in range(nch):
        base = ch * L
        attn = jnp.where(colblk == rowblk, a_diag[base:base + L, :], attn_off[ch])
        v = v_ref[base:base + L, :]
        v_bd = jnp.where(head_rows_v, jnp.concatenate([v] * H, axis=0), 0.0).astype(BF16)
        o_intra.append(_dot(attn.astype(BF16), v_bd))

    same_head = _iota((GLA_VD, GLA_KD), 0) // V == _iota((GLA_VD, GLA_KD), 1) // K
    st = st_scr[...]
    for ch in range(nch):
        base = ch * L
        o = o_intra[ch] + _dot_nt(q_dec[ch], st.astype(BF16))
        st = st * jnp.exp2(g_last[ch]) + jnp.where(same_head, st_upd[ch], 0.0)
        normed = jnp.concatenate([_rms(o[:, h * V:(h + 1) * V], gn_ref[...]) for h in range(H)], axis=1)
        o_ref[base:base + L, :] = (normed * _silu(r_ref[base:base + L, :])).astype(BF16)
    st_scr[...] = st


def _gla(proj, prm, layer, batch, seq, nch=4):
    L = GLA_CHUNK * nch
    nc = seq // L
    t = proj.shape[0]
    row = lambda b, c: b * nc + c
    lay3 = lambda shape: pl.BlockSpec((None,) + shape, lambda b, c: (layer, 0, 0))
    seg = prm["gla_seg"]
    return pl.pallas_call(
        functools.partial(_gla_body, nch=nch),
        grid=(batch, nc),
        in_specs=[
            pl.BlockSpec((L, GLA_KD), lambda b, c: (row(b, c), C_GQ // GLA_KD)),
            pl.BlockSpec((L, GLA_KD), lambda b, c: (row(b, c), C_GK // GLA_KD)),
            pl.BlockSpec((L, GLA_VD), lambda b, c: (row(b, c), C_GV // GLA_VD)),
            pl.BlockSpec((L, GLA_VD), lambda b, c: (row(b, c), C_GR // GLA_VD)),
            pl.BlockSpec((L, 128), lambda b, c: (row(b, c), C_SM // 128)),
            lay3((128, GLA_KD)),
            lay3((1, GLA_KD)),
            lay3((1, GLA_HEAD_V)),
            pl.BlockSpec(seg.shape, lambda b, c: (0, 0)),
        ],
        out_specs=pl.BlockSpec((L, GLA_VD), lambda b, c: (row(b, c), 0)),
        out_shape=jax.ShapeDtypeStruct((t, GLA_VD), BF16),
        scratch_shapes=[
            pltpu.VMEM((L, GLA_KD), F32),
            pltpu.VMEM((GLA_VD, GLA_KD), F32),
        ],
        compiler_params=_cparams(("parallel", "arbitrary")),
        name="gla",
    )(proj, proj, proj, proj, proj, prm["gla_wgk"], prm["gla_bgk"], prm["gla_norm"], seg)


def _out_proj_body(x_ref, ya_ref, yb_ref, yc_ref, w_ref, *refs, tn):
    o_ref = refs[-2] if len(refs) == 3 else refs[0]
    y = jnp.concatenate([ya_ref[...], yb_ref[...], yc_ref[...]], axis=1)
    for j in range(o_ref.shape[1] // tn):
        cols = slice(j * tn, (j + 1) * tn)
        o_ref[:, cols] = x_ref[:, cols] + _dot(y, w_ref[:, cols])
    if len(refs) == 3:
        _reorder_w_in_body(refs[0], refs[2])


def _out_proj(x2d, y_mla, y_ssd, y_gla, w16, w_in_t, next_layer, tm=512, tn=1024):
    t, d = x2d.shape
    da, db, dc = y_mla.shape[1], y_ssd.shape[1], y_gla.shape[1]
    tm = min(tm, t)
    steps = t // tm
    in_specs = [
        pl.BlockSpec((tm, d), lambda i: (i, 0)),
        pl.BlockSpec((tm, da), lambda i: (i, 0)),
        pl.BlockSpec((tm, db), lambda i: (i, 0)),
        pl.BlockSpec((tm, dc), lambda i: (i, 0)),
        pl.BlockSpec((da + db + dc, d), lambda i: (0, 0), pipeline_mode=pl.Buffered(1)),
    ]
    out_specs = [pl.BlockSpec((tm, d), lambda i: (i, 0))]
    out_shape = [jax.ShapeDtypeStruct((t, d), F32)]
    args = [x2d, y_mla, y_ssd, y_gla, w16]
    if next_layer is not None:
        slab = d // steps
        assert slab * steps == d and slab % 128 == 0, (d, steps)
        in_specs.append(pl.BlockSpec((None, w_in_t.shape[1], slab), lambda i: (next_layer, 0, i)))
        out_specs.append(pl.BlockSpec((slab, P_WIDTH), lambda i: (i, 0)))
        out_shape.append(jax.ShapeDtypeStruct((d, P_WIDTH), BF16))
        args.append(w_in_t)
    outs = pl.pallas_call(
        functools.partial(_out_proj_body, tn=tn),
        grid=(steps,),
        in_specs=in_specs,
        out_specs=out_specs,
        out_shape=out_shape,
        compiler_params=_cparams(("arbitrary",)),
        name="out_proj",
    )(*args)
    return outs if next_layer is not None else (outs[0], None)


def _ffn_body(x_ref, xh_ref, g_ref, wg_ref, wu_ref, cw_ref, cb_ref, wd_ref, fg_ref, o_ref,
              h_scr, gate_scr, *, tm, tiles_per_seq, n_out_chunks, last_layer):
    i = pl.program_id(0)
    f = pl.program_id(1)

    @pl.when(f == 0)
    def _():
        x = x_ref[...]
        h_scr[0:HALO, :] = _rms(xh_ref[...], g_ref[...]).astype(BF16)
        h_scr[HALO:HALO + tm, :] = _rms(x, g_ref[...]).astype(BF16)
        o_ref[...] = x

    gate = _dot(h_scr[...], wg_ref[...])
    keep = (i % tiles_per_seq != 0).astype(F32)
    gate_scr[0:HALO, :] = gate[0:HALO, :] * keep
    gate_scr[HALO:HALO + tm, :] = gate[HALO:, :]
    cw = cw_ref[...]
    conv = cb_ref[...] + cw[FFN_CONV - 1:FFN_CONV] * gate[HALO:, :]
    for tap in range(1, FFN_CONV):
        conv = conv + cw[FFN_CONV - 1 - tap:FFN_CONV - tap] * gate_scr[HALO - tap:HALO - tap + tm, :]
    act = (_silu(conv) * _dot(h_scr[HALO:HALO + tm, :], wu_ref[...])).astype(BF16)
    dn = o_ref.shape[1] // n_out_chunks
    for c in range(n_out_chunks):
        o_ref[:, c * dn:(c + 1) * dn] += _dot(act, wd_ref[:, c * dn:(c + 1) * dn])

    if last_layer:
        @pl.when(f == pl.num_programs(1) - 1)
        def _():
            o_ref[...] = _rms(o_ref[...], fg_ref[...])


def _ffn(x2d, prm, wg16, wu16, wd16, final_gain, layer, seq, tm=512, tf=512):
    t, d = x2d.shape
    tm = min(tm, seq)
    tiles_per_seq = seq // tm
    halo_blocks = tm // HALO
    last_layer = layer == prm["ffn_norm"].shape[0] - 1
    lay = lambda shape, idx: pl.BlockSpec((None,) + shape, idx)
    return pl.pallas_call(
        functools.partial(_ffn_body, tm=tm, tiles_per_seq=tiles_per_seq, n_out_chunks=4,
                          last_layer=last_layer),
        grid=(t // tm, D_FF // tf),
        in_specs=[
            pl.BlockSpec((tm, d), lambda i, f: (i, 0)),
            pl.BlockSpec((HALO, d), lambda i, f: (jnp.maximum(i * halo_blocks - 1, 0), 0)),
            lay((1, d), lambda i, f: (layer, 0, 0)),
            pl.BlockSpec((d, tf), lambda i, f: (0, f)),
            pl.BlockSpec((d, tf), lambda i, f: (0, f)),
            lay((FFN_CONV, tf), lambda i, f: (layer, 0, f)),
            lay((1, tf), lambda i, f: (layer, 0, f)),
            pl.BlockSpec((tf, d), lambda i, f: (f, 0)),
            pl.BlockSpec((1, d), lambda i, f: (0, 0)),
        ],
        out_specs=pl.BlockSpec((tm, d), lambda i, f: (i, 0)),
        out_shape=jax.ShapeDtypeStruct((t, d), F32),
        scratch_shapes=[
            pltpu.VMEM((HALO + tm, d), BF16),
            pltpu.VMEM((HALO + tm, tf), F32),
        ],
        compiler_params=_cparams(("parallel", "arbitrary")),
        name="ffn",
    )(x2d, x2d, prm["ffn_norm"], wg16, wu16, prm["ffn_dw_w"], prm["ffn_dw_b"], wd16, final_gain)


_W_IN_PIECES = (
    ((1344, 1856),), ((1856, 2368),),
    ((832, 1344),), ((2888, 3400),), ((3416, 3928),), ((2376, 2632),), ((2632, 2888),),
    ((2368, 2376), (3400, 3416)),
    ((0, 512),), ((512, 768),),
    ((768, 832), (768, 832)),
)


def _reorder_w_in_body(wt_ref, o_ref):
    tk = wt_ref.shape[1]
    col = 0
    for ranges in _W_IN_PIECES:
        parts = [wt_ref[a:b, :] for a, b in ranges]
        rows = sum(b - a for a, b in ranges)
        if rows % 128:
            parts.append(jnp.zeros((-rows % 128, tk), F32))
            rows += -rows % 128
        piece = parts[0] if len(parts) == 1 else jnp.concatenate(parts, axis=0)
        o_ref[:, col:col + rows] = piece.T.astype(BF16)
        col += rows
    assert col == P_WIDTH


def _reorder_w_in(w_in_t, layer, tk=512):
    _, d_in, d = w_in_t.shape
    return pl.pallas_call(
        _reorder_w_in_body,
        grid=(d // tk,),
        in_specs=[pl.BlockSpec((None, d_in, tk), lambda i: (layer, 0, i))],
        out_specs=pl.BlockSpec((tk, P_WIDTH), lambda i: (i, 0)),
        out_shape=jax.ShapeDtypeStruct((d, P_WIDTH), BF16),
        compiler_params=_cparams(("parallel",)),
        name="reorder_w_in",
    )(w_in_t)


def _prepare(seq, attn_norm, mla_q_norm, mla_w_uq, mla_kv_norm, mla_w_ukv,
             ssd_conv_w, ssd_conv_b, ssd_dt_bias, ssd_a_log, ssd_d, ssd_norm,
             gla_w_gk, gla_b_gk, gla_norm, ffn_norm, ffn_dw_w, ffn_dw_b):
    depth = mla_w_uq.shape[0]
    wq = mla_w_uq.reshape(depth, MLA_Q_RANK, MLA_HEADS, MLA_NOPE + MLA_ROPE)
    wq = jnp.concatenate([wq, wq[..., MLA_NOPE:]], axis=-1)
    wq = wq.reshape(depth, MLA_Q_RANK, MLA_HEADS * MLA_QK).astype(BF16)
    wkv = mla_w_ukv.astype(BF16)

    inv_freq = ROPE_BASE ** (-jnp.arange(0, MLA_ROPE, 2, dtype=F32) / MLA_ROPE)
    ang = jnp.arange(seq, dtype=F32)[:, None] * inv_freq[None, :]
    cos, sin = jnp.cos(ang), jnp.sin(ang)
    zeros = jnp.zeros((seq, MLA_ROPE), F32)
    cc = jnp.concatenate([cos, cos, zeros], axis=1)
    ss = jnp.concatenate([-sin, sin, zeros], axis=1)

    pad_lanes = lambda a: jnp.pad(a, ((0, 0), (0, 128 - a.shape[1])))[:, None, :]
    wgk = jnp.zeros((depth, 128, GLA_KD), F32).at[:, SM_GLOW:SM_GLOW + GLA_GATE_RANK, :].set(gla_w_gk)

    rows = jnp.arange(GLA_SUB * GLA_KD)
    cols = jnp.arange(GLA_HEADS * GLA_CHUNK)
    seg = ((rows[:, None] % GLA_KD) // GLA_HEAD_K == cols[None, :] // GLA_CHUNK) & (
        rows[:, None] // GLA_KD == cols[None, :] % GLA_SUB)

    return dict(
        attn_norm=attn_norm[:, None, :],
        mla_q_norm=mla_q_norm[:, None, :], mla_kv_norm=mla_kv_norm[:, None, :], wq=wq, wkv=wkv,
        cc=cc, ss=ss,
        ssd_conv_w=ssd_conv_w, ssd_conv_b=ssd_conv_b[:, None, :],
        ssd_dtb=pad_lanes(ssd_dt_bias), ssd_dtb_t=ssd_dt_bias[:, :, None],
        ssd_alog=pad_lanes(ssd_a_log), ssd_alog_t=ssd_a_log[:, :, None],
        ssd_dexp=jnp.repeat(ssd_d, SSD_HEAD_DIM, axis=1)[:, None, :], ssd_norm=ssd_norm[:, None, :],
        gla_wgk=wgk.astype(BF16), gla_bgk=gla_b_gk[:, None, :], gla_norm=gla_norm[:, None, :],
        gla_seg=seg.astype(BF16),
        ffn_norm=ffn_norm[:, None, :], ffn_dw_w=ffn_dw_w, ffn_dw_b=ffn_dw_b[:, None, :],
    )


def kernel(x, attn_norm, w_in, mla_q_norm, mla_w_uq, mla_kv_norm, mla_w_ukv, ssd_conv_w, ssd_conv_b,
           ssd_dt_bias, ssd_a_log, ssd_d, ssd_norm, gla_w_gk, gla_b_gk, gla_norm, w_out, ffn_norm,
           ffn_w_gate, ffn_w_up, ffn_dw_w, ffn_dw_b, ffn_w_down, final_norm):
    batch, seq, d = x.shape
    prm = _prepare(seq, attn_norm, mla_q_norm, mla_w_uq, mla_kv_norm, mla_w_ukv,
                   ssd_conv_w, ssd_conv_b, ssd_dt_bias, ssd_a_log, ssd_d, ssd_norm,
                   gla_w_gk, gla_b_gk, gla_norm, ffn_norm, ffn_dw_w, ffn_dw_b)
    depth = w_in.shape[0]
    w_in_t = jnp.swapaxes(w_in, 1, 2)
    w_in16 = _reorder_w_in(w_in_t, 0)
    x2d = x.reshape(batch * seq, d)
    for layer in range(depth):
        proj, qt, k, vt, wg16, wu16 = _in_proj(x2d, prm, w_in16, layer, batch, seq,
                                               (ffn_w_gate, ffn_w_up))
        y_mla, wd16, wo16 = _mla_attn(qt, k, vt, layer, (ffn_w_down, w_out))
        y_mla = y_mla.reshape(batch * seq, MLA_HEADS * MLA_V)
        y_ssd = _ssd(proj, prm, layer, batch, seq)
        y_gla = _gla(proj, prm, layer, batch, seq)
        x2d, w_in16 = _out_proj(x2d, y_mla, y_ssd, y_gla, wo16, w_in_t,
                                layer + 1 if layer + 1 < depth else None)
        x2d = _ffn(x2d, prm, wg16, wu16, wd16, final_norm[None, :], layer, seq)
    return x2d.reshape(batch, seq, d)
```

```python
import functools

import jax
import jax.numpy as jnp
from jax import lax
from jax.experimental import pallas as pl
from jax.experimental.pallas import tpu as pltpu

F32 = jnp.float32
BF16 = jnp.bfloat16

D_MODEL = 2048
DEPTH = 4
EPS = 1e-6
MLA_HEADS = 8
MLA_Q_RANK = 512
MLA_KV_RANK = 256
MLA_NOPE = 128
MLA_ROPE = 64
MLA_V = 128
MLA_QK = 256
MLA_VE = MLA_V + 16
LOG2_E = 1.4426950408889634
ROPE_BASE = 10000.0
SSD_HEADS = 8
SSD_HEAD_DIM = 64
SSD_STATE = 128
SSD_CONV = 4
SSD_CHUNK = 128
SSD_INNER = 512
SSD_CONV_DIM = 1024
GLA_HEADS = 4
GLA_HEAD_K = 64
GLA_HEAD_V = 128
GLA_GATE_RANK = 16
GLA_GATE_TAU = 16.0
GLA_CHUNK = 64
GLA_SUB = 16
GLA_KD = 256
GLA_VD = 512
D_FF = 5632
FFN_CONV = 3
HALO = 16

P_WIDTH = 4096
P_OUT = 3200
C_XBC, C_Z, C_GV, C_GR, C_GQ, C_GK, C_SM, C_CQ, C_CKV, C_KPE = (
    0, 1024, 1536, 2048, 2560, 2816, 3072, 3200, 3712, 3968)
SM_DT = 0
SM_GLOW = 8

VMEM_LIMIT = 52 * 1024 * 1024


def _cparams(sem):
    return pltpu.CompilerParams(dimension_semantics=sem, vmem_limit_bytes=VMEM_LIMIT)


def _rms(x, g):
    var = jnp.mean(x * x, axis=-1, keepdims=True)
    return x * lax.rsqrt(var + EPS) * g


def _softplus(x):
    return jnp.maximum(x, 0.0) + jnp.log(1.0 + jnp.exp(-jnp.abs(x)))


def _silu(x):
    return x * jax.nn.sigmoid(x)


def _dot(a, b):
    return jnp.dot(a, b, preferred_element_type=F32)


def _dot_nt(a, b):
    return lax.dot_general(a, b, (((1,), (1,)), ((), ())), preferred_element_type=F32)


def _dot_tn(a, b):
    return lax.dot_general(a, b, (((0,), (0,)), ((), ())), preferred_element_type=F32)


def _iota(shape, axis):
    return lax.broadcasted_iota(jnp.int32, shape, axis)


def _split3(x):
    hi = x.astype(BF16)
    r = x - hi.astype(F32)
    mid = r.astype(BF16)
    lo = (r - mid.astype(F32)).astype(BF16)
    return hi, mid, lo


def _select_sum(mask, x):
    m = mask.astype(BF16)
    hi, mid, lo = _split3(x)
    return (_dot(m, lo) + _dot(m, mid)) + _dot(m, hi)


def _select_sum_t(x, mask):
    m = mask.astype(BF16)
    hi, mid, lo = _split3(x)
    return (_dot(lo, m) + _dot(mid, m)) + _dot(hi, m)


def _cast_specs(weights, layer, steps, step_index):
    in_specs, out_specs, out_shapes = [], [], []
    for w in weights:
        _, rows, cols = w.shape
        slab = rows // steps
        assert slab * steps == rows and slab % 16 == 0, (rows, steps)
        in_specs.append(pl.BlockSpec((None, slab, cols), lambda *g: (layer, step_index(*g), 0)))
        out_specs.append(pl.BlockSpec((slab, cols), lambda *g: (step_index(*g), 0)))
        out_shapes.append(jax.ShapeDtypeStruct((rows, cols), BF16))
    return in_specs, out_specs, out_shapes


def _cast_slabs(src_refs, dst_refs):
    for src, dst in zip(src_refs, dst_refs):
        dst[...] = src[...].astype(BF16)


def _rope(blk, cc, ss):
    return blk * cc + pltpu.roll(blk, MLA_ROPE // 2, axis=1) * ss


def _in_proj_body(x_ref, g_ref, w_ref, qn_ref, kvn_ref, wq_ref, wkv_ref, cc_ref, ss_ref, *refs, tn):
    n_cast = (len(refs) - 4) // 2
    o_ref, qt_out, k_out, vt_out = refs[n_cast:n_cast + 4]
    tm = x_ref.shape[0]
    h = _rms(x_ref[...], g_ref[...]).astype(BF16)
    mla = _dot(h, w_ref[:, P_OUT:])
    cq = mla[:, C_CQ - P_OUT:C_CKV - P_OUT]
    ckv = mla[:, C_CKV - P_OUT:C_KPE - P_OUT]
    cc = cc_ref[...]
    ss = ss_ref[...]
    kpe = _rope(mla[:, C_KPE - P_OUT:], cc, ss).astype(BF16)
    qf = _dot(_rms(cq, qn_ref[...]).astype(BF16), wq_ref[...])
    kvf = _dot(_rms(ckv, kvn_ref[...]).astype(BF16), wkv_ref[...])
    scale = (MLA_NOPE + MLA_ROPE) ** -0.5 * LOG2_E
    ones = jnp.ones((MLA_VE - MLA_V, tm), BF16)

    def head_outputs(hd):
        base = hd * MLA_QK
        q_pe = _rope(qf[:, base + MLA_NOPE:base + MLA_QK], cc, ss)
        q_h = jnp.concatenate([qf[:, base:base + MLA_NOPE], q_pe], axis=1) * scale
        qt_out[hd] = q_h.T.astype(BF16)
        k_out[hd] = jnp.concatenate([kvf[:, base:base + MLA_NOPE].astype(BF16), kpe], axis=1)
        vt_out[hd, 0:MLA_V, :] = kvf[:, base + MLA_NOPE:base + MLA_QK].T.astype(BF16)
        vt_out[hd, MLA_V:MLA_VE, :] = ones

    starts = list(range(0, P_OUT, tn))
    heads_per_chunk = -(-MLA_HEADS // len(starts))
    for idx, c0 in enumerate(starts):
        cols = slice(c0, min(c0 + tn, P_OUT))
        o_ref[:, cols] = _dot(h, w_ref[:, cols])
        for hd in range(idx * heads_per_chunk, min((idx + 1) * heads_per_chunk, MLA_HEADS)):
            head_outputs(hd)
    _cast_slabs(refs[:n_cast], refs[n_cast + 4:])


def _in_proj(x2d, prm, w_in16, layer, batch, seq, cast_weights, tm=256, tn=1024):
    t, d = x2d.shape
    tm = min(tm, seq)
    steps = t // tm
    nsb = seq // tm
    H = MLA_HEADS
    resident = lambda shape: pl.BlockSpec((None,) + shape, lambda i: (layer, 0, 0),
                                          pipeline_mode=pl.Buffered(1))
    c_in, c_out, c_shapes = _cast_specs(cast_weights, layer, steps, lambda i: i)
    return pl.pallas_call(
        functools.partial(_in_proj_body, tn=tn),
        grid=(steps,),
        in_specs=[
            pl.BlockSpec((tm, d), lambda i: (i, 0)),
            pl.BlockSpec((None, 1, d), lambda i: (layer, 0, 0)),
            pl.BlockSpec((d, P_WIDTH), lambda i: (0, 0), pipeline_mode=pl.Buffered(1)),
            pl.BlockSpec((None, 1, MLA_Q_RANK), lambda i: (layer, 0, 0)),
            pl.BlockSpec((None, 1, MLA_KV_RANK), lambda i: (layer, 0, 0)),
            resident((MLA_Q_RANK, H * MLA_QK)),
            resident((MLA_KV_RANK, H * MLA_QK)),
            pl.BlockSpec((tm, 128), lambda i: (i % nsb, 0)),
            pl.BlockSpec((tm, 128), lambda i: (i % nsb, 0)),
        ] + c_in,
        out_specs=[
            pl.BlockSpec((tm, P_OUT), lambda i: (i, 0)),
            pl.BlockSpec((None, H, MLA_QK, tm), lambda i: (i // nsb, 0, 0, i % nsb)),
            pl.BlockSpec((None, H, tm, MLA_QK), lambda i: (i // nsb, 0, i % nsb, 0)),
            pl.BlockSpec((None, H, None, MLA_VE, tm), lambda i: (i // nsb, 0, i % nsb, 0, 0)),
        ] + c_out,
        out_shape=[
            jax.ShapeDtypeStruct((t, P_OUT), F32),
            jax.ShapeDtypeStruct((batch, H, MLA_QK, seq), BF16),
            jax.ShapeDtypeStruct((batch, H, seq, MLA_QK), BF16),
            jax.ShapeDtypeStruct((batch, H, nsb, MLA_VE, tm), BF16),
        ] + c_shapes,
        compiler_params=_cparams(("arbitrary",)),
        name="in_proj",
    )(x2d, prm["attn_norm"], w_in16, prm["mla_q_norm"], prm["mla_kv_norm"], prm["wq"],
      prm["wkv"], prm["cc"], prm["ss"], *cast_weights)


def _attn_body(qt_ref, k_ref, vt_ref, *refs, tq, hp):
    n_cast = (len(refs) - 2) // 2
    o_ref, acc_scr = refs[n_cast], refs[-1]
    _cast_slabs(refs[:n_cast], refs[n_cast + 1:-1])
    qi = pl.program_id(2)
    qts = [qt_ref[h] for h in range(hp)]

    def scores(kb, h):
        start = pl.multiple_of(kb * tq, tq)
        return _dot(k_ref[h, pl.ds(start, tq), :], qts[h])

    def update(kb, h, s, m, diagonal):
        if diagonal:
            s = jnp.where(_iota((tq, tq), 0) <= _iota((tq, tq), 1), s, -jnp.inf)
        m_new = jnp.maximum(m, jnp.max(s, axis=0, keepdims=True))
        p = jnp.exp2(s - m_new).astype(BF16)
        acc_scr[h] = jnp.exp2(m - m_new) * acc_scr[h] + _dot(vt_ref[h, kb], p)
        return m_new

    def step(kb, ms, diagonal):
        ss = [scores(kb, h) for h in range(hp)]
        return tuple(update(kb, h, ss[h], ms[h], diagonal) for h in range(hp))

    def pair_step(kp, ms):
        ka, kb = 2 * kp, 2 * kp + 1
        sa = [scores(ka, h) for h in range(hp)]
        sb, ma = [], []
        for h in range(hp):
            ma.append(update(ka, h, sa[h], ms[h], False))
            sb.append(scores(kb, h))
        return tuple(update(kb, h, sb[h], ma[h], False) for h in range(hp))

    acc_scr[...] = jnp.zeros_like(acc_scr)
    init = tuple(jnp.full((1, tq), -jnp.inf, F32) for _ in range(hp))
    ms = lax.fori_loop(0, qi // 2, pair_step, init)
    ms = lax.cond(qi % 2 == 1, lambda c: step(qi - 1, c, False), lambda c: c, ms)
    step(qi, ms, True)
    outs = []
    for h in range(hp):
        acc = acc_scr[h]
        outs.append((acc[0:MLA_V, :] / acc[MLA_V:MLA_V + 1, :]).T)
    o_ref[...] = jnp.concatenate(outs, axis=1).astype(BF16)


def _mla_attn(qt, k, vt, layer, cast_weights, hp=8):
    batch, heads, _, seq = qt.shape
    tq = vt.shape[-1]
    nh, nq = heads // hp, seq // tq
    c_in, c_out, c_shapes = _cast_specs(cast_weights, layer, batch * nh * nq,
                                        lambda b, h, i: (b * nh + h) * nq + i)
    return pl.pallas_call(
        functools.partial(_attn_body, tq=tq, hp=hp),
        grid=(batch, nh, nq),
        in_specs=[
            pl.BlockSpec((None, hp, MLA_QK, tq), lambda b, h, i: (b, h, 0, i)),
            pl.BlockSpec((None, hp, seq, MLA_QK), lambda b, h, i: (b, h, 0, 0)),
            pl.BlockSpec((None, hp, seq // tq, MLA_VE, tq), lambda b, h, i: (b, h, 0, 0, 0)),
        ] + c_in,
        out_specs=[pl.BlockSpec((None, tq, hp * MLA_V), lambda b, h, i: (b, i, h))] + c_out,
        out_shape=[jax.ShapeDtypeStruct((batch, seq, heads * MLA_V), BF16)] + c_shapes,
        scratch_shapes=[pltpu.VMEM((hp, MLA_VE, tq), F32)],
        compiler_params=_cparams(("arbitrary", "arbitrary", "arbitrary")),
        name="mla_attn",
    )(qt, k, vt, *cast_weights)


def _ssd_body(z_ref, xbc_ref, sm_ref, cw_ref, cb_ref, dtb_ref, dtbt_ref, alog_ref, alogt_ref,
              dexp_ref, ng_ref, o_ref, xb_scr, st_scr, *, nch):
    L, N, P = SSD_CHUNK, SSD_STATE, SSD_HEAD_DIM
    T = nch * L
    pad = 8

    @pl.when(pl.program_id(1) == 0)
    def _():
        xb_scr[0:pad, :] = jnp.zeros((pad, SSD_CONV_DIM), F32)
        st_scr[...] = jnp.zeros_like(st_scr)

    xb_scr[pad:pad + T, :] = xbc_ref[...]
    cw = cw_ref[...]
    conv = cb_ref[...] + cw[3:4] * xb_scr[pad:pad + T, :]
    for tap in range(1, SSD_CONV):
        conv = conv + cw[3 - tap:4 - tap] * xb_scr[pad - tap:pad - tap + T, :]
    xb_scr[0:pad, :] = xb_scr[T:T + pad, :]
    act = _silu(conv)
    xs = act[:, :SSD_INNER]
    bm = act[:, SSD_INNER:SSD_INNER + 2 * N]
    cm = act[:, SSD_INNER + 2 * N:]

    sm = sm_ref[...]
    lane = _iota((1, 128), 1)
    nega = jnp.where(lane < SSD_HEADS, -LOG2_E * jnp.exp(alog_ref[...]), 0.0)
    dt = _softplus(sm + dtb_ref[...])
    a = dt * nega
    smt = sm.T[0:SSD_HEADS, :]
    dtt = _softplus(smt + dtbt_ref[...])
    at = dtt * (-LOG2_E * jnp.exp(alogt_ref[...]))
    row = _iota((T, T), 0)
    col = _iota((T, T), 1)
    same_chunk = row // L == col // L
    acum = _select_sum(jnp.where(same_chunk, (row >= col).astype(F32), 0.0), a)
    acumt = _select_sum_t(at, jnp.where(same_chunk, (row <= col).astype(F32), 0.0))

    colb = [jnp.broadcast_to(acum[:, h:h + 1], (T, L)) for h in range(SSD_HEADS)]
    dtb = [jnp.broadcast_to(dt[:, h:h + 1], (T, L)) for h in range(SSD_HEADS)]
    left_t = _iota((T, 2 * P), 1) < P
    left = _iota((L, 2 * P), 1) < P
    causal = _iota((L, L), 0) >= _iota((L, L), 1)

    def per_head_lanes(vals):
        return jnp.concatenate([jnp.where(left_t, vals[2 * p], vals[2 * p + 1])
                                for p in range(SSD_HEADS // 2)], axis=1)

    acum_e = per_head_lanes(colb)
    x_dt = xs * per_head_lanes(dtb)
    x_dt16 = x_dt.astype(BF16)

    ht = st_scr[...]
    heads_per_group = SSD_HEADS // 2
    width = heads_per_group * P
    y_chunks = []
    for c in range(nch):
        rows = slice(c * L, (c + 1) * L)
        alast_e = acum_e[(c + 1) * L - 1:(c + 1) * L, :]
        xd = (x_dt[rows, :] * jnp.exp2(alast_e - acum_e[rows, :])).astype(BF16)
        ht16 = ht.astype(BF16)
        g_mats, c_mats, st_new = [], [], []
        for g in range(2):
            b_g = bm[rows, g * N:(g + 1) * N]
            c_g = cm[rows, g * N:(g + 1) * N]
            c_mats.append(c_g)
            g_mats.append(_dot_nt(c_g.astype(BF16), b_g.astype(BF16)))
            st_new.append(_dot(b_g.T.astype(BF16), xd[:, g * width:(g + 1) * width]))
        ys = []
        for p in range(SSD_HEADS // 2):
            g = (2 * p) // heads_per_group
            rhs = jnp.concatenate([x_dt16[rows, 2 * p * P:(2 * p + 2) * P],
                                   ht16[:, 2 * p * P:(2 * p + 2) * P]], axis=0)
            pair = []
            for h in (2 * p, 2 * p + 1):
                colb_h = colb[h][rows, :]
                rowb = jnp.broadcast_to(acumt[h:h + 1, rows], (L, L))
                decay = jnp.exp2(jnp.where(causal, colb_h - rowb, -jnp.inf))
                lhs = jnp.concatenate([g_mats[g] * decay, c_mats[g] * jnp.exp2(colb_h)], axis=1)
                pair.append(_dot(lhs.astype(BF16), rhs))
            ys.append(jnp.where(left, pair[0], pair[1]))
        y_chunks.append(jnp.concatenate(ys, axis=1))
        ht = ht * jnp.exp2(alast_e) + jnp.concatenate(st_new, axis=1)
    st_scr[...] = ht
    y = jnp.concatenate(y_chunks, axis=0) + xs * dexp_ref[...]
    o_ref[...] = _rms(y * _silu(z_ref[...]), ng_ref[...]).astype(BF16)


def _ssd(proj, prm, layer, batch, seq, nch=4):
    L = SSD_CHUNK * nch
    nc = seq // L
    t = proj.shape[0]
    row = lambda b, c: b * nc + c
    lay3 = lambda shape: pl.BlockSpec((None,) + shape, lambda b, c: (layer, 0, 0))
    return pl.pallas_call(
        functools.partial(_ssd_body, nch=nch),
        grid=(batch, nc),
        in_specs=[
            pl.BlockSpec((L, SSD_INNER), lambda b, c: (row(b, c), C_Z // SSD_INNER)),
            pl.BlockSpec((L, SSD_CONV_DIM), lambda b, c: (row(b, c), C_XBC // SSD_CONV_DIM)),
            pl.BlockSpec((L, 128), lambda b, c: (row(b, c), C_SM // 128)),
            lay3((SSD_CONV, SSD_CONV_DIM)),
            lay3((1, SSD_CONV_DIM)),
            lay3((1, 128)),
            lay3((SSD_HEADS, 1)),
            lay3((1, 128)),
            lay3((SSD_HEADS, 1)),
            lay3((1, SSD_INNER)),
            lay3((1, SSD_INNER)),
        ],
        out_specs=pl.BlockSpec((L, SSD_INNER), lambda b, c: (row(b, c), 0)),
        out_shape=jax.ShapeDtypeStruct((t, SSD_INNER), BF16),
        scratch_shapes=[
            pltpu.VMEM((L + 8, SSD_CONV_DIM), F32),
            pltpu.VMEM((SSD_STATE, SSD_INNER), F32),
        ],
        compiler_params=_cparams(("parallel", "arbitrary")),
        name="ssd",
    )(proj, proj, proj, prm["ssd_conv_w"], prm["ssd_conv_b"], prm["ssd_dtb"], prm["ssd_dtb_t"],
      prm["ssd_alog"], prm["ssd_alog_t"], prm["ssd_dexp"], prm["ssd_norm"])


def _gla_body(q_ref, k_ref, v_ref, r_ref, sm_ref, wgk_ref, bgk_ref, gn_ref, seg_ref, o_ref,
              g_scr, st_scr, *, nch):
    L, C, H, K, V = GLA_CHUNK, GLA_SUB, GLA_HEADS, GLA_HEAD_K, GLA_HEAD_V
    nb = L // C
    T = nch * L

    @pl.when(pl.program_id(1) == 0)
    def _():
        st_scr[...] = jnp.zeros_like(st_scr)

    xg = _dot(sm_ref[...].astype(BF16), wgk_ref[...]) + bgk_ref[...]
    log_a = -_softplus(-xg) * (LOG2_E / GLA_GATE_TAU)
    rr = _iota((T, T), 0)
    cc = _iota((T, T), 1)
    tril = jnp.where(rr // L == cc // L, (rr >= cc).astype(F32), 0.0)
    g_all = _select_sum(tril, log_a)
    g_scr[...] = g_all
    qs_all = q_ref[...] * (K ** -0.5)

    head_rows_k = _iota((H * L, GLA_KD), 0) // L == _iota((H * L, GLA_KD), 1) // K
    head_rows_v = _iota((H * L, GLA_VD), 0) // L == _iota((H * L, GLA_VD), 1) // V
    rowblk = _iota((L, H * L), 0) // C
    colblk = (_iota((L, H * L), 1) % L) // C
    lrow = _iota((C, GLA_KD), 0)

    attn_off, rel_prod, q_dec, st_upd, g_last = [], [], [], [], []
    for ch in range(nch):
        base = ch * L
        g = g_all[base:base + L, :]
        qs = qs_all[base:base + L, :]
        k = k_ref[base:base + L, :]
        ends = [g_scr[base + C * j + C - 1:base + C * j + C, :] for j in range(nb)]
        end_b = jnp.concatenate([jnp.broadcast_to(e, (C, GLA_KD)) for e in ends], axis=0)
        kd = k * jnp.exp2(end_b - g)
        k_exp = jnp.where(head_rows_k, jnp.concatenate([kd] * H, axis=0), 0.0).astype(BF16)

        attn = jnp.zeros((L, H * L), F32)
        for j in range(nb - 1):
            qd = qs * jnp.exp2(jnp.minimum(g - ends[j], 0.0))
            aj = _dot_nt(qd.astype(BF16), k_exp)
            attn = jnp.where(colblk == j, jnp.where(rowblk > j, aj, 0.0), attn)

        pieces = []
        for j in range(C):
            blocks = []
            for i in range(nb):
                r0 = C * i
                gj = g_scr[base + r0 + j:base + r0 + j + 1, :]
                kj = k_ref[base + r0 + j:base + r0 + j + 1, :]
                rel = jnp.exp2(jnp.where(lrow >= j, g[r0:r0 + C, :] - gj, -jnp.inf))
                blocks.append(qs[r0:r0 + C, :] * kj * rel)
            pieces.append(jnp.concatenate(blocks, axis=0).astype(BF16))
        attn_off.append(attn)
        rel_prod.append(jnp.concatenate(pieces, axis=1))
        q_dec.append((qs * jnp.exp2(g)).astype(BF16))
        g_last.append(ends[nb - 1])
        k_dec = (k * jnp.exp2(ends[nb - 1] - g)).astype(BF16)
        st_upd.append(_dot_tn(v_ref[base:base + L, :].astype(BF16), k_dec))

    a_diag = _dot(jnp.concatenate(rel_prod, axis=0), seg_ref[...])
    o_intra = []
    for ch in range(nch):
        base = ch * L
        attn = jnp.where(colblk == rowblk, a_diag[base:base + L, :], attn_off[ch])
        v = v_ref[base:base + L, :]
        v_bd = jnp.where(head_rows_v, jnp.concatenate([v] * H, axis=0), 0.0).astype(BF16)
        o_intra.append(_dot(attn.astype(BF16), v_bd))

    same_head = _iota((GLA_VD, GLA_KD), 0) // V == _iota((GLA_VD, GLA_KD), 1) // K
    st = st_scr[...]
    for ch in range(nch):
        base = ch * L
        o = o_intra[ch] + _dot_nt(q_dec[ch], st.astype(BF16))
        st = st * jnp.exp2(g_last[ch]) + jnp.where(same_head, st_upd[ch], 0.0)
        normed = jnp.concatenate([_rms(o[:, h * V:(h + 1) * V], gn_ref[...]) for h in range(H)], axis=1)
        o_ref[base:base + L, :] = (normed * _silu(r_ref[base:base + L, :])).astype(BF16)
    st_scr[...] = st


def _gla(proj, prm, layer, batch, seq, nch=4):
    L = GLA_CHUNK * nch
    nc = seq // L
    t = proj.shape[0]
    row = lambda b, c: b * nc + c
    lay3 = lambda shape: pl.BlockSpec((None,) + shape, lambda b, c: (layer, 0, 0))
    seg = prm["gla_seg"]
    return pl.pallas_call(
        functools.partial(_gla_body, nch=nch),
        grid=(batch, nc),
        in_specs=[
            pl.BlockSpec((L, GLA_KD), lambda b, c: (row(b, c), C_GQ // GLA_KD)),
            pl.BlockSpec((L, GLA_KD), lambda b, c: (row(b, c), C_GK // GLA_KD)),
            pl.BlockSpec((L, GLA_VD), lambda b, c: (row(b, c), C_GV // GLA_VD)),
            pl.BlockSpec((L, GLA_VD), lambda b, c: (row(b, c), C_GR // GLA_VD)),
            pl.BlockSpec((L, 128), lambda b, c: (row(b, c), C_SM // 128)),
            lay3((128, GLA_KD)),
            lay3((1, GLA_KD)),
            lay3((1, GLA_HEAD_V)),
            pl.BlockSpec(seg.shape, lambda b, c: (0, 0)),
        ],
        out_specs=pl.BlockSpec((L, GLA_VD), lambda b, c: (row(b, c), 0)),
        out_shape=jax.ShapeDtypeStruct((t, GLA_VD), BF16),
        scratch_shapes=[
            pltpu.VMEM((L, GLA_KD), F32),
            pltpu.VMEM((GLA_VD, GLA_KD), F32),
        ],
        compiler_params=_cparams(("parallel", "arbitrary")),
        name="gla",
    )(proj, proj, proj, proj, proj, prm["gla_wgk"], prm["gla_bgk"], prm["gla_norm"], seg)


def _out_proj_body(x_ref, ya_ref, yb_ref, yc_ref, w_ref, *refs, tn):
    o_ref = refs[-2] if len(refs) == 3 else refs[0]
    y = jnp.concatenate([ya_ref[...], yb_ref[...], yc_ref[...]], axis=1)
    for j in range(o_ref.shape[1] // tn):
        cols = slice(j * tn, (j + 1) * tn)
        o_ref[:, cols] = x_ref[:, cols] + _dot(y, w_ref[:, cols])
    if len(refs) == 3:
        _reorder_w_in_body(refs[0], refs[2])


def _out_proj(x2d, y_mla, y_ssd, y_gla, w16, w_in_t, next_layer, tm=512, tn=1024):
    t, d = x2d.shape
    da, db, dc = y_mla.shape[1], y_ssd.shape[1], y_gla.shape[1]
    tm = min(tm, t)
    steps = t // tm
    in_specs = [
        pl.BlockSpec((tm, d), lambda i: (i, 0)),
        pl.BlockSpec((tm, da), lambda i: (i, 0)),
        pl.BlockSpec((tm, db), lambda i: (i, 0)),
        pl.BlockSpec((tm, dc), lambda i: (i, 0)),
        pl.BlockSpec((da + db + dc, d), lambda i: (0, 0), pipeline_mode=pl.Buffered(1)),
    ]
    out_specs = [pl.BlockSpec((tm, d), lambda i: (i, 0))]
    out_shape = [jax.ShapeDtypeStruct((t, d), F32)]
    args = [x2d, y_mla, y_ssd, y_gla, w16]
    if next_layer is not None:
        slab = d // steps
        assert slab * steps == d and slab % 128 == 0, (d, steps)
        in_specs.append(pl.BlockSpec((None, w_in_t.shape[1], slab), lambda i: (next_layer, 0, i)))
        out_specs.append(pl.BlockSpec((slab, P_WIDTH), lambda i: (i, 0)))
        out_shape.append(jax.ShapeDtypeStruct((d, P_WIDTH), BF16))
        args.append(w_in_t)
    outs = pl.pallas_call(
        functools.partial(_out_proj_body, tn=tn),
        grid=(steps,),
        in_specs=in_specs,
        out_specs=out_specs,
        out_shape=out_shape,
        compiler_params=_cparams(("arbitrary",)),
        name="out_proj",
    )(*args)
    return outs if next_layer is not None else (outs[0], None)


def _ffn_body(x_ref, xh_ref, g_ref, wg_ref, wu_ref, cw_ref, cb_ref, wd_ref, fg_ref, o_ref,
              h_scr, gate_scr, *, tm, tiles_per_seq, n_out_chunks, last_layer):
    i = pl.program_id(0)
    f = pl.program_id(1)

    @pl.when(f == 0)
    def _():
        x = x_ref[...]
        h_scr[0:HALO, :] = _rms(xh_ref[...], g_ref[...]).astype(BF16)
        h_scr[HALO:HALO + tm, :] = _rms(x, g_ref[...]).astype(BF16)
        o_ref[...] = x

    gate = _dot(h_scr[...], wg_ref[...])
    keep = (i % tiles_per_seq != 0).astype(F32)
    gate_scr[0:HALO, :] = gate[0:HALO, :] * keep
    gate_scr[HALO:HALO + tm, :] = gate[HALO:, :]
    cw = cw_ref[...]
    conv = cb_ref[...] + cw[FFN_CONV - 1:FFN_CONV] * gate[HALO:, :]
    for tap in range(1, FFN_CONV):
        conv = conv + cw[FFN_CONV - 1 - tap:FFN_CONV - tap] * gate_scr[HALO - tap:HALO - tap + tm, :]
    act = (_silu(conv) * _dot(h_scr[HALO:HALO + tm, :], wu_ref[...])).astype(BF16)
    dn = o_ref.shape[1] // n_out_chunks
    for c in range(n_out_chunks):
        o_ref[:, c * dn:(c + 1) * dn] += _dot(act, wd_ref[:, c * dn:(c + 1) * dn])

    if last_layer:
        @pl.when(f == pl.num_programs(1) - 1)
        def _():
            o_ref[...] = _rms(o_ref[...], fg_ref[...])


def _ffn(x2d, prm, wg16, wu16, wd16, final_gain, layer, seq, tm=512, tf=512):
    t, d = x2d.shape
    tm = min(tm, seq)
    tiles_per_seq = seq // tm
    halo_blocks = tm // HALO
    last_layer = layer == prm["ffn_norm"].shape[0] - 1
    lay = lambda shape, idx: pl.BlockSpec((None,) + shape, idx)
    return pl.pallas_call(
        functools.partial(_ffn_body, tm=tm, tiles_per_seq=tiles_per_seq, n_out_chunks=4,
                          last_layer=last_layer),
        grid=(t // tm, D_FF // tf),
        in_specs=[
            pl.BlockSpec((tm, d), lambda i, f: (i, 0)),
            pl.BlockSpec((HALO, d), lambda i, f: (jnp.maximum(i * halo_blocks - 1, 0), 0)),
            lay((1, d), lambda i, f: (layer, 0, 0)),
            pl.BlockSpec((d, tf), lambda i, f: (0, f)),
            pl.BlockSpec((d, tf), lambda i, f: (0, f)),
            lay((FFN_CONV, tf), lambda i, f: (layer, 0, f)),
            lay((1, tf), lambda i, f: (layer, 0, f)),
            pl.BlockSpec((tf, d), lambda i, f: (f, 0)),
            pl.BlockSpec((1, d), lambda i, f: (0, 0)),
        ],
        out_specs=pl.BlockSpec((tm, d), lambda i, f: (i, 0)),
        out_shape=jax.ShapeDtypeStruct((t, d), F32),
        scratch_shapes=[
            pltpu.VMEM((HALO + tm, d), BF16),
            pltpu.VMEM((HALO + tm, tf), F32),
        ],
        compiler_params=_cparams(("parallel", "arbitrary")),
        name="ffn",
    )(x2d, x2d, prm["ffn_norm"], wg16, wu16, prm["ffn_dw_w"], prm["ffn_dw_b"], wd16, final_gain)


_W_IN_PIECES = (
    ((1344, 1856),), ((1856, 2368),),
    ((832, 1344),), ((2888, 3400),), ((3416, 3928),), ((2376, 2632),), ((2632, 2888),),
    ((2368, 2376), (3400, 3416)),
    ((0, 512),), ((512, 768),),
    ((768, 832), (768, 832)),
)


def _reorder_w_in_body(wt_ref, o_ref):
    tk = wt_ref.shape[1]
    col = 0
    for ranges in _W_IN_PIECES:
        parts = [wt_ref[a:b, :] for a, b in ranges]
        rows = sum(b - a for a, b in ranges)
        if rows % 128:
            parts.append(jnp.zeros((-rows % 128, tk), F32))
            rows += -rows % 128
        piece = parts[0] if len(parts) == 1 else jnp.concatenate(parts, axis=0)
        o_ref[:, col:col + rows] = piece.T.astype(BF16)
        col += rows
    assert col == P_WIDTH


def _reorder_w_in(w_in_t, layer, tk=512):
    _, d_in, d = w_in_t.shape
    return pl.pallas_call(
        _reorder_w_in_body,
        grid=(d // tk,),
        in_specs=[pl.BlockSpec((None, d_in, tk), lambda i: (layer, 0, i))],
        out_specs=pl.BlockSpec((tk, P_WIDTH), lambda i: (i, 0)),
        out_shape=jax.ShapeDtypeStruct((d, P_WIDTH), BF16),
        compiler_params=_cparams(("parallel",)),
        name="reorder_w_in",
    )(w_in_t)


def _prepare(seq, attn_norm, mla_q_norm, mla_w_uq, mla_kv_norm, mla_w_ukv,
             ssd_conv_w, ssd_conv_b, ssd_dt_bias, ssd_a_log, ssd_d, ssd_norm,
             gla_w_gk, gla_b_gk, gla_norm, ffn_norm, ffn_dw_w, ffn_dw_b):
    depth = mla_w_uq.shape[0]
    wq = mla_w_uq.reshape(depth, MLA_Q_RANK, MLA_HEADS, MLA_NOPE + MLA_ROPE)
    wq = jnp.concatenate([wq, wq[..., MLA_NOPE:]], axis=-1)
    wq = wq.reshape(depth, MLA_Q_RANK, MLA_HEADS * MLA_QK).astype(BF16)
    wkv = mla_w_ukv.astype(BF16)

    inv_freq = ROPE_BASE ** (-jnp.arange(0, MLA_ROPE, 2, dtype=F32) / MLA_ROPE)
    ang = jnp.arange(seq, dtype=F32)[:, None] * inv_freq[None, :]
    cos, sin = jnp.cos(ang), jnp.sin(ang)
    zeros = jnp.zeros((seq, MLA_ROPE), F32)
    cc = jnp.concatenate([cos, cos, zeros], axis=1)
    ss = jnp.concatenate([-sin, sin, zeros], axis=1)

    pad_lanes = lambda a: jnp.pad(a, ((0, 0), (0, 128 - a.shape[1])))[:, None, :]
    wgk = jnp.zeros((depth, 128, GLA_KD), F32).at[:, SM_GLOW:SM_GLOW + GLA_GATE_RANK, :].set(gla_w_gk)

    rows = jnp.arange(GLA_SUB * GLA_KD)
    cols = jnp.arange(GLA_HEADS * GLA_CHUNK)
    seg = ((rows[:, None] % GLA_KD) // GLA_HEAD_K == cols[None, :] // GLA_CHUNK) & (
        rows[:, None] // GLA_KD == cols[None, :] % GLA_SUB)

    return dict(
        attn_norm=attn_norm[:, None, :],
        mla_q_norm=mla_q_norm[:, None, :], mla_kv_norm=mla_kv_norm[:, None, :], wq=wq, wkv=wkv,
        cc=cc, ss=ss,
        ssd_conv_w=ssd_conv_w, ssd_conv_b=ssd_conv_b[:, None, :],
        ssd_dtb=pad_lanes(ssd_dt_bias), ssd_dtb_t=ssd_dt_bias[:, :, None],
        ssd_alog=pad_lanes(ssd_a_log), ssd_alog_t=ssd_a_log[:, :, None],
        ssd_dexp=jnp.repeat(ssd_d, SSD_HEAD_DIM, axis=1)[:, None, :], ssd_norm=ssd_norm[:, None, :],
        gla_wgk=wgk.astype(BF16), gla_bgk=gla_b_gk[:, None, :], gla_norm=gla_norm[:, None, :],
        gla_seg=seg.astype(BF16),
        ffn_norm=ffn_norm[:, None, :], ffn_dw_w=ffn_dw_w, ffn_dw_b=ffn_dw_b[:, None, :],
    )


def kernel(x, attn_norm, w_in, mla_q_norm, mla_w_uq, mla_kv_norm, mla_w_ukv, ssd_conv_w, ssd_conv_b,
           ssd_dt_bias, ssd_a_log, ssd_d, ssd_norm, gla_w_gk, gla_b_gk, gla_norm, w_out, ffn_norm,
           ffn_w_gate, ffn_w_up, ffn_dw_w, ffn_dw_b, ffn_w_down, final_norm):
    batch, seq, d = x.shape
    prm = _prepare(seq, attn_norm, mla_q_norm, mla_w_uq, mla_kv_norm, mla_w_ukv,
                   ssd_conv_w, ssd_conv_b, ssd_dt_bias, ssd_a_log, ssd_d, ssd_norm,
                   gla_w_gk, gla_b_gk, gla_norm, ffn_norm, ffn_dw_w, ffn_dw_b)
    depth = w_in.shape[0]
    w_in_t = jnp.swapaxes(w_in, 1, 2)
    w_in16 = _reorder_w_in(w_in_t, 0)
    x2d = x.reshape(batch * seq, d)
    for layer in range(depth):
        proj, qt, k, vt, wg16, wu16 = _in_proj(x2d, prm, w_in16, layer, batch, seq,
                                               (ffn_w_gate, ffn_w_up))
        y_mla, wd16, wo16 = _mla_attn(qt, k, vt, layer, (ffn_w_down, w_out))
        y_mla = y_mla.reshape(batch * seq, MLA_HEADS * MLA_V)
        y_ssd = _ssd(proj, prm, layer, batch, seq)
        y_gla = _gla(proj, prm, layer, batch, seq)
        x2d, w_in16 = _out_proj(x2d, y_mla, y_ssd, y_gla, wo16, w_in_t,
                                layer + 1 if layer + 1 < depth else None)
        x2d = _ffn(x2d, prm, wg16, wu16, wd16, final_norm[None, :], layer, seq)
    return x2d.reshape(batch, seq, d)
```

```python
import functools

import jax
import jax.numpy as jnp
from jax import lax
from jax.experimental import pallas as pl
from jax.experimental.pallas import tpu as pltpu

F32 = jnp.float32
BF16 = jnp.bfloat16

D_MODEL = 2048
DEPTH = 4
EPS = 1e-6
MLA_HEADS = 8
MLA_Q_RANK = 512
MLA_KV_RANK = 256
MLA_NOPE = 128
MLA_ROPE = 64
MLA_V = 128
MLA_QK = 256
MLA_VE = MLA_V + 16
LOG2_E = 1.4426950408889634
ROPE_BASE = 10000.0
SSD_HEADS = 8
SSD_HEAD_DIM = 64
SSD_STATE = 128
SSD_CONV = 4
SSD_CHUNK = 128
SSD_INNER = 512
SSD_CONV_DIM = 1024
GLA_HEADS = 4
GLA_HEAD_K = 64
GLA_HEAD_V = 128
GLA_GATE_RANK = 16
GLA_GATE_TAU = 16.0
GLA_CHUNK = 64
GLA_SUB = 16
GLA_KD = 256
GLA_VD = 512
D_FF = 5632
FFN_CONV = 3
HALO = 16

P_WIDTH = 4096
P_OUT = 3200
C_XBC, C_Z, C_GV, C_GR, C_GQ, C_GK, C_SM, C_CQ, C_CKV, C_KPE = (
    0, 1024, 1536, 2048, 2560, 2816, 3072, 3200, 3712, 3968)
SM_DT = 0
SM_GLOW = 8

VMEM_LIMIT = 52 * 1024 * 1024


def _cparams(sem):
    return pltpu.CompilerParams(dimension_semantics=sem, vmem_limit_bytes=VMEM_LIMIT)


def _rms(x, g):
    var = jnp.mean(x * x, axis=-1, keepdims=True)
    return x * lax.rsqrt(var + EPS) * g


def _softplus(x):
    return jnp.maximum(x, 0.0) + jnp.log(1.0 + jnp.exp(-jnp.abs(x)))


def _silu(x):
    return x * jax.nn.sigmoid(x)


def _dot(a, b):
    return jnp.dot(a, b, preferred_element_type=F32)


def _dot_nt(a, b):
    return lax.dot_general(a, b, (((1,), (1,)), ((), ())), preferred_element_type=F32)


def _dot_tn(a, b):
    return lax.dot_general(a, b, (((0,), (0,)), ((), ())), preferred_element_type=F32)


def _iota(shape, axis):
    return lax.broadcasted_iota(jnp.int32, shape, axis)


def _split3(x):
    hi = x.astype(BF16)
    r = x - hi.astype(F32)
    mid = r.astype(BF16)
    lo = (r - mid.astype(F32)).astype(BF16)
    return hi, mid, lo


def _select_sum(mask, x):
    m = mask.astype(BF16)
    hi, mid, lo = _split3(x)
    return (_dot(m, lo) + _dot(m, mid)) + _dot(m, hi)


def _select_sum_t(x, mask):
    m = mask.astype(BF16)
    hi, mid, lo = _split3(x)
    return (_dot(lo, m) + _dot(mid, m)) + _dot(hi, m)


def _cast_specs(weights, layer, steps, step_index):
    in_specs, out_specs, out_shapes = [], [], []
    for w in weights:
        _, rows, cols = w.shape
        slab = rows // steps
        assert slab * steps == rows and slab % 16 == 0, (rows, steps)
        in_specs.append(pl.BlockSpec((None, slab, cols), lambda *g: (layer, step_index(*g), 0)))
        out_specs.append(pl.BlockSpec((slab, cols), lambda *g: (step_index(*g), 0)))
        out_shapes.append(jax.ShapeDtypeStruct((rows, cols), BF16))
    return in_specs, out_specs, out_shapes


def _cast_slabs(src_refs, dst_refs):
    for src, dst in zip(src_refs, dst_refs):
        dst[...] = src[...].astype(BF16)


def _rope(blk, cc, ss):
    return blk * cc + pltpu.roll(blk, MLA_ROPE // 2, axis=1) * ss


def _in_proj_body(x_ref, g_ref, w_ref, qn_ref, kvn_ref, wq_ref, wkv_ref, cc_ref, ss_ref, *refs, tn):
    n_cast = (len(refs) - 4) // 2
    o_ref, qt_out, k_out, vt_out = refs[n_cast:n_cast + 4]
    tm = x_ref.shape[0]
    h = _rms(x_ref[...], g_ref[...]).astype(BF16)
    mla = _dot(h, w_ref[:, P_OUT:])
    cq = mla[:, C_CQ - P_OUT:C_CKV - P_OUT]
    ckv = mla[:, C_CKV - P_OUT:C_KPE - P_OUT]
    cc = cc_ref[...]
    ss = ss_ref[...]
    kpe = _rope(mla[:, C_KPE - P_OUT:], cc, ss).astype(BF16)
    qf = _dot(_rms(cq, qn_ref[...]).astype(BF16), wq_ref[...])
    kvf = _dot(_rms(ckv, kvn_ref[...]).astype(BF16), wkv_ref[...])
    scale = (MLA_NOPE + MLA_ROPE) ** -0.5 * LOG2_E
    ones = jnp.ones((MLA_VE - MLA_V, tm), BF16)

    def head_outputs(hd):
        base = hd * MLA_QK
        q_pe = _rope(qf[:, base + MLA_NOPE:base + MLA_QK], cc, ss)
        q_h = jnp.concatenate([qf[:, base:base + MLA_NOPE], q_pe], axis=1) * scale
        qt_out[hd] = q_h.T.astype(BF16)
        k_out[hd] = jnp.concatenate([kvf[:, base:base + MLA_NOPE].astype(BF16), kpe], axis=1)
        vt_out[hd, 0:MLA_V, :] = kvf[:, base + MLA_NOPE:base + MLA_QK].T.astype(BF16)
        vt_out[hd, MLA_V:MLA_VE, :] = ones

    starts = list(range(0, P_OUT, tn))
    heads_per_chunk = -(-MLA_HEADS // len(starts))
    for idx, c0 in enumerate(starts):
        cols = slice(c0, min(c0 + tn, P_OUT))
        o_ref[:, cols] = _dot(h, w_ref[:, cols])
        for hd in range(idx * heads_per_chunk, min((idx + 1) * heads_per_chunk, MLA_HEADS)):
            head_outputs(hd)
    _cast_slabs(refs[:n_cast], refs[n_cast + 4:])


def _in_proj(x2d, prm, w_in16, layer, batch, seq, cast_weights, tm=256, tn=1024):
    t, d = x2d.shape
    tm = min(tm, seq)
    steps = t // tm
    nsb = seq // tm
    H = MLA_HEADS
    resident = lambda shape: pl.BlockSpec((None,) + shape, lambda i: (layer, 0, 0),
                                          pipeline_mode=pl.Buffered(1))
    c_in, c_out, c_shapes = _cast_specs(cast_weights, layer, steps, lambda i: i)
    return pl.pallas_call(
        functools.partial(_in_proj_body, tn=tn),
        grid=(steps,),
        in_specs=[
            pl.BlockSpec((tm, d), lambda i: (i, 0)),
            pl.BlockSpec((None, 1, d), lambda i: (layer, 0, 0)),
            pl.BlockSpec((d, P_WIDTH), lambda i: (0, 0), pipeline_mode=pl.Buffered(1)),
            pl.BlockSpec((None, 1, MLA_Q_RANK), lambda i: (layer, 0, 0)),
            pl.BlockSpec((None, 1, MLA_KV_RANK), lambda i: (layer, 0, 0)),
            resident((MLA_Q_RANK, H * MLA_QK)),
            resident((MLA_KV_RANK, H * MLA_QK)),
            pl.BlockSpec((tm, 128), lambda i: (i % nsb, 0)),
            pl.BlockSpec((tm, 128), lambda i: (i % nsb, 0)),
        ] + c_in,
        out_specs=[
            pl.BlockSpec((tm, P_OUT), lambda i: (i, 0)),
            pl.BlockSpec((None, H, MLA_QK, tm), lambda i: (i // nsb, 0, 0, i % nsb)),
            pl.BlockSpec((None, H, tm, MLA_QK), lambda i: (i // nsb, 0, i % nsb, 0)),
            pl.BlockSpec((None, H, None, MLA_VE, tm), lambda i: (i // nsb, 0, i % nsb, 0, 0)),
        ] + c_out,
        out_shape=[
            jax.ShapeDtypeStruct((t, P_OUT), F32),
            jax.ShapeDtypeStruct((batch, H, MLA_QK, seq), BF16),
            jax.ShapeDtypeStruct((batch, H, seq, MLA_QK), BF16),
            jax.ShapeDtypeStruct((batch, H, nsb, MLA_VE, tm), BF16),
        ] + c_shapes,
        compiler_params=_cparams(("arbitrary",)),
        name="in_proj",
    )(x2d, prm["attn_norm"], w_in16, prm["mla_q_norm"], prm["mla_kv_norm"], prm["wq"],
      prm["wkv"], prm["cc"], prm["ss"], *cast_weights)


def _attn_body(qt_ref, k_ref, vt_ref, *refs, tq, hp):
    n_cast = (len(refs) - 2) // 2
    o_ref, acc_scr = refs[n_cast], refs[-1]
    _cast_slabs(refs[:n_cast], refs[n_cast + 1:-1])
    qi = pl.program_id(2)
    qts = [qt_ref[h] for h in range(hp)]

    def scores(kb, h):
        start = pl.multiple_of(kb * tq, tq)
        return _dot(k_ref[h, pl.ds(start, tq), :], qts[h])

    def update(kb, h, s, m, diagonal):
        if diagonal:
            s = jnp.where(_iota((tq, tq), 0) <= _iota((tq, tq), 1), s, -jnp.inf)
        m_new = jnp.maximum(m, jnp.max(s, axis=0, keepdims=True))
        p = jnp.exp2(s - m_new).astype(BF16)
        acc_scr[h] = jnp.exp2(m - m_new) * acc_scr[h] + _dot(vt_ref[h, kb], p)
        return m_new

    def step(kb, ms, diagonal):
        ss = [scores(kb, h) for h in range(hp)]
        return tuple(update(kb, h, ss[h], ms[h], diagonal) for h in range(hp))

    def pair_step(kp, ms):
        ka, kb = 2 * kp, 2 * kp + 1
        sa = [scores(ka, h) for h in range(hp)]
        sb, ma = [], []
        for h in range(hp):
            ma.append(update(ka, h, sa[h], ms[h], False))
            sb.append(scores(kb, h))
        return tuple(update(kb, h, sb[h], ma[h], False) for h in range(hp))

    acc_scr[...] = jnp.zeros_like(acc_scr)
    init = tuple(jnp.full((1, tq), -jnp.inf, F32) for _ in range(hp))
    ms = lax.fori_loop(0, qi // 2, pair_step, init)
    ms = lax.cond(qi % 2 == 1, lambda c: step(qi - 1, c, False), lambda c: c, ms)
    step(qi, ms, True)
    outs = []
    for h in range(hp):
        acc = acc_scr[h]
        outs.append((acc[0:MLA_V, :] / acc[MLA_V:MLA_V + 1, :]).T)
    o_ref[...] = jnp.concatenate(outs, axis=1).astype(BF16)


def _mla_attn(qt, k, vt, layer, cast_weights, hp=8):
    batch, heads, _, seq = qt.shape
    tq = vt.shape[-1]
    nh, nq = heads // hp, seq // tq
    c_in, c_out, c_shapes = _cast_specs(cast_weights, layer, batch * nh * nq,
                                        lambda b, h, i: (b * nh + h) * nq + i)
    return pl.pallas_call(
        functools.partial(_attn_body, tq=tq, hp=hp),
        grid=(batch, nh, nq),
        in_specs=[
            pl.BlockSpec((None, hp, MLA_QK, tq), lambda b, h, i: (b, h, 0, i)),
            pl.BlockSpec((None, hp, seq, MLA_QK), lambda b, h, i: (b, h, 0, 0)),
            pl.BlockSpec((None, hp, seq // tq, MLA_VE, tq), lambda b, h, i: (b, h, 0, 0, 0)),
        ] + c_in,
        out_specs=[pl.BlockSpec((None, tq, hp * MLA_V), lambda b, h, i: (b, i, h))] + c_out,
        out_shape=[jax.ShapeDtypeStruct((batch, seq, heads * MLA_V), BF16)] + c_shapes,
        scratch_shapes=[pltpu.VMEM((hp, MLA_VE, tq), F32)],
        compiler_params=_cparams(("arbitrary", "arbitrary", "arbitrary")),
        name="mla_attn",
    )(qt, k, vt, *cast_weights)


def _ssd_body(z_ref, xbc_ref, sm_ref, cw_ref, cb_ref, dtb_ref, dtbt_ref, alog_ref, alogt_ref,
              dexp_ref, ng_ref, o_ref, xb_scr, st_scr, *, nch):
    L, N, P = SSD_CHUNK, SSD_STATE, SSD_HEAD_DIM
    T = nch * L
    pad = 8

    @pl.when(pl.program_id(1) == 0)
    def _():
        xb_scr[0:pad, :] = jnp.zeros((pad, SSD_CONV_DIM), F32)
        st_scr[...] = jnp.zeros_like(st_scr)

    xb_scr[pad:pad + T, :] = xbc_ref[...]
    cw = cw_ref[...]
    conv = cb_ref[...] + cw[3:4] * xb_scr[pad:pad + T, :]
    for tap in range(1, SSD_CONV):
        conv = conv + cw[3 - tap:4 - tap] * xb_scr[pad - tap:pad - tap + T, :]
    xb_scr[0:pad, :] = xb_scr[T:T + pad, :]
    act = _silu(conv)
    xs = act[:, :SSD_INNER]
    bm = act[:, SSD_INNER:SSD_INNER + 2 * N]
    cm = act[:, SSD_INNER + 2 * N:]

    sm = sm_ref[...]
    lane = _iota((1, 128), 1)
    nega = jnp.where(lane < SSD_HEADS, -LOG2_E * jnp.exp(alog_ref[...]), 0.0)
    dt = _softplus(sm + dtb_ref[...])
    a = dt * nega
    smt = sm.T[0:SSD_HEADS, :]
    dtt = _softplus(smt + dtbt_ref[...])
    at = dtt * (-LOG2_E * jnp.exp(alogt_ref[...]))
    row = _iota((T, T), 0)
    col = _iota((T, T), 1)
    same_chunk = row // L == col // L
    acum = _select_sum(jnp.where(same_chunk, (row >= col).astype(F32), 0.0), a)
    acumt = _select_sum_t(at, jnp.where(same_chunk, (row <= col).astype(F32), 0.0))

    colb = [jnp.broadcast_to(acum[:, h:h + 1], (T, L)) for h in range(SSD_HEADS)]
    dtb = [jnp.broadcast_to(dt[:, h:h + 1], (T, L)) for h in range(SSD_HEADS)]
    left_t = _iota((T, 2 * P), 1) < P
    left = _iota((L, 2 * P), 1) < P
    causal = _iota((L, L), 0) >= _iota((L, L), 1)

    def per_head_lanes(vals):
        return jnp.concatenate([jnp.where(left_t, vals[2 * p], vals[2 * p + 1])
                                for p in range(SSD_HEADS // 2)], axis=1)

    acum_e = per_head_lanes(colb)
    x_dt = xs * per_head_lanes(dtb)
    x_dt16 = x_dt.astype(BF16)

    ht = st_scr[...]
    heads_per_group = SSD_HEADS // 2
    width = heads_per_group * P
    y_chunks = []
    for c in range(nch):
        rows = slice(c * L, (c + 1) * L)
        alast_e = acum_e[(c + 1) * L - 1:(c + 1) * L, :]
        xd = (x_dt[rows, :] * jnp.exp2(alast_e - acum_e[rows, :])).astype(BF16)
        ht16 = ht.astype(BF16)
        g_mats, c_mats, st_new = [], [], []
        for g in range(2):
            b_g = bm[rows, g * N:(g + 1) * N]
            c_g = cm[rows, g * N:(g + 1) * N]
            c_mats.append(c_g)
            g_mats.append(_dot_nt(c_g.astype(BF16), b_g.astype(BF16)))
            st_new.append(_dot(b_g.T.astype(BF16), xd[:, g * width:(g + 1) * width]))
        ys = []
        for p in range(SSD_HEADS // 2):
            g = (2 * p) // heads_per_group
            rhs = jnp.concatenate([x_dt16[rows, 2 * p * P:(2 * p + 2) * P],
                                   ht16[:, 2 * p * P:(2 * p + 2) * P]], axis=0)
            pair = []
            for h in (2 * p, 2 * p + 1):
                colb_h = colb[h][rows, :]
                rowb = jnp.broadcast_to(acumt[h:h + 1, rows], (L, L))
                decay = jnp.exp2(jnp.where(causal, colb_h - rowb, -jnp.inf))
                lhs = jnp.concatenate([g_mats[g] * decay, c_mats[g] * jnp.exp2(colb_h)], axis=1)
                pair.append(_dot(lhs.astype(BF16), rhs))
            ys.append(jnp.where(left, pair[0], pair[1]))
        y_chunks.append(jnp.concatenate(ys, axis=1))
        ht = ht * jnp.exp2(alast_e) + jnp.concatenate(st_new, axis=1)
    st_scr[...] = ht
    y = jnp.concatenate(y_chunks, axis=0) + xs * dexp_ref[...]
    o_ref[...] = _rms(y * _silu(z_ref[...]), ng_ref[...]).astype(BF16)


def _ssd(proj, prm, layer, batch, seq, nch=4):
    L = SSD_CHUNK * nch
    nc = seq // L
    t = proj.shape[0]
    row = lambda b, c: b * nc + c
    lay3 = lambda shape: pl.BlockSpec((None,) + shape, lambda b, c: (layer, 0, 0))
    return pl.pallas_call(
        functools.partial(_ssd_body, nch=nch),
        grid=(batch, nc),
        in_specs=[
            pl.BlockSpec((L, SSD_INNER), lambda b, c: (row(b, c), C_Z // SSD_INNER)),
            pl.BlockSpec((L, SSD_CONV_DIM), lambda b, c: (row(b, c), C_XBC // SSD_CONV_DIM)),
            pl.BlockSpec((L, 128), lambda b, c: (row(b, c), C_SM // 128)),
            lay3((SSD_CONV, SSD_CONV_DIM)),
            lay3((1, SSD_CONV_DIM)),
            lay3((1, 128)),
            lay3((SSD_HEADS, 1)),
            lay3((1, 128)),
            lay3((SSD_HEADS, 1)),
            lay3((1, SSD_INNER)),
            lay3((1, SSD_INNER)),
        ],
        out_specs=pl.BlockSpec((L, SSD_INNER), lambda b, c: (row(b, c), 0)),
        out_shape=jax.ShapeDtypeStruct((t, SSD_INNER), BF16),
        scratch_shapes=[
            pltpu.VMEM((L + 8, SSD_CONV_DIM), F32),
            pltpu.VMEM((SSD_STATE, SSD_INNER), F32),
        ],
        compiler_params=_cparams(("parallel", "arbitrary")),
        name="ssd",
    )(proj, proj, proj, prm["ssd_conv_w"], prm["ssd_conv_b"], prm["ssd_dtb"], prm["ssd_dtb_t"],
      prm["ssd_alog"], prm["ssd_alog_t"], prm["ssd_dexp"], prm["ssd_norm"])


def _gla_body(q_ref, k_ref, v_ref, r_ref, sm_ref, wgk_ref, bgk_ref, gn_ref, seg_ref, o_ref,
              g_scr, st_scr, *, nch):
    L, C, H, K, V = GLA_CHUNK, GLA_SUB, GLA_HEADS, GLA_HEAD_K, GLA_HEAD_V
    nb = L // C
    T = nch * L

    @pl.when(pl.program_id(1) == 0)
    def _():
        st_scr[...] = jnp.zeros_like(st_scr)

    xg = _dot(sm_ref[...].astype(BF16), wgk_ref[...]) + bgk_ref[...]
    log_a = -_softplus(-xg) * (LOG2_E / GLA_GATE_TAU)
    rr = _iota((T, T), 0)
    cc = _iota((T, T), 1)
    tril = jnp.where(rr // L == cc // L, (rr >= cc).astype(F32), 0.0)
    g_all = _select_sum(tril, log_a)
    g_scr[...] = g_all
    qs_all = q_ref[...] * (K ** -0.5)

    head_rows_k = _iota((H * L, GLA_KD), 0) // L == _iota((H * L, GLA_KD), 1) // K
    head_rows_v = _iota((H * L, GLA_VD), 0) // L == _iota((H * L, GLA_VD), 1) // V
    rowblk = _iota((L, H * L), 0) // C
    colblk = (_iota((L, H * L), 1) % L) // C
    lrow = _iota((C, GLA_KD), 0)

    attn_off, rel_prod, q_dec, st_upd, g_last = [], [], [], [], []
    for ch in range(nch):
        base = ch * L
        g = g_all[base:base + L, :]
        qs = qs_all[base:base + L, :]
        k = k_ref[base:base + L, :]
        ends = [g_scr[base + C * j + C - 1:base + C * j + C, :] for j in range(nb)]
        end_b = jnp.concatenate([jnp.broadcast_to(e, (C, GLA_KD)) for e in ends], axis=0)
        kd = k * jnp.exp2(end_b - g)
        k_exp = jnp.where(head_rows_k, jnp.concatenate([kd] * H, axis=0), 0.0).astype(BF16)

        attn = jnp.zeros((L, H * L), F32)
        for j in range(nb - 1):
            qd = qs * jnp.exp2(jnp.minimum(g - ends[j], 0.0))
            aj = _dot_nt(qd.astype(BF16), k_exp)
            attn = jnp.where(colblk == j, jnp.where(rowblk > j, aj, 0.0), attn)

        pieces = []
        for j in range(C):
            blocks = []
            for i in range(nb):
                r0 = C * i
                gj = g_scr[base + r0 + j:base + r0 + j + 1, :]
                kj = k_ref[base + r0 + j:base + r0 + j + 1, :]
                rel = jnp.exp2(jnp.where(lrow >= j, g[r0:r0 + C, :] - gj, -jnp.inf))
                blocks.append(qs[r0:r0 + C, :] * kj * rel)
            pieces.append(jnp.concatenate(blocks, axis=0).astype(BF16))
        attn_off.append(attn)
        rel_prod.append(jnp.concatenate(pieces, axis=1))
        q_dec.append((qs * jnp.exp2(g)).astype(BF16))
        g_last.append(ends[nb - 1])
        k_dec = (k * jnp.exp2(ends[nb - 1] - g)).astype(BF16)
        st_upd.append(_dot_tn(v_ref[base:base + L, :].astype(BF16), k_dec))

    a_diag = _dot(jnp.concatenate(rel_prod, axis=0), seg_ref[...])
    o_intra = []
    for ch in range(nch):
        base = ch * L
        attn = jnp.where(colblk == rowblk, a_diag[base:base + L, :], attn_off[ch])
        v = v_ref[base:base + L, :]
        v_bd = jnp.where(head_rows_v, jnp.concatenate([v] * H, axis=0), 0.0).astype(BF16)
        o_intra.append(_dot(attn.astype(BF16), v_bd))

    same_head = _iota((GLA_VD, GLA_KD), 0) // V == _iota((GLA_VD, GLA_KD), 1) // K
    st = st_scr[...]
    for ch in range(nch):
        base = ch * L
        o = o_intra[ch] + _dot_nt(q_dec[ch], st.astype(BF16))
        st = st * jnp.exp2(g_last[ch]) + jnp.where(same_head, st_upd[ch], 0.0)
        normed = jnp.concatenate([_rms(o[:, h * V:(h + 1) * V], gn_ref[...]) for h in range(H)], axis=1)
        o_ref[base:base + L, :] = (normed * _silu(r_ref[base:base + L, :])).astype(BF16)
    st_scr[...] = st


def _gla(proj, prm, layer, batch, seq, nch=8):
    L = GLA_CHUNK * nch
    nc = seq // L
    t = proj.shape[0]
    row = lambda b, c: b * nc + c
    lay3 = lambda shape: pl.BlockSpec((None,) + shape, lambda b, c: (layer, 0, 0))
    seg = prm["gla_seg"]
    return pl.pallas_call(
        functools.partial(_gla_body, nch=nch),
        grid=(batch, nc),
        in_specs=[
            pl.BlockSpec((L, GLA_KD), lambda b, c: (row(b, c), C_GQ // GLA_KD)),
            pl.BlockSpec((L, GLA_KD), lambda b, c: (row(b, c), C_GK // GLA_KD)),
            pl.BlockSpec((L, GLA_VD), lambda b, c: (row(b, c), C_GV // GLA_VD)),
            pl.BlockSpec((L, GLA_VD), lambda b, c: (row(b, c), C_GR // GLA_VD)),
            pl.BlockSpec((L, 128), lambda b, c: (row(b, c), C_SM // 128)),
            lay3((128, GLA_KD)),
            lay3((1, GLA_KD)),
            lay3((1, GLA_HEAD_V)),
            pl.BlockSpec(seg.shape, lambda b, c: (0, 0)),
        ],
        out_specs=pl.BlockSpec((L, GLA_VD), lambda b, c: (row(b, c), 0)),
        out_shape=jax.ShapeDtypeStruct((t, GLA_VD), BF16),
        scratch_shapes=[
            pltpu.VMEM((L, GLA_KD), F32),
            pltpu.VMEM((GLA_VD, GLA_KD), F32),
        ],
        compiler_params=_cparams(("parallel", "arbitrary")),
        name="gla",
    )(proj, proj, proj, proj, proj, prm["gla_wgk"], prm["gla_bgk"], prm["gla_norm"], seg)


def _out_proj_body(x_ref, ya_ref, yb_ref, yc_ref, w_ref, *refs, tn):
    o_ref = refs[-2] if len(refs) == 3 else refs[0]
    y = jnp.concatenate([ya_ref[...], yb_ref[...], yc_ref[...]], axis=1)
    for j in range(o_ref.shape[1] // tn):
        cols = slice(j * tn, (j + 1) * tn)
        o_ref[:, cols] = x_ref[:, cols] + _dot(y, w_ref[:, cols])
    if len(refs) == 3:
        _reorder_w_in_body(refs[0], refs[2])


def _out_proj(x2d, y_mla, y_ssd, y_gla, w16, w_in_t, next_layer, tm=512, tn=1024):
    t, d = x2d.shape
    da, db, dc = y_mla.shape[1], y_ssd.shape[1], y_gla.shape[1]
    tm = min(tm, t)
    steps = t // tm
    in_specs = [
        pl.BlockSpec((tm, d), lambda i: (i, 0)),
        pl.BlockSpec((tm, da), lambda i: (i, 0)),
        pl.BlockSpec((tm, db), lambda i: (i, 0)),
        pl.BlockSpec((tm, dc), lambda i: (i, 0)),
        pl.BlockSpec((da + db + dc, d), lambda i: (0, 0), pipeline_mode=pl.Buffered(1)),
    ]
    out_specs = [pl.BlockSpec((tm, d), lambda i: (i, 0))]
    out_shape = [jax.ShapeDtypeStruct((t, d), F32)]
    args = [x2d, y_mla, y_ssd, y_gla, w16]
    if next_layer is not None:
        slab = d // steps
        assert slab * steps == d and slab % 128 == 0, (d, steps)
        in_specs.append(pl.BlockSpec((None, w_in_t.shape[1], slab), lambda i: (next_layer, 0, i)))
        out_specs.append(pl.BlockSpec((slab, P_WIDTH), lambda i: (i, 0)))
        out_shape.append(jax.ShapeDtypeStruct((d, P_WIDTH), BF16))
        args.append(w_in_t)
    outs = pl.pallas_call(
        functools.partial(_out_proj_body, tn=tn),
        grid=(steps,),
        in_specs=in_specs,
        out_specs=out_specs,
        out_shape=out_shape,
        compiler_params=_cparams(("arbitrary",)),
        name="out_proj",
    )(*args)
    return outs if next_layer is not None else (outs[0], None)


def _ffn_body(x_ref, xh_ref, g_ref, wg_ref, wu_ref, cw_ref, cb_ref, wd_ref, fg_ref, o_ref,
              h_scr, gate_scr, *, tm, tiles_per_seq, n_out_chunks, last_layer):
    i = pl.program_id(0)
    f = pl.program_id(1)

    @pl.when(f == 0)
    def _():
        x = x_ref[...]
        h_scr[0:HALO, :] = _rms(xh_ref[...], g_ref[...]).astype(BF16)
        h_scr[HALO:HALO + tm, :] = _rms(x, g_ref[...]).astype(BF16)
        o_ref[...] = x

    gate = _dot(h_scr[...], wg_ref[...])
    keep = (i % tiles_per_seq != 0).astype(F32)
    gate_scr[0:HALO, :] = gate[0:HALO, :] * keep
    gate_scr[HALO:HALO + tm, :] = gate[HALO:, :]
    cw = cw_ref[...]
    conv = cb_ref[...] + cw[FFN_CONV - 1:FFN_CONV] * gate[HALO:, :]
    for tap in range(1, FFN_CONV):
        conv = conv + cw[FFN_CONV - 1 - tap:FFN_CONV - tap] * gate_scr[HALO - tap:HALO - tap + tm, :]
    act = (_silu(conv) * _dot(h_scr[HALO:HALO + tm, :], wu_ref[...])).astype(BF16)
    dn = o_ref.shape[1] // n_out_chunks
    for c in range(n_out_chunks):
        o_ref[:, c * dn:(c + 1) * dn] += _dot(act, wd_ref[:, c * dn:(c + 1) * dn])

    if last_layer:
        @pl.when(f == pl.num_programs(1) - 1)
        def _():
            o_ref[...] = _rms(o_ref[...], fg_ref[...])


def _ffn(x2d, prm, wg16, wu16, wd16, final_gain, layer, seq, tm=512, tf=512):
    t, d = x2d.shape
    tm = min(tm, seq)
    tiles_per_seq = seq // tm
    halo_blocks = tm // HALO
    last_layer = layer == prm["ffn_norm"].shape[0] - 1
    lay = lambda shape, idx: pl.BlockSpec((None,) + shape, idx)
    return pl.pallas_call(
        functools.partial(_ffn_body, tm=tm, tiles_per_seq=tiles_per_seq, n_out_chunks=4,
                          last_layer=last_layer),
        grid=(t // tm, D_FF // tf),
        in_specs=[
            pl.BlockSpec((tm, d), lambda i, f: (i, 0)),
            pl.BlockSpec((HALO, d), lambda i, f: (jnp.maximum(i * halo_blocks - 1, 0), 0)),
            lay((1, d), lambda i, f: (layer, 0, 0)),
            pl.BlockSpec((d, tf), lambda i, f: (0, f)),
            pl.BlockSpec((d, tf), lambda i, f: (0, f)),
            lay((FFN_CONV, tf), lambda i, f: (layer, 0, f)),
            lay((1, tf), lambda i, f: (layer, 0, f)),
            pl.BlockSpec((tf, d), lambda i, f: (f, 0)),
            pl.BlockSpec((1, d), lambda i, f: (0, 0)),
        ],
        out_specs=pl.BlockSpec((tm, d), lambda i, f: (i, 0)),
        out_shape=jax.ShapeDtypeStruct((t, d), F32),
        scratch_shapes=[
            pltpu.VMEM((HALO + tm, d), BF16),
            pltpu.VMEM((HALO + tm, tf), F32),
        ],
        compiler_params=_cparams(("parallel", "arbitrary")),
        name="ffn",
    )(x2d, x2d, prm["ffn_norm"], wg16, wu16, prm["ffn_dw_w"], prm["ffn_dw_b"], wd16, final_gain)


_W_IN_PIECES = (
    ((1344, 1856),), ((1856, 2368),),
    ((832, 1344),), ((2888, 3400),), ((3416, 3928),), ((2376, 2632),), ((2632, 2888),),
    ((2368, 2376), (3400, 3416)),
    ((0, 512),), ((512, 768),),
    ((768, 832), (768, 832)),
)


def _reorder_w_in_body(wt_ref, o_ref):
    tk = wt_ref.shape[1]
    col = 0
    for ranges in _W_IN_PIECES:
        parts = [wt_ref[a:b, :] for a, b in ranges]
        rows = sum(b - a for a, b in ranges)
        if rows % 128:
            parts.append(jnp.zeros((-rows % 128, tk), F32))
            rows += -rows % 128
        piece = parts[0] if len(parts) == 1 else jnp.concatenate(parts, axis=0)
        o_ref[:, col:col + rows] = piece.T.astype(BF16)
        col += rows
    assert col == P_WIDTH


def _reorder_w_in(w_in_t, layer, tk=512):
    _, d_in, d = w_in_t.shape
    return pl.pallas_call(
        _reorder_w_in_body,
        grid=(d // tk,),
        in_specs=[pl.BlockSpec((None, d_in, tk), lambda i: (layer, 0, i))],
        out_specs=pl.BlockSpec((tk, P_WIDTH), lambda i: (i, 0)),
        out_shape=jax.ShapeDtypeStruct((d, P_WIDTH), BF16),
        compiler_params=_cparams(("parallel",)),
        name="reorder_w_in",
    )(w_in_t)


def _prepare(seq, attn_norm, mla_q_norm, mla_w_uq, mla_kv_norm, mla_w_ukv,
             ssd_conv_w, ssd_conv_b, ssd_dt_bias, ssd_a_log, ssd_d, ssd_norm,
             gla_w_gk, gla_b_gk, gla_norm, ffn_norm, ffn_dw_w, ffn_dw_b):
    depth = mla_w_uq.shape[0]
    wq = mla_w_uq.reshape(depth, MLA_Q_RANK, MLA_HEADS, MLA_NOPE + MLA_ROPE)
    wq = jnp.concatenate([wq, wq[..., MLA_NOPE:]], axis=-1)
    wq = wq.reshape(depth, MLA_Q_RANK, MLA_HEADS * MLA_QK).astype(BF16)
    wkv = mla_w_ukv.astype(BF16)

    inv_freq = ROPE_BASE ** (-jnp.arange(0, MLA_ROPE, 2, dtype=F32) / MLA_ROPE)
    ang = jnp.arange(seq, dtype=F32)[:, None] * inv_freq[None, :]
    cos, sin = jnp.cos(ang), jnp.sin(ang)
    zeros = jnp.zeros((seq, MLA_ROPE), F32)
    cc = jnp.concatenate([cos, cos, zeros], axis=1)
    ss = jnp.concatenate([-sin, sin, zeros], axis=1)

    pad_lanes = lambda a: jnp.pad(a, ((0, 0), (0, 128 - a.shape[1])))[:, None, :]
    wgk = jnp.zeros((depth, 128, GLA_KD), F32).at[:, SM_GLOW:SM_GLOW + GLA_GATE_RANK, :].set(gla_w_gk)

    rows = jnp.arange(GLA_SUB * GLA_KD)
    cols = jnp.arange(GLA_HEADS * GLA_CHUNK)
    seg = ((rows[:, None] % GLA_KD) // GLA_HEAD_K == cols[None, :] // GLA_CHUNK) & (
        rows[:, None] // GLA_KD == cols[None, :] % GLA_SUB)

    return dict(
        attn_norm=attn_norm[:, None, :],
        mla_q_norm=mla_q_norm[:, None, :], mla_kv_norm=mla_kv_norm[:, None, :], wq=wq, wkv=wkv,
        cc=cc, ss=ss,
        ssd_conv_w=ssd_conv_w, ssd_conv_b=ssd_conv_b[:, None, :],
        ssd_dtb=pad_lanes(ssd_dt_bias), ssd_dtb_t=ssd_dt_bias[:, :, None],
        ssd_alog=pad_lanes(ssd_a_log), ssd_alog_t=ssd_a_log[:, :, None],
        ssd_dexp=jnp.repeat(ssd_d, SSD_HEAD_DIM, axis=1)[:, None, :], ssd_norm=ssd_norm[:, None, :],
        gla_wgk=wgk.astype(BF16), gla_bgk=gla_b_gk[:, None, :], gla_norm=gla_norm[:, None, :],
        gla_seg=seg.astype(BF16),
        ffn_norm=ffn_norm[:, None, :], ffn_dw_w=ffn_dw_w, ffn_dw_b=ffn_dw_b[:, None, :],
    )


def kernel(x, attn_norm, w_in, mla_q_norm, mla_w_uq, mla_kv_norm, mla_w_ukv, ssd_conv_w, ssd_conv_b,
           ssd_dt_bias, ssd_a_log, ssd_d, ssd_norm, gla_w_gk, gla_b_gk, gla_norm, w_out, ffn_norm,
           ffn_w_gate, ffn_w_up, ffn_dw_w, ffn_dw_b, ffn_w_down, final_norm):
    batch, seq, d = x.shape
    prm = _prepare(seq, attn_norm, mla_q_norm, mla_w_uq, mla_kv_norm, mla_w_ukv,
                   ssd_conv_w, ssd_conv_b, ssd_dt_bias, ssd_a_log, ssd_d, ssd_norm,
                   gla_w_gk, gla_b_gk, gla_norm, ffn_norm, ffn_dw_w, ffn_dw_b)
    depth = w_in.shape[0]
    w_in_t = jnp.swapaxes(w_in, 1, 2)
    w_in16 = _reorder_w_in(w_in_t, 0)
    x2d = x.reshape(batch * seq, d)
    for layer in range(depth):
        proj, qt, k, vt, wg16, wu16 = _in_proj(x2d, prm, w_in16, layer, batch, seq,
                                               (ffn_w_gate, ffn_w_up))
        y_mla, wd16, wo16 = _mla_attn(qt, k, vt, layer, (ffn_w_down, w_out))
        y_mla = y_mla.reshape(batch * seq, MLA_HEADS * MLA_V)
        y_ssd = _ssd(proj, prm, layer, batch, seq)
        y_gla = _gla(proj, prm, layer, batch, seq)
        x2d, w_in16 = _out_proj(x2d, y_mla, y_ssd, y_gla, wo16, w_in_t,
                                layer + 1 if layer + 1 < depth else None)
        x2d = _ffn(x2d, prm, wg16, wu16, wd16, final_norm[None, :], layer, seq)
    return x2d.reshape(batch, seq, d)
```

```python
import functools

import jax
import jax.numpy as jnp
from jax import lax
from jax.experimental import pallas as pl
from jax.experimental.pallas import tpu as pltpu

F32 = jnp.float32
BF16 = jnp.bfloat16

EPS = 1e-6
MLA_HEADS = 8
MLA_Q_RANK = 512
MLA_KV_RANK = 256
MLA_NOPE = 128
MLA_ROPE = 64
MLA_V = 128
MLA_QK = 256
MLA_VE = MLA_V + 16
LOG2_E = 1.4426950408889634
ROPE_BASE = 10000.0
SSD_HEADS = 8
SSD_HEAD_DIM = 64
SSD_STATE = 128
SSD_CONV = 4
SSD_CHUNK = 128
SSD_INNER = 512
SSD_CONV_DIM = 1024
GLA_HEADS = 4
GLA_HEAD_K = 64
GLA_HEAD_V = 128
GLA_GATE_RANK = 16
GLA_GATE_TAU = 16.0
GLA_CHUNK = 64
GLA_SUB = 16
GLA_KD = 256
GLA_VD = 512
D_FF = 5632
FFN_CONV = 3
HALO = 16

P_WIDTH = 4096
P_OUT = 3200
C_XBC, C_Z, C_GV, C_GR, C_GQ, C_GK, C_SM, C_CQ, C_CKV, C_KPE = (
    0, 1024, 1536, 2048, 2560, 2816, 3072, 3200, 3712, 3968)
SM_GLOW = 8

VMEM_LIMIT = 52 * 1024 * 1024


def _cparams(sem):
    return pltpu.CompilerParams(dimension_semantics=sem, vmem_limit_bytes=VMEM_LIMIT)


def _rms(x, g):
    var = jnp.mean(x * x, axis=-1, keepdims=True)
    return x * lax.rsqrt(var + EPS) * g


def _softplus(x):
    return jnp.maximum(x, 0.0) + jnp.log(1.0 + jnp.exp(-jnp.abs(x)))


def _silu(x):
    return x * jax.nn.sigmoid(x)


def _dot(a, b):
    return jnp.dot(a, b, preferred_element_type=F32)


def _dot_nt(a, b):
    return lax.dot_general(a, b, (((1,), (1,)), ((), ())), preferred_element_type=F32)


def _dot_tn(a, b):
    return lax.dot_general(a, b, (((0,), (0,)), ((), ())), preferred_element_type=F32)


def _iota(shape, axis):
    return lax.broadcasted_iota(jnp.int32, shape, axis)


def _split3(x):
    hi = x.astype(BF16)
    r = x - hi.astype(F32)
    mid = r.astype(BF16)
    lo = (r - mid.astype(F32)).astype(BF16)
    return hi, mid, lo


def _select_sum(mask, x):
    m = mask.astype(BF16)
    hi, mid, lo = _split3(x)
    return (_dot(m, lo) + _dot(m, mid)) + _dot(m, hi)


def _select_sum_t(x, mask):
    m = mask.astype(BF16)
    hi, mid, lo = _split3(x)
    return (_dot(lo, m) + _dot(mid, m)) + _dot(hi, m)


def _cast_specs(weights, layer, steps, step_index):
    in_specs, out_specs, out_shapes = [], [], []
    for w in weights:
        _, rows, cols = w.shape
        slab = rows // steps
        assert slab * steps == rows and slab % 16 == 0, (rows, steps)
        in_specs.append(pl.BlockSpec((None, slab, cols), lambda *g: (layer, step_index(*g), 0)))
        out_specs.append(pl.BlockSpec((slab, cols), lambda *g: (step_index(*g), 0)))
        out_shapes.append(jax.ShapeDtypeStruct((rows, cols), BF16))
    return in_specs, out_specs, out_shapes


def _cast_slabs(src_refs, dst_refs):
    for src, dst in zip(src_refs, dst_refs):
        dst[...] = src[...].astype(BF16)


def _rope(blk, cc, ss):
    return blk * cc + pltpu.roll(blk, MLA_ROPE // 2, axis=1) * ss


def _in_proj_body(x_ref, g_ref, w_ref, qn_ref, kvn_ref, wq_ref, wkv_ref, cc_ref, ss_ref, *refs, tn):
    n_cast = (len(refs) - 4) // 2
    o_ref, qt_out, k_out, vt_out = refs[n_cast:n_cast + 4]
    tm = x_ref.shape[0]
    h = _rms(x_ref[...], g_ref[...]).astype(BF16)
    mla = _dot(h, w_ref[:, P_OUT:])
    cq = mla[:, C_CQ - P_OUT:C_CKV - P_OUT]
    ckv = mla[:, C_CKV - P_OUT:C_KPE - P_OUT]
    cc = cc_ref[...]
    ss = ss_ref[...]
    kpe = _rope(mla[:, C_KPE - P_OUT:], cc, ss).astype(BF16)
    qf = _dot(_rms(cq, qn_ref[...]).astype(BF16), wq_ref[...])
    kvf = _dot(_rms(ckv, kvn_ref[...]).astype(BF16), wkv_ref[...])
    scale = (MLA_NOPE + MLA_ROPE) ** -0.5 * LOG2_E
    ones = jnp.ones((MLA_VE - MLA_V, tm), BF16)

    def head_outputs(hd):
        base = hd * MLA_QK
        q_pe = _rope(qf[:, base + MLA_NOPE:base + MLA_QK], cc, ss)
        q_h = jnp.concatenate([qf[:, base:base + MLA_NOPE], q_pe], axis=1) * scale
        qt_out[hd] = q_h.T.astype(BF16)
        k_out[hd] = jnp.concatenate([kvf[:, base:base + MLA_NOPE].astype(BF16), kpe], axis=1)
        vt_out[hd, 0:MLA_V, :] = kvf[:, base + MLA_NOPE:base + MLA_QK].T.astype(BF16)
        vt_out[hd, MLA_V:MLA_VE, :] = ones

    starts = list(range(0, P_OUT, tn))
    heads_per_chunk = -(-MLA_HEADS // len(starts))
    for idx, c0 in enumerate(starts):
        cols = slice(c0, min(c0 + tn, P_OUT))
        o_ref[:, cols] = _dot(h, w_ref[:, cols])
        for hd in range(idx * heads_per_chunk, min((idx + 1) * heads_per_chunk, MLA_HEADS)):
            head_outputs(hd)
    _cast_slabs(refs[:n_cast], refs[n_cast + 4:])


def _in_proj(x2d, prm, w_in16, layer, batch, seq, cast_weights, tm=256, tn=1024):
    t, d = x2d.shape
    tm = min(tm, seq)
    steps = t // tm
    nsb = seq // tm
    H = MLA_HEADS
    resident = lambda shape: pl.BlockSpec((None,) + shape, lambda i: (layer, 0, 0),
                                          pipeline_mode=pl.Buffered(1))
    c_in, c_out, c_shapes = _cast_specs(cast_weights, layer, steps, lambda i: i)
    return pl.pallas_call(
        functools.partial(_in_proj_body, tn=tn),
        grid=(steps,),
        in_specs=[
            pl.BlockSpec((tm, d), lambda i: (i, 0)),
            pl.BlockSpec((None, 1, d), lambda i: (layer, 0, 0)),
            pl.BlockSpec((d, P_WIDTH), lambda i: (0, 0), pipeline_mode=pl.Buffered(1)),
            pl.BlockSpec((None, 1, MLA_Q_RANK), lambda i: (layer, 0, 0)),
            pl.BlockSpec((None, 1, MLA_KV_RANK), lambda i: (layer, 0, 0)),
            resident((MLA_Q_RANK, H * MLA_QK)),
            resident((MLA_KV_RANK, H * MLA_QK)),
            pl.BlockSpec((tm, 128), lambda i: (i % nsb, 0)),
            pl.BlockSpec((tm, 128), lambda i: (i % nsb, 0)),
        ] + c_in,
        out_specs=[
            pl.BlockSpec((tm, P_OUT), lambda i: (i, 0)),
            pl.BlockSpec((None, H, MLA_QK, tm), lambda i: (i // nsb, 0, 0, i % nsb)),
            pl.BlockSpec((None, H, tm, MLA_QK), lambda i: (i // nsb, 0, i % nsb, 0)),
            pl.BlockSpec((None, H, None, MLA_VE, tm), lambda i: (i // nsb, 0, i % nsb, 0, 0)),
        ] + c_out,
        out_shape=[
            jax.ShapeDtypeStruct((t, P_OUT), F32),
            jax.ShapeDtypeStruct((batch, H, MLA_QK, seq), BF16),
            jax.ShapeDtypeStruct((batch, H, seq, MLA_QK), BF16),
            jax.ShapeDtypeStruct((batch, H, nsb, MLA_VE, tm), BF16),
        ] + c_shapes,
        compiler_params=_cparams(("arbitrary",)),
        name="in_proj",
    )(x2d, prm["attn_norm"], w_in16, prm["mla_q_norm"], prm["mla_kv_norm"], prm["wq"],
      prm["wkv"], prm["cc"], prm["ss"], *cast_weights)


def _attn_body(qt_ref, k_ref, vt_ref, *refs, tq, hp):
    n_cast = (len(refs) - 2) // 2
    o_ref, acc_scr = refs[n_cast], refs[-1]
    _cast_slabs(refs[:n_cast], refs[n_cast + 1:-1])
    qi = pl.program_id(2)
    qts = [qt_ref[h] for h in range(hp)]

    def scores(kb, h):
        start = pl.multiple_of(kb * tq, tq)
        return _dot(k_ref[h, pl.ds(start, tq), :], qts[h])

    def update(kb, h, s, m, diagonal):
        if diagonal:
            s = jnp.where(_iota((tq, tq), 0) <= _iota((tq, tq), 1), s, -jnp.inf)
        m_new = jnp.maximum(m, jnp.max(s, axis=0, keepdims=True))
        p = jnp.exp2(s - m_new).astype(BF16)
        acc_scr[h] = jnp.exp2(m - m_new) * acc_scr[h] + _dot(vt_ref[h, kb], p)
        return m_new

    def step(kb, ms, diagonal):
        ss = [scores(kb, h) for h in range(hp)]
        return tuple(update(kb, h, ss[h], ms[h], diagonal) for h in range(hp))

    def pair_step(kp, ms):
        ka, kb = 2 * kp, 2 * kp + 1
        sa = [scores(ka, h) for h in range(hp)]
        sb, ma = [], []
        for h in range(hp):
            ma.append(update(ka, h, sa[h], ms[h], False))
            sb.append(scores(kb, h))
        return tuple(update(kb, h, sb[h], ma[h], False) for h in range(hp))

    acc_scr[...] = jnp.zeros_like(acc_scr)
    init = tuple(jnp.full((1, tq), -jnp.inf, F32) for _ in range(hp))
    ms = lax.fori_loop(0, qi // 2, pair_step, init)
    ms = lax.cond(qi % 2 == 1, lambda c: step(qi - 1, c, False), lambda c: c, ms)
    step(qi, ms, True)
    outs = []
    for h in range(hp):
        acc = acc_scr[h]
        outs.append((acc[0:MLA_V, :] / acc[MLA_V:MLA_V + 1, :]).T)
    o_ref[...] = jnp.concatenate(outs, axis=1).astype(BF16)


def _mla_attn(qt, k, vt, layer, cast_weights, hp=8):
    batch, heads, _, seq = qt.shape
    tq = vt.shape[-1]
    nh, nq = heads // hp, seq // tq
    c_in, c_out, c_shapes = _cast_specs(cast_weights, layer, batch * nh * nq,
                                        lambda b, h, i: (b * nh + h) * nq + i)
    return pl.pallas_call(
        functools.partial(_attn_body, tq=tq, hp=hp),
        grid=(batch, nh, nq),
        in_specs=[
            pl.BlockSpec((None, hp, MLA_QK, tq), lambda b, h, i: (b, h, 0, i)),
            pl.BlockSpec((None, hp, seq, MLA_QK), lambda b, h, i: (b, h, 0, 0)),
            pl.BlockSpec((None, hp, seq // tq, MLA_VE, tq), lambda b, h, i: (b, h, 0, 0, 0)),
        ] + c_in,
        out_specs=[pl.BlockSpec((None, tq, hp * MLA_V), lambda b, h, i: (b, i, h))] + c_out,
        out_shape=[jax.ShapeDtypeStruct((batch, seq, heads * MLA_V), BF16)] + c_shapes,
        scratch_shapes=[pltpu.VMEM((hp, MLA_VE, tq), F32)],
        compiler_params=_cparams(("arbitrary", "arbitrary", "arbitrary")),
        name="mla_attn",
    )(qt, k, vt, *cast_weights)


def _ssd_body(z_ref, xbc_ref, sm_ref, cw_ref, cb_ref, dtb_ref, dtbt_ref, alog_ref, alogt_ref,
              dexp_ref, ng_ref, o_ref, xb_scr, st_scr, *, nch):
    L, N, P = SSD_CHUNK, SSD_STATE, SSD_HEAD_DIM
    T = nch * L
    pad = 8

    @pl.when(pl.program_id(1) == 0)
    def _():
        xb_scr[0:pad, :] = jnp.zeros((pad, SSD_CONV_DIM), F32)
        st_scr[...] = jnp.zeros_like(st_scr)

    xb_scr[pad:pad + T, :] = xbc_ref[...]
    cw = cw_ref[...]
    conv = cb_ref[...] + cw[3:4] * xb_scr[pad:pad + T, :]
    for tap in range(1, SSD_CONV):
        conv = conv + cw[3 - tap:4 - tap] * xb_scr[pad - tap:pad - tap + T, :]
    xb_scr[0:pad, :] = xb_scr[T:T + pad, :]
    act = _silu(conv)
    xs = act[:, :SSD_INNER]
    bm = act[:, SSD_INNER:SSD_INNER + 2 * N]
    cm = act[:, SSD_INNER + 2 * N:]

    sm = sm_ref[...]
    lane = _iota((1, 128), 1)
    nega = jnp.where(lane < SSD_HEADS, -LOG2_E * jnp.exp(alog_ref[...]), 0.0)
    dt = _softplus(sm + dtb_ref[...])
    a = dt * nega
    smt = sm.T[0:SSD_HEADS, :]
    dtt = _softplus(smt + dtbt_ref[...])
    at = dtt * (-LOG2_E * jnp.exp(alogt_ref[...]))
    row = _iota((T, T), 0)
    col = _iota((T, T), 1)
    same_chunk = row // L == col // L
    acum = _select_sum(jnp.where(same_chunk, (row >= col).astype(F32), 0.0), a)
    acumt = _select_sum_t(at, jnp.where(same_chunk, (row <= col).astype(F32), 0.0))

    colb = [jnp.broadcast_to(acum[:, h:h + 1], (T, L)) for h in range(SSD_HEADS)]
    dtb = [jnp.broadcast_to(dt[:, h:h + 1], (T, L)) for h in range(SSD_HEADS)]
    left_t = _iota((T, 2 * P), 1) < P
    left = _iota((L, 2 * P), 1) < P
    causal = _iota((L, L), 0) >= _iota((L, L), 1)

    def per_head_lanes(vals):
        return jnp.concatenate([jnp.where(left_t, vals[2 * p], vals[2 * p + 1])
                                for p in range(SSD_HEADS // 2)], axis=1)

    acum_e = per_head_lanes(colb)
    x_dt = xs * per_head_lanes(dtb)
    x_dt16 = x_dt.astype(BF16)

    ht = st_scr[...]
    heads_per_group = SSD_HEADS // 2
    width = heads_per_group * P
    y_chunks = []
    for c in range(nch):
        rows = slice(c * L, (c + 1) * L)
        alast_e = acum_e[(c + 1) * L - 1:(c + 1) * L, :]
        xd = (x_dt[rows, :] * jnp.exp2(alast_e - acum_e[rows, :])).astype(BF16)
        ht16 = ht.astype(BF16)
        g_mats, c_mats, st_new = [], [], []
        for g in range(2):
            b_g = bm[rows, g * N:(g + 1) * N]
            c_g = cm[rows, g * N:(g + 1) * N]
            c_mats.append(c_g)
            g_mats.append(_dot_nt(c_g.astype(BF16), b_g.astype(BF16)))
            st_new.append(_dot(b_g.T.astype(BF16), xd[:, g * width:(g + 1) * width]))
        ys = []
        for p in range(SSD_HEADS // 2):
            g = (2 * p) // heads_per_group
            rhs = jnp.concatenate([x_dt16[rows, 2 * p * P:(2 * p + 2) * P],
                                   ht16[:, 2 * p * P:(2 * p + 2) * P]], axis=0)
            pair = []
            for h in (2 * p, 2 * p + 1):
                colb_h = colb[h][rows, :]
                rowb = jnp.broadcast_to(acumt[h:h + 1, rows], (L, L))
                decay = jnp.exp2(jnp.where(causal, colb_h - rowb, -jnp.inf))
                lhs = jnp.concatenate([g_mats[g] * decay, c_mats[g] * jnp.exp2(colb_h)], axis=1)
                pair.append(_dot(lhs.astype(BF16), rhs))
            ys.append(jnp.where(left, pair[0], pair[1]))
        y_chunks.append(jnp.concatenate(ys, axis=1))
        ht = ht * jnp.exp2(alast_e) + jnp.concatenate(st_new, axis=1)
    st_scr[...] = ht
    y = jnp.concatenate(y_chunks, axis=0) + xs * dexp_ref[...]
    o_ref[...] = _rms(y * _silu(z_ref[...]), ng_ref[...]).astype(BF16)


def _ssd(proj, prm, layer, batch, seq, nch=4):
    L = SSD_CHUNK * nch
    nc = seq // L
    t = proj.shape[0]
    row = lambda b, c: b * nc + c
    lay3 = lambda shape: pl.BlockSpec((None,) + shape, lambda b, c: (layer, 0, 0))
    return pl.pallas_call(
        functools.partial(_ssd_body, nch=nch),
        grid=(batch, nc),
        in_specs=[
            pl.BlockSpec((L, SSD_INNER), lambda b, c: (row(b, c), C_Z // SSD_INNER)),
            pl.BlockSpec((L, SSD_CONV_DIM), lambda b, c: (row(b, c), C_XBC // SSD_CONV_DIM)),
            pl.BlockSpec((L, 128), lambda b, c: (row(b, c), C_SM // 128)),
            lay3((SSD_CONV, SSD_CONV_DIM)),
            lay3((1, SSD_CONV_DIM)),
            lay3((1, 128)),
            lay3((SSD_HEADS, 1)),
            lay3((1, 128)),
            lay3((SSD_HEADS, 1)),
            lay3((1, SSD_INNER)),
            lay3((1, SSD_INNER)),
        ],
        out_specs=pl.BlockSpec((L, SSD_INNER), lambda b, c: (row(b, c), 0)),
        out_shape=jax.ShapeDtypeStruct((t, SSD_INNER), BF16),
        scratch_shapes=[
            pltpu.VMEM((L + 8, SSD_CONV_DIM), F32),
            pltpu.VMEM((SSD_STATE, SSD_INNER), F32),
        ],
        compiler_params=_cparams(("parallel", "arbitrary")),
        name="ssd",
    )(proj, proj, proj, prm["ssd_conv_w"], prm["ssd_conv_b"], prm["ssd_dtb"], prm["ssd_dtb_t"],
      prm["ssd_alog"], prm["ssd_alog_t"], prm["ssd_dexp"], prm["ssd_norm"])


def _gla_body(q_ref, k_ref, v_ref, r_ref, sm_ref, wgk_ref, bgk_ref, gn_ref, seg_ref, o_ref,
              g_scr, st_scr, *, nch):
    L, C, H, K, V = GLA_CHUNK, GLA_SUB, GLA_HEADS, GLA_HEAD_K, GLA_HEAD_V
    nb = L // C
    T = nch * L

    @pl.when(pl.program_id(1) == 0)
    def _():
        st_scr[...] = jnp.zeros_like(st_scr)

    xg = _dot(sm_ref[...].astype(BF16), wgk_ref[...]) + bgk_ref[...]
    log_a = -_softplus(-xg) * (LOG2_E / GLA_GATE_TAU)
    rr = _iota((T, T), 0)
    cc = _iota((T, T), 1)
    tril = jnp.where(rr // L == cc // L, (rr >= cc).astype(F32), 0.0)
    g_all = _select_sum(tril, log_a)
    g_scr[...] = g_all
    qs_all = q_ref[...] * (K ** -0.5)

    head_rows_k = _iota((H * L, GLA_KD), 0) // L == _iota((H * L, GLA_KD), 1) // K
    head_rows_v = _iota((H * L, GLA_VD), 0) // L == _iota((H * L, GLA_VD), 1) // V
    rowblk = _iota((L, H * L), 0) // C
    colblk = (_iota((L, H * L), 1) % L) // C
    lrow = _iota((C, GLA_KD), 0)

    attn_off, rel_prod, q_dec, st_upd, g_last = [], [], [], [], []
    for ch in range(nch):
        base = ch * L
        g = g_all[base:base + L, :]
        qs = qs_all[base:base + L, :]
        k = k_ref[base:base + L, :]
        ends = [g_scr[base + C * j + C - 1:base + C * j + C, :] for j in range(nb)]
        end_b = jnp.concatenate([jnp.broadcast_to(e, (C, GLA_KD)) for e in ends], axis=0)
        kd = k * jnp.exp2(end_b - g)
        k_exp = jnp.where(head_rows_k, jnp.concatenate([kd] * H, axis=0), 0.0).astype(BF16)

        attn = jnp.zeros((L, H * L), F32)
        for j in range(nb - 1):
            qd = qs * jnp.exp2(jnp.minimum(g - ends[j], 0.0))
            aj = _dot_nt(qd.astype(BF16), k_exp)
            attn = jnp.where(colblk == j, jnp.where(rowblk > j, aj, 0.0), attn)

        pieces = []
        for j in range(C):
            blocks = []
            for i in range(nb):
                r0 = C * i
                gj = g_scr[base + r0 + j:base + r0 + j + 1, :]
                kj = k_ref[base + r0 + j:base + r0 + j + 1, :]
                rel = jnp.exp2(jnp.where(lrow >= j, g[r0:r0 + C, :] - gj, -jnp.inf))
                blocks.append(qs[r0:r0 + C, :] * kj * rel)
            pieces.append(jnp.concatenate(blocks, axis=0).astype(BF16))
        attn_off.append(attn)
        rel_prod.append(jnp.concatenate(pieces, axis=1))
        q_dec.append((qs * jnp.exp2(g)).astype(BF16))
        g_last.append(ends[nb - 1])
        k_dec = (k * jnp.exp2(ends[nb - 1] - g)).astype(BF16)
        st_upd.append(_dot_tn(v_ref[base:base + L, :].astype(BF16), k_dec))

    a_diag = _dot(jnp.concatenate(rel_prod, axis=0), seg_ref[...])
    o_intra = []
    for ch in range(nch):
        base = ch * L
        attn = jnp.where(colblk == rowblk, a_diag[base:base + L, :], attn_off[ch])
        v = v_ref[base:base + L, :]
        v_bd = jnp.where(head_rows_v, jnp.concatenate([v] * H, axis=0), 0.0).astype(BF16)
        o_intra.append(_dot(attn.astype(BF16), v_bd))

    same_head = _iota((GLA_VD, GLA_KD), 0) // V == _iota((GLA_VD, GLA_KD), 1) // K
    st = st_scr[...]
    for ch in range(nch):
        base = ch * L
        o = o_intra[ch] + _dot_nt(q_dec[ch], st.astype(BF16))
        st = st * jnp.exp2(g_last[ch]) + jnp.where(same_head, st_upd[ch], 0.0)
        normed = jnp.concatenate([_rms(o[:, h * V:(h + 1) * V], gn_ref[...]) for h in range(H)], axis=1)
        o_ref[base:base + L, :] = (normed * _silu(r_ref[base:base + L, :])).astype(BF16)
    st_scr[...] = st


def _gla(proj, prm, layer, batch, seq, nch=8):
    L = GLA_CHUNK * nch
    nc = seq // L
    t = proj.shape[0]
    row = lambda b, c: b * nc + c
    lay3 = lambda shape: pl.BlockSpec((None,) + shape, lambda b, c: (layer, 0, 0))
    seg = prm["gla_seg"]
    return pl.pallas_call(
        functools.partial(_gla_body, nch=nch),
        grid=(batch, nc),
        in_specs=[
            pl.BlockSpec((L, GLA_KD), lambda b, c: (row(b, c), C_GQ // GLA_KD)),
            pl.BlockSpec((L, GLA_KD), lambda b, c: (row(b, c), C_GK // GLA_KD)),
            pl.BlockSpec((L, GLA_VD), lambda b, c: (row(b, c), C_GV // GLA_VD)),
            pl.BlockSpec((L, GLA_VD), lambda b, c: (row(b, c), C_GR // GLA_VD)),
            pl.BlockSpec((L, 128), lambda b, c: (row(b, c), C_SM // 128)),
            lay3((128, GLA_KD)),
            lay3((1, GLA_KD)),
            lay3((1, GLA_HEAD_V)),
            pl.BlockSpec(seg.shape, lambda b, c: (0, 0)),
        ],
        out_specs=pl.BlockSpec((L, GLA_VD), lambda b, c: (row(b, c), 0)),
        out_shape=jax.ShapeDtypeStruct((t, GLA_VD), BF16),
        scratch_shapes=[
            pltpu.VMEM((L, GLA_KD), F32),
            pltpu.VMEM((GLA_VD, GLA_KD), F32),
        ],
        compiler_params=_cparams(("parallel", "arbitrary")),
        name="gla",
    )(proj, proj, proj, proj, proj, prm["gla_wgk"], prm["gla_bgk"], prm["gla_norm"], seg)


def _out_proj_body(x_ref, ya_ref, yb_ref, yc_ref, w_ref, *refs, tn):
    o_ref = refs[-2] if len(refs) == 3 else refs[0]
    y = jnp.concatenate([ya_ref[...], yb_ref[...], yc_ref[...]], axis=1)
    for j in range(o_ref.shape[1] // tn):
        cols = slice(j * tn, (j + 1) * tn)
        o_ref[:, cols] = x_ref[:, cols] + _dot(y, w_ref[:, cols])
    if len(refs) == 3:
        _reorder_w_in_body(refs[0], refs[2])


def _out_proj(x2d, y_mla, y_ssd, y_gla, w16, w_in_t, next_layer, tm=512, tn=1024):
    t, d = x2d.shape
    da, db, dc = y_mla.shape[1], y_ssd.shape[1], y_gla.shape[1]
    tm = min(tm, t)
    steps = t // tm
    in_specs = [
        pl.BlockSpec((tm, d), lambda i: (i, 0)),
        pl.BlockSpec((tm, da), lambda i: (i, 0)),
        pl.BlockSpec((tm, db), lambda i: (i, 0)),
        pl.BlockSpec((tm, dc), lambda i: (i, 0)),
        pl.BlockSpec((da + db + dc, d), lambda i: (0, 0), pipeline_mode=pl.Buffered(1)),
    ]
    out_specs = [pl.BlockSpec((tm, d), lambda i: (i, 0))]
    out_shape = [jax.ShapeDtypeStruct((t, d), F32)]
    args = [x2d, y_mla, y_ssd, y_gla, w16]
    if next_layer is not None:
        slab = d // steps
        assert slab * steps == d and slab % 128 == 0, (d, steps)
        in_specs.append(pl.BlockSpec((None, w_in_t.shape[1], slab), lambda i: (next_layer, 0, i)))
        out_specs.append(pl.BlockSpec((slab, P_WIDTH), lambda i: (i, 0)))
        out_shape.append(jax.ShapeDtypeStruct((d, P_WIDTH), BF16))
        args.append(w_in_t)
    outs = pl.pallas_call(
        functools.partial(_out_proj_body, tn=tn),
        grid=(steps,),
        in_specs=in_specs,
        out_specs=out_specs,
        out_shape=out_shape,
        compiler_params=_cparams(("arbitrary",)),
        name="out_proj",
    )(*args)
    return outs if next_layer is not None else (outs[0], None)


def _ffn_body(x_ref, xh_ref, g_ref, wg_ref, wu_ref, cw_ref, cb_ref, wd_ref, fg_ref, o_ref,
              h_scr, gate_scr, *, tm, tiles_per_seq, n_out_chunks, last_layer):
    i = pl.program_id(0)
    f = pl.program_id(1)

    @pl.when(f == 0)
    def _():
        x = x_ref[...]
        h_scr[0:HALO, :] = _rms(xh_ref[...], g_ref[...]).astype(BF16)
        h_scr[HALO:HALO + tm, :] = _rms(x, g_ref[...]).astype(BF16)
        o_ref[...] = x

    gate = _dot(h_scr[...], wg_ref[...])
    keep = (i % tiles_per_seq != 0).astype(F32)
    gate_scr[0:HALO, :] = gate[0:HALO, :] * keep
    gate_scr[HALO:HALO + tm, :] = gate[HALO:, :]
    cw = cw_ref[...]
    conv = cb_ref[...] + cw[FFN_CONV - 1:FFN_CONV] * gate[HALO:, :]
    for tap in range(1, FFN_CONV):
        conv = conv + cw[FFN_CONV - 1 - tap:FFN_CONV - tap] * gate_scr[HALO - tap:HALO - tap + tm, :]
    act = (_silu(conv) * _dot(h_scr[HALO:HALO + tm, :], wu_ref[...])).astype(BF16)
    dn = o_ref.shape[1] // n_out_chunks
    for c in range(n_out_chunks):
        o_ref[:, c * dn:(c + 1) * dn] += _dot(act, wd_ref[:, c * dn:(c + 1) * dn])

    if last_layer:
        @pl.when(f == pl.num_programs(1) - 1)
        def _():
            o_ref[...] = _rms(o_ref[...], fg_ref[...])


def _ffn(x2d, prm, wg16, wu16, wd16, final_gain, layer, seq, tm=512, tf=512):
    t, d = x2d.shape
    tm = min(tm, seq)
    tiles_per_seq = seq // tm
    halo_blocks = tm // HALO
    last_layer = layer == prm["ffn_norm"].shape[0] - 1
    lay = lambda shape, idx: pl.BlockSpec((None,) + shape, idx)
    return pl.pallas_call(
        functools.partial(_ffn_body, tm=tm, tiles_per_seq=tiles_per_seq, n_out_chunks=4,
                          last_layer=last_layer),
        grid=(t // tm, D_FF // tf),
        in_specs=[
            pl.BlockSpec((tm, d), lambda i, f: (i, 0)),
            pl.BlockSpec((HALO, d), lambda i, f: (jnp.maximum(i * halo_blocks - 1, 0), 0)),
            lay((1, d), lambda i, f: (layer, 0, 0)),
            pl.BlockSpec((d, tf), lambda i, f: (0, f)),
            pl.BlockSpec((d, tf), lambda i, f: (0, f)),
            lay((FFN_CONV, tf), lambda i, f: (layer, 0, f)),
            lay((1, tf), lambda i, f: (layer, 0, f)),
            pl.BlockSpec((tf, d), lambda i, f: (f, 0)),
            pl.BlockSpec((1, d), lambda i, f: (0, 0)),
        ],
        out_specs=pl.BlockSpec((tm, d), lambda i, f: (i, 0)),
        out_shape=jax.ShapeDtypeStruct((t, d), F32),
        scratch_shapes=[
            pltpu.VMEM((HALO + tm, d), BF16),
            pltpu.VMEM((HALO + tm, tf), F32),
        ],
        compiler_params=_cparams(("parallel", "arbitrary")),
        name="ffn",
    )(x2d, x2d, prm["ffn_norm"], wg16, wu16, prm["ffn_dw_w"], prm["ffn_dw_b"], wd16, final_gain)


_W_IN_PIECES = (
    ((1344, 1856),), ((1856, 2368),),
    ((832, 1344),), ((2888, 3400),), ((3416, 3928),), ((2376, 2632),), ((2632, 2888),),
    ((2368, 2376), (3400, 3416)),
    ((0, 512),), ((512, 768),),
    ((768, 832), (768, 832)),
)


def _reorder_w_in_body(wt_ref, o_ref):
    tk = wt_ref.shape[1]
    col = 0
    for ranges in _W_IN_PIECES:
        parts = [wt_ref[a:b, :] for a, b in ranges]
        rows = sum(b - a for a, b in ranges)
        if rows % 128:
            parts.append(jnp.zeros((-rows % 128, tk), F32))
            rows += -rows % 128
        piece = parts[0] if len(parts) == 1 else jnp.concatenate(parts, axis=0)
        o_ref[:, col:col + rows] = piece.T.astype(BF16)
        col += rows
    assert col == P_WIDTH


def _reorder_w_in(w_in_t, layer, tk=512):
    _, d_in, d = w_in_t.shape
    return pl.pallas_call(
        _reorder_w_in_body,
        grid=(d // tk,),
        in_specs=[pl.BlockSpec((None, d_in, tk), lambda i: (layer, 0, i))],
        out_specs=pl.BlockSpec((tk, P_WIDTH), lambda i: (i, 0)),
        out_shape=jax.ShapeDtypeStruct((d, P_WIDTH), BF16),
        compiler_params=_cparams(("parallel",)),
        name="reorder_w_in",
    )(w_in_t)


def _prepare(seq, attn_norm, mla_q_norm, mla_w_uq, mla_kv_norm, mla_w_ukv,
             ssd_conv_w, ssd_conv_b, ssd_dt_bias, ssd_a_log, ssd_d, ssd_norm,
             gla_w_gk, gla_b_gk, gla_norm, ffn_norm, ffn_dw_w, ffn_dw_b):
    depth = mla_w_uq.shape[0]
    wq = mla_w_uq.reshape(depth, MLA_Q_RANK, MLA_HEADS, MLA_NOPE + MLA_ROPE)
    wq = jnp.concatenate([wq, wq[..., MLA_NOPE:]], axis=-1)
    wq = wq.reshape(depth, MLA_Q_RANK, MLA_HEADS * MLA_QK).astype(BF16)
    wkv = mla_w_ukv.astype(BF16)

    inv_freq = ROPE_BASE ** (-jnp.arange(0, MLA_ROPE, 2, dtype=F32) / MLA_ROPE)
    ang = jnp.arange(seq, dtype=F32)[:, None] * inv_freq[None, :]
    cos, sin = jnp.cos(ang), jnp.sin(ang)
    zeros = jnp.zeros((seq, MLA_ROPE), F32)
    cc = jnp.concatenate([cos, cos, zeros], axis=1)
    ss = jnp.concatenate([-sin, sin, zeros], axis=1)

    pad_lanes = lambda a: jnp.pad(a, ((0, 0), (0, 128 - a.shape[1])))[:, None, :]
    wgk = jnp.zeros((depth, 128, GLA_KD), F32).at[:, SM_GLOW:SM_GLOW + GLA_GATE_RANK, :].set(gla_w_gk)

    rows = jnp.arange(GLA_SUB * GLA_KD)
    cols = jnp.arange(GLA_HEADS * GLA_CHUNK)
    seg = ((rows[:, None] % GLA_KD) // GLA_HEAD_K == cols[None, :] // GLA_CHUNK) & (
        rows[:, None] // GLA_KD == cols[None, :] % GLA_SUB)

    return dict(
        attn_norm=attn_norm[:, None, :],
        mla_q_norm=mla_q_norm[:, None, :], mla_kv_norm=mla_kv_norm[:, None, :], wq=wq, wkv=wkv,
        cc=cc, ss=ss,
        ssd_conv_w=ssd_conv_w, ssd_conv_b=ssd_conv_b[:, None, :],
        ssd_dtb=pad_lanes(ssd_dt_bias), ssd_dtb_t=ssd_dt_bias[:, :, None],
        ssd_alog=pad_lanes(ssd_a_log), ssd_alog_t=ssd_a_log[:, :, None],
        ssd_dexp=jnp.repeat(ssd_d, SSD_HEAD_DIM, axis=1)[:, None, :], ssd_norm=ssd_norm[:, None, :],
        gla_wgk=wgk.astype(BF16), gla_bgk=gla_b_gk[:, None, :], gla_norm=gla_norm[:, None, :],
        gla_seg=seg.astype(BF16),
        ffn_norm=ffn_norm[:, None, :], ffn_dw_w=ffn_dw_w, ffn_dw_b=ffn_dw_b[:, None, :],
    )


def kernel(x, attn_norm, w_in, mla_q_norm, mla_w_uq, mla_kv_norm, mla_w_ukv, ssd_conv_w, ssd_conv_b,
           ssd_dt_bias, ssd_a_log, ssd_d, ssd_norm, gla_w_gk, gla_b_gk, gla_norm, w_out, ffn_norm,
           ffn_w_gate, ffn_w_up, ffn_dw_w, ffn_dw_b, ffn_w_down, final_norm):
    batch, seq, d = x.shape
    prm = _prepare(seq, attn_norm, mla_q_norm, mla_w_uq, mla_kv_norm, mla_w_ukv,
                   ssd_conv_w, ssd_conv_b, ssd_dt_bias, ssd_a_log, ssd_d, ssd_norm,
                   gla_w_gk, gla_b_gk, gla_norm, ffn_norm, ffn_dw_w, ffn_dw_b)
    depth = w_in.shape[0]
    w_in_t = jnp.swapaxes(w_in, 1, 2)
    w_in16 = _reorder_w_in(w_in_t, 0)
    x2d = x.reshape(batch * seq, d)
    for layer in range(depth):
        proj, qt, k, vt, wg16, wu16 = _in_proj(x2d, prm, w_in16, layer, batch, seq,
                                               (ffn_w_gate, ffn_w_up))
        y_mla, wd16, wo16 = _mla_attn(qt, k, vt, layer, (ffn_w_down, w_out))
        y_mla = y_mla.reshape(batch * seq, MLA_HEADS * MLA_V)
        y_ssd = _ssd(proj, prm, layer, batch, seq)
        y_gla = _gla(proj, prm, layer, batch, seq)
        x2d, w_in16 = _out_proj(x2d, y_mla, y_ssd, y_gla, wo16, w_in_t,
                                layer + 1 if layer + 1 < depth else None)
        x2d = _ffn(x2d, prm, wg16, wu16, wd16, final_norm[None, :], layer, seq)
    return x2d.reshape(batch, seq, d)
```

```python
import functools

import jax
import jax.numpy as jnp
from jax import lax
from jax.experimental import pallas as pl
from jax.experimental.pallas import tpu as pltpu

F32 = jnp.float32
BF16 = jnp.bfloat16

EPS = 1e-6
MLA_HEADS = 8
MLA_Q_RANK = 512
MLA_KV_RANK = 256
MLA_NOPE = 128
MLA_ROPE = 64
MLA_V = 128
MLA_QK = 256
MLA_VE = MLA_V + 16
LOG2_E = 1.4426950408889634
ROPE_BASE = 10000.0
SSD_HEADS = 8
SSD_HEAD_DIM = 64
SSD_STATE = 128
SSD_CONV = 4
SSD_CHUNK = 128
SSD_INNER = 512
SSD_CONV_DIM = 1024
GLA_HEADS = 4
GLA_HEAD_K = 64
GLA_HEAD_V = 128
GLA_GATE_RANK = 16
GLA_GATE_TAU = 16.0
GLA_CHUNK = 64
GLA_SUB = 16
GLA_KD = 256
GLA_VD = 512
D_FF = 5632
FFN_CONV = 3
HALO = 16

P_WIDTH = 4096
P_OUT = 3200
C_XBC, C_Z, C_GV, C_GR, C_GQ, C_GK, C_SM, C_CQ, C_CKV, C_KPE = (
    0, 1024, 1536, 2048, 2560, 2816, 3072, 3200, 3712, 3968)
SM_GLOW = 8

VMEM_LIMIT = 52 * 1024 * 1024


def _cparams(sem):
    return pltpu.CompilerParams(dimension_semantics=sem, vmem_limit_bytes=VMEM_LIMIT)


def _rms(x, g):
    var = jnp.mean(x * x, axis=-1, keepdims=True)
    return x * lax.rsqrt(var + EPS) * g


def _softplus(x):
    return jnp.maximum(x, 0.0) + jnp.log(1.0 + jnp.exp(-jnp.abs(x)))


def _silu(x):
    return x * jax.nn.sigmoid(x)


def _dot(a, b):
    return jnp.dot(a, b, preferred_element_type=F32)


def _dot_nt(a, b):
    return lax.dot_general(a, b, (((1,), (1,)), ((), ())), preferred_element_type=F32)


def _dot_tn(a, b):
    return lax.dot_general(a, b, (((0,), (0,)), ((), ())), preferred_element_type=F32)


def _iota(shape, axis):
    return lax.broadcasted_iota(jnp.int32, shape, axis)


def _split3(x):
    hi = x.astype(BF16)
    r = x - hi.astype(F32)
    mid = r.astype(BF16)
    lo = (r - mid.astype(F32)).astype(BF16)
    return hi, mid, lo


def _select_sum(mask, x):
    m = mask.astype(BF16)
    hi, mid, lo = _split3(x)
    return (_dot(m, lo) + _dot(m, mid)) + _dot(m, hi)


def _select_sum_t(x, mask):
    m = mask.astype(BF16)
    hi, mid, lo = _split3(x)
    return (_dot(lo, m) + _dot(mid, m)) + _dot(hi, m)


def _cast_specs(weights, layer, steps, step_index):
    in_specs, out_specs, out_shapes = [], [], []
    for w in weights:
        _, rows, cols = w.shape
        slab = rows // steps
        assert slab * steps == rows and slab % 16 == 0, (rows, steps)
        in_specs.append(pl.BlockSpec((None, slab, cols), lambda *g: (layer, step_index(*g), 0)))
        out_specs.append(pl.BlockSpec((slab, cols), lambda *g: (step_index(*g), 0)))
        out_shapes.append(jax.ShapeDtypeStruct((rows, cols), BF16))
    return in_specs, out_specs, out_shapes


def _cast_slabs(src_refs, dst_refs):
    for src, dst in zip(src_refs, dst_refs):
        dst[...] = src[...].astype(BF16)


def _rope(blk, cc, ss):
    return blk * cc + pltpu.roll(blk, MLA_ROPE // 2, axis=1) * ss


def _in_proj_body(x_ref, g_ref, w_ref, qn_ref, kvn_ref, wq_ref, wkv_ref, cc_ref, ss_ref, *refs, tn):
    n_cast = (len(refs) - 4) // 2
    o_ref, qt_out, k_out, vt_out = refs[n_cast:n_cast + 4]
    tm = x_ref.shape[0]
    h = _rms(x_ref[...], g_ref[...]).astype(BF16)
    mla = _dot(h, w_ref[:, P_OUT:])
    cq = mla[:, C_CQ - P_OUT:C_CKV - P_OUT]
    ckv = mla[:, C_CKV - P_OUT:C_KPE - P_OUT]
    cc = cc_ref[...]
    ss = ss_ref[...]
    kpe = _rope(mla[:, C_KPE - P_OUT:], cc, ss).astype(BF16)
    qf = _dot(_rms(cq, qn_ref[...]).astype(BF16), wq_ref[...])
    kvf = _dot(_rms(ckv, kvn_ref[...]).astype(BF16), wkv_ref[...])
    scale = (MLA_NOPE + MLA_ROPE) ** -0.5 * LOG2_E
    ones = jnp.ones((MLA_VE - MLA_V, tm), BF16)

    def head_outputs(hd):
        base = hd * MLA_QK
        q_pe = _rope(qf[:, base + MLA_NOPE:base + MLA_QK], cc, ss)
        q_h = jnp.concatenate([qf[:, base:base + MLA_NOPE], q_pe], axis=1) * scale
        qt_out[hd] = q_h.T.astype(BF16)
        k_out[hd] = jnp.concatenate([kvf[:, base:base + MLA_NOPE].astype(BF16), kpe], axis=1)
        vt_out[hd, 0:MLA_V, :] = kvf[:, base + MLA_NOPE:base + MLA_QK].T.astype(BF16)
        vt_out[hd, MLA_V:MLA_VE, :] = ones

    starts = list(range(0, P_OUT, tn))
    heads_per_chunk = -(-MLA_HEADS // len(starts))
    for idx, c0 in enumerate(starts):
        cols = slice(c0, min(c0 + tn, P_OUT))
        o_ref[:, cols] = _dot(h, w_ref[:, cols])
        for hd in range(idx * heads_per_chunk, min((idx + 1) * heads_per_chunk, MLA_HEADS)):
            head_outputs(hd)
    _cast_slabs(refs[:n_cast], refs[n_cast + 4:])


def _in_proj(x2d, prm, w_in16, layer, batch, seq, cast_weights, tm=256, tn=1024):
    t, d = x2d.shape
    tm = min(tm, seq)
    steps = t // tm
    nsb = seq // tm
    H = MLA_HEADS
    resident = lambda shape: pl.BlockSpec((None,) + shape, lambda i: (layer, 0, 0),
                                          pipeline_mode=pl.Buffered(1))
    c_in, c_out, c_shapes = _cast_specs(cast_weights, layer, steps, lambda i: i)
    return pl.pallas_call(
        functools.partial(_in_proj_body, tn=tn),
        grid=(steps,),
        in_specs=[
            pl.BlockSpec((tm, d), lambda i: (i, 0)),
            pl.BlockSpec((None, 1, d), lambda i: (layer, 0, 0)),
            pl.BlockSpec((d, P_WIDTH), lambda i: (0, 0), pipeline_mode=pl.Buffered(1)),
            pl.BlockSpec((None, 1, MLA_Q_RANK), lambda i: (layer, 0, 0)),
            pl.BlockSpec((None, 1, MLA_KV_RANK), lambda i: (layer, 0, 0)),
            resident((MLA_Q_RANK, H * MLA_QK)),
            resident((MLA_KV_RANK, H * MLA_QK)),
            pl.BlockSpec((tm, 128), lambda i: (i % nsb, 0)),
            pl.BlockSpec((tm, 128), lambda i: (i % nsb, 0)),
        ] + c_in,
        out_specs=[
            pl.BlockSpec((tm, P_OUT), lambda i: (i, 0)),
            pl.BlockSpec((None, H, MLA_QK, tm), lambda i: (i // nsb, 0, 0, i % nsb)),
            pl.BlockSpec((None, H, tm, MLA_QK), lambda i: (i // nsb, 0, i % nsb, 0)),
            pl.BlockSpec((None, H, None, MLA_VE, tm), lambda i: (i // nsb, 0, i % nsb, 0, 0)),
        ] + c_out,
        out_shape=[
            jax.ShapeDtypeStruct((t, P_OUT), F32),
            jax.ShapeDtypeStruct((batch, H, MLA_QK, seq), BF16),
            jax.ShapeDtypeStruct((batch, H, seq, MLA_QK), BF16),
            jax.ShapeDtypeStruct((batch, H, nsb, MLA_VE, tm), BF16),
        ] + c_shapes,
        compiler_params=_cparams(("arbitrary",)),
        name="in_proj",
    )(x2d, prm["attn_norm"], w_in16, prm["mla_q_norm"], prm["mla_kv_norm"], prm["wq"],
      prm["wkv"], prm["cc"], prm["ss"], *cast_weights)


def _attn_body(qt_ref, k_ref, vt_ref, *refs, tq, hp):
    n_cast = (len(refs) - 2) // 2
    o_ref, acc_scr = refs[n_cast], refs[-1]
    _cast_slabs(refs[:n_cast], refs[n_cast + 1:-1])
    qi = pl.program_id(2)
    qts = [qt_ref[h] for h in range(hp)]

    def scores(kb, h):
        start = pl.multiple_of(kb * tq, tq)
        return _dot(k_ref[h, pl.ds(start, tq), :], qts[h])

    def update(kb, h, s, m, diagonal):
        if diagonal:
            s = jnp.where(_iota((tq, tq), 0) <= _iota((tq, tq), 1), s, -jnp.inf)
        m_new = jnp.maximum(m, jnp.max(s, axis=0, keepdims=True))
        p = jnp.exp2(s - m_new).astype(BF16)
        acc_scr[h] = jnp.exp2(m - m_new) * acc_scr[h] + _dot(vt_ref[h, kb], p)
        return m_new

    def step(kb, ms, diagonal):
        ss = [scores(kb, h) for h in range(hp)]
        return tuple(update(kb, h, ss[h], ms[h], diagonal) for h in range(hp))

    def pair_step(kp, ms):
        ka, kb = 2 * kp, 2 * kp + 1
        sa = [scores(ka, h) for h in range(hp)]
        sb, ma = [], []
        for h in range(hp):
            ma.append(update(ka, h, sa[h], ms[h], False))
            sb.append(scores(kb, h))
        return tuple(update(kb, h, sb[h], ma[h], False) for h in range(hp))

    acc_scr[...] = jnp.zeros_like(acc_scr)
    init = tuple(jnp.full((1, tq), -jnp.inf, F32) for _ in range(hp))
    ms = lax.fori_loop(0, qi // 2, pair_step, init)
    ms = lax.cond(qi % 2 == 1, lambda c: step(qi - 1, c, False), lambda c: c, ms)
    step(qi, ms, True)
    outs = []
    for h in range(hp):
        acc = acc_scr[h]
        outs.append((acc[0:MLA_V, :] / acc[MLA_V:MLA_V + 1, :]).T)
    o_ref[...] = jnp.concatenate(outs, axis=1).astype(BF16)


def _mla_attn(qt, k, vt, layer, cast_weights, hp=8):
    batch, heads, _, seq = qt.shape
    tq = vt.shape[-1]
    nh, nq = heads // hp, seq // tq
    c_in, c_out, c_shapes = _cast_specs(cast_weights, layer, batch * nh * nq,
                                        lambda b, h, i: (b * nh + h) * nq + i)
    return pl.pallas_call(
        functools.partial(_attn_body, tq=tq, hp=hp),
        grid=(batch, nh, nq),
        in_specs=[
            pl.BlockSpec((None, hp, MLA_QK, tq), lambda b, h, i: (b, h, 0, i)),
            pl.BlockSpec((None, hp, seq, MLA_QK), lambda b, h, i: (b, h, 0, 0)),
            pl.BlockSpec((None, hp, seq // tq, MLA_VE, tq), lambda b, h, i: (b, h, 0, 0, 0)),
        ] + c_in,
        out_specs=[pl.BlockSpec((None, tq, hp * MLA_V), lambda b, h, i: (b, i, h))] + c_out,
        out_shape=[jax.ShapeDtypeStruct((batch, seq, heads * MLA_V), BF16)] + c_shapes,
        scratch_shapes=[pltpu.VMEM((hp, MLA_VE, tq), F32)],
        compiler_params=_cparams(("arbitrary", "arbitrary", "arbitrary")),
        name="mla_attn",
    )(qt, k, vt, *cast_weights)


def _ssd_body(z_ref, xbc_ref, sm_ref, cw_ref, cb_ref, dtb_ref, dtbt_ref, alog_ref, alogt_ref,
              dexp_ref, ng_ref, o_ref, xb_scr, st_scr, *, nch):
    L, N, P = SSD_CHUNK, SSD_STATE, SSD_HEAD_DIM
    T = nch * L
    pad = 8

    @pl.when(pl.program_id(1) == 0)
    def _():
        xb_scr[0:pad, :] = jnp.zeros((pad, SSD_CONV_DIM), F32)
        st_scr[...] = jnp.zeros_like(st_scr)

    xb_scr[pad:pad + T, :] = xbc_ref[...]
    cw = cw_ref[...]
    conv = cb_ref[...] + cw[3:4] * xb_scr[pad:pad + T, :]
    for tap in range(1, SSD_CONV):
        conv = conv + cw[3 - tap:4 - tap] * xb_scr[pad - tap:pad - tap + T, :]
    xb_scr[0:pad, :] = xb_scr[T:T + pad, :]
    act = _silu(conv)
    xs = act[:, :SSD_INNER]
    bm = act[:, SSD_INNER:SSD_INNER + 2 * N]
    cm = act[:, SSD_INNER + 2 * N:]

    sm = sm_ref[...]
    lane = _iota((1, 128), 1)
    nega = jnp.where(lane < SSD_HEADS, -LOG2_E * jnp.exp(alog_ref[...]), 0.0)
    dt = _softplus(sm + dtb_ref[...])
    a = dt * nega
    smt = sm.T[0:SSD_HEADS, :]
    dtt = _softplus(smt + dtbt_ref[...])
    at = dtt * (-LOG2_E * jnp.exp(alogt_ref[...]))
    row = _iota((T, T), 0)
    col = _iota((T, T), 1)
    same_chunk = row // L == col // L
    acum = _select_sum(jnp.where(same_chunk, (row >= col).astype(F32), 0.0), a)
    acumt = _select_sum_t(at, jnp.where(same_chunk, (row <= col).astype(F32), 0.0))

    colb = [jnp.broadcast_to(acum[:, h:h + 1], (T, L)) for h in range(SSD_HEADS)]
    dtb = [jnp.broadcast_to(dt[:, h:h + 1], (T, L)) for h in range(SSD_HEADS)]
    left_t = _iota((T, 2 * P), 1) < P
    left = _iota((L, 2 * P), 1) < P
    causal = _iota((L, L), 0) >= _iota((L, L), 1)

    def per_head_lanes(vals):
        return jnp.concatenate([jnp.where(left_t, vals[2 * p], vals[2 * p + 1])
                                for p in range(SSD_HEADS // 2)], axis=1)

    acum_e = per_head_lanes(colb)
    x_dt = xs * per_head_lanes(dtb)
    x_dt16 = x_dt.astype(BF16)

    ht = st_scr[...]
    heads_per_group = SSD_HEADS // 2
    width = heads_per_group * P
    y_chunks = []
    for c in range(nch):
        rows = slice(c * L, (c + 1) * L)
        alast_e = acum_e[(c + 1) * L - 1:(c + 1) * L, :]
        xd = (x_dt[rows, :] * jnp.exp2(alast_e - acum_e[rows, :])).astype(BF16)
        ht16 = ht.astype(BF16)
        g_mats, c_mats, st_new = [], [], []
        for g in range(2):
            b_g = bm[rows, g * N:(g + 1) * N]
            c_g = cm[rows, g * N:(g + 1) * N]
            c_mats.append(c_g)
            g_mats.append(_dot_nt(c_g.astype(BF16), b_g.astype(BF16)))
            st_new.append(_dot(b_g.T.astype(BF16), xd[:, g * width:(g + 1) * width]))
        ys = []
        for p in range(SSD_HEADS // 2):
            g = (2 * p) // heads_per_group
            rhs = jnp.concatenate([x_dt16[rows, 2 * p * P:(2 * p + 2) * P],
                                   ht16[:, 2 * p * P:(2 * p + 2) * P]], axis=0)
            pair = []
            for h in (2 * p, 2 * p + 1):
                colb_h = colb[h][rows, :]
                rowb = jnp.broadcast_to(acumt[h:h + 1, rows], (L, L))
                decay = jnp.exp2(jnp.where(causal, colb_h - rowb, -jnp.inf))
                lhs = jnp.concatenate([g_mats[g] * decay, c_mats[g] * jnp.exp2(colb_h)], axis=1)
                pair.append(_dot(lhs.astype(BF16), rhs))
            ys.append(jnp.where(left, pair[0], pair[1]))
        y_chunks.append(jnp.concatenate(ys, axis=1))
        ht = ht * jnp.exp2(alast_e) + jnp.concatenate(st_new, axis=1)
    st_scr[...] = ht
    y = jnp.concatenate(y_chunks, axis=0) + xs * dexp_ref[...]
    o_ref[...] = _rms(y * _silu(z_ref[...]), ng_ref[...]).astype(BF16)


def _ssd(proj, prm, layer, batch, seq, nch=4):
    L = SSD_CHUNK * nch
    nc = seq // L
    t = proj.shape[0]
    row = lambda b, c: b * nc + c
    lay3 = lambda shape: pl.BlockSpec((None,) + shape, lambda b, c: (layer, 0, 0))
    return pl.pallas_call(
        functools.partial(_ssd_body, nch=nch),
        grid=(batch, nc),
        in_specs=[
            pl.BlockSpec((L, SSD_INNER), lambda b, c: (row(b, c), C_Z // SSD_INNER)),
            pl.BlockSpec((L, SSD_CONV_DIM), lambda b, c: (row(b, c), C_XBC // SSD_CONV_DIM)),
            pl.BlockSpec((L, 128), lambda b, c: (row(b, c), C_SM // 128)),
            lay3((SSD_CONV, SSD_CONV_DIM)),
            lay3((1, SSD_CONV_DIM)),
            lay3((1, 128)),
            lay3((SSD_HEADS, 1)),
            lay3((1, 128)),
            lay3((SSD_HEADS, 1)),
            lay3((1, SSD_INNER)),
            lay3((1, SSD_INNER)),
        ],
        out_specs=pl.BlockSpec((L, SSD_INNER), lambda b, c: (row(b, c), 0)),
        out_shape=jax.ShapeDtypeStruct((t, SSD_INNER), BF16),
        scratch_shapes=[
            pltpu.VMEM((L + 8, SSD_CONV_DIM), F32),
            pltpu.VMEM((SSD_STATE, SSD_INNER), F32),
        ],
        compiler_params=_cparams(("parallel", "arbitrary")),
        name="ssd",
    )(proj, proj, proj, prm["ssd_conv_w"], prm["ssd_conv_b"], prm["ssd_dtb"], prm["ssd_dtb_t"],
      prm["ssd_alog"], prm["ssd_alog_t"], prm["ssd_dexp"], prm["ssd_norm"])


def _gla_body(q_ref, k_ref, v_ref, r_ref, sm_ref, wgk_ref, bgk_ref, gn_ref, seg_ref, o_ref,
              g_scr, st_scr, *, nch):
    L, C, H, K, V = GLA_CHUNK, GLA_SUB, GLA_HEADS, GLA_HEAD_K, GLA_HEAD_V
    nb = L // C
    T = nch * L

    @pl.when(pl.program_id(1) == 0)
    def _():
        st_scr[...] = jnp.zeros_like(st_scr)

    xg = _dot(sm_ref[...].astype(BF16), wgk_ref[...]) + bgk_ref[...]
    log_a = -_softplus(-xg) * (LOG2_E / GLA_GATE_TAU)
    rr = _iota((T, T), 0)
    cc = _iota((T, T), 1)
    tril = jnp.where(rr // L == cc // L, (rr >= cc).astype(F32), 0.0)
    g_all = _select_sum(tril, log_a)
    g_scr[...] = g_all
    qs_all = q_ref[...] * (K ** -0.5)

    head_rows_k = _iota((H * L, GLA_KD), 0) // L == _iota((H * L, GLA_KD), 1) // K
    head_rows_v = _iota((H * L, GLA_VD), 0) // L == _iota((H * L, GLA_VD), 1) // V
    rowblk = _iota((L, H * L), 0) // C
    colblk = (_iota((L, H * L), 1) % L) // C
    lrow = _iota((C, GLA_KD), 0)

    attn_off, rel_prod, q_dec, st_upd, g_last = [], [], [], [], []
    for ch in range(nch):
        base = ch * L
        g = g_all[base:base + L, :]
        qs = qs_all[base:base + L, :]
        k = k_ref[base:base + L, :]
        ends = [g_scr[base + C * j + C - 1:base + C * j + C, :] for j in range(nb)]
        end_b = jnp.concatenate([jnp.broadcast_to(e, (C, GLA_KD)) for e in ends], axis=0)
        kd = k * jnp.exp2(end_b - g)
        k_exp = jnp.where(head_rows_k, jnp.concatenate([kd] * H, axis=0), 0.0).astype(BF16)

        attn = jnp.zeros((L, H * L), F32)
        for j in range(nb - 1):
            qd = qs * jnp.exp2(jnp.minimum(g - ends[j], 0.0))
            aj = _dot_nt(qd.astype(BF16), k_exp)
            attn = jnp.where(colblk == j, jnp.where(rowblk > j, aj, 0.0), attn)

        pieces = []
        for j in range(C):
            blocks = []
            for i in range(nb):
                r0 = C * i
                gj = g_scr[base + r0 + j:base + r0 + j + 1, :]
                kj = k_ref[base + r0 + j:base + r0 + j + 1, :]
                rel = jnp.exp2(jnp.where(lrow >= j, g[r0:r0 + C, :] - gj, -jnp.inf))
                blocks.append(qs[r0:r0 + C, :] * kj * rel)
            pieces.append(jnp.concatenate(blocks, axis=0).astype(BF16))
        attn_off.append(attn)
        rel_prod.append(jnp.concatenate(pieces, axis=1))
        q_dec.append((qs * jnp.exp2(g)).astype(BF16))
        g_last.append(ends[nb - 1])
        k_dec = (k * jnp.exp2(ends[nb - 1] - g)).astype(BF16)
        st_upd.append(_dot_tn(v_ref[base:base + L, :].astype(BF16), k_dec))

    a_diag = _dot(jnp.concatenate(rel_prod, axis=0), seg_ref[...])
    o_intra = []
    for ch in range(nch):
        base = ch * L
        attn = jnp.where(colblk == rowblk, a_diag[base:base + L, :], attn_off[ch])
        v = v_ref[base:base + L, :]
        v_bd = jnp.where(head_rows_v, jnp.concatenate([v] * H, axis=0), 0.0).astype(BF16)
        o_intra.append(_dot(attn.astype(BF16), v_bd))

    same_head = _iota((GLA_VD, GLA_KD), 0) // V == _iota((GLA_VD, GLA_KD), 1) // K
    st = st_scr[...]
    for ch in range(nch):
        base = ch * L
        o = o_intra[ch] + _dot_nt(q_dec[ch], st.astype(BF16))
        st = st * jnp.exp2(g_last[ch]) + jnp.where(same_head, st_upd[ch], 0.0)
        normed = jnp.concatenate([_rms(o[:, h * V:(h + 1) * V], gn_ref[...]) for h in range(H)], axis=1)
        o_ref[base:base + L, :] = (normed * _silu(r_ref[base:base + L, :])).astype(BF16)
    st_scr[...] = st


def _gla(proj, prm, layer, batch, seq, nch=8):
    L = GLA_CHUNK * nch
    nc = seq // L
    t = proj.shape[0]
    row = lambda b, c: b * nc + c
    lay3 = lambda shape: pl.BlockSpec((None,) + shape, lambda b, c: (layer, 0, 0))
    seg = prm["gla_seg"]
    return pl.pallas_call(
        functools.partial(_gla_body, nch=nch),
        grid=(batch, nc),
        in_specs=[
            pl.BlockSpec((L, GLA_KD), lambda b, c: (row(b, c), C_GQ // GLA_KD)),
            pl.BlockSpec((L, GLA_KD), lambda b, c: (row(b, c), C_GK // GLA_KD)),
            pl.BlockSpec((L, GLA_VD), lambda b, c: (row(b, c), C_GV // GLA_VD)),
            pl.BlockSpec((L, GLA_VD), lambda b, c: (row(b, c), C_GR // GLA_VD)),
            pl.BlockSpec((L, 128), lambda b, c: (row(b, c), C_SM // 128)),
            lay3((128, GLA_KD)),
            lay3((1, GLA_KD)),
            lay3((1, GLA_HEAD_V)),
            pl.BlockSpec(seg.shape, lambda b, c: (0, 0)),
        ],
        out_specs=pl.BlockSpec((L, GLA_VD), lambda b, c: (row(b, c), 0)),
        out_shape=jax.ShapeDtypeStruct((t, GLA_VD), BF16),
        scratch_shapes=[
            pltpu.VMEM((L, GLA_KD), F32),
            pltpu.VMEM((GLA_VD, GLA_KD), F32),
        ],
        compiler_params=_cparams(("parallel", "arbitrary")),
        name="gla",
    )(proj, proj, proj, proj, proj, prm["gla_wgk"], prm["gla_bgk"], prm["gla_norm"], seg)


def _out_proj_body(x_ref, ya_ref, yb_ref, yc_ref, w_ref, *refs, tn):
    o_ref = refs[-2] if len(refs) == 3 else refs[0]
    y = jnp.concatenate([ya_ref[...], yb_ref[...], yc_ref[...]], axis=1)
    for j in range(o_ref.shape[1] // tn):
        cols = slice(j * tn, (j + 1) * tn)
        o_ref[:, cols] = x_ref[:, cols] + _dot(y, w_ref[:, cols])
    if len(refs) == 3:
        _reorder_w_in_body(refs[0], refs[2])


def _out_proj(x2d, y_mla, y_ssd, y_gla, w16, w_in_t, next_layer, tm=512, tn=1024):
    t, d = x2d.shape
    da, db, dc = y_mla.shape[1], y_ssd.shape[1], y_gla.shape[1]
    tm = min(tm, t)
    steps = t // tm
    in_specs = [
        pl.BlockSpec((tm, d), lambda i: (i, 0)),
        pl.BlockSpec((tm, da), lambda i: (i, 0)),
        pl.BlockSpec((tm, db), lambda i: (i, 0)),
        pl.BlockSpec((tm, dc), lambda i: (i, 0)),
        pl.BlockSpec((da + db + dc, d), lambda i: (0, 0), pipeline_mode=pl.Buffered(1)),
    ]
    out_specs = [pl.BlockSpec((tm, d), lambda i: (i, 0))]
    out_shape = [jax.ShapeDtypeStruct((t, d), F32)]
    args = [x2d, y_mla, y_ssd, y_gla, w16]
    if next_layer is not None:
        slab = d // steps
        assert slab * steps == d and slab % 128 == 0, (d, steps)
        in_specs.append(pl.BlockSpec((None, w_in_t.shape[1], slab), lambda i: (next_layer, 0, i)))
        out_specs.append(pl.BlockSpec((slab, P_WIDTH), lambda i: (i, 0)))
        out_shape.append(jax.ShapeDtypeStruct((d, P_WIDTH), BF16))
        args.append(w_in_t)
    outs = pl.pallas_call(
        functools.partial(_out_proj_body, tn=tn),
        grid=(steps,),
        in_specs=in_specs,
        out_specs=out_specs,
        out_shape=out_shape,
        compiler_params=_cparams(("arbitrary",)),
        name="out_proj",
    )(*args)
    return outs if next_layer is not None else (outs[0], None)


def _ffn_body(x_ref, xh_ref, g_ref, wg_ref, wu_ref, cw_ref, cb_ref, wd_ref, fg_ref, o_ref,
              h_scr, gate_scr, *, tm, tiles_per_seq, n_out_chunks, last_layer):
    i = pl.program_id(0)
    f = pl.program_id(1)

    @pl.when(f == 0)
    def _():
        x = x_ref[...]
        h_scr[0:HALO, :] = _rms(xh_ref[...], g_ref[...]).astype(BF16)
        h_scr[HALO:HALO + tm, :] = _rms(x, g_ref[...]).astype(BF16)
        o_ref[...] = x

    gate = _dot(h_scr[...], wg_ref[...])
    keep = (i % tiles_per_seq != 0).astype(F32)
    gate_scr[0:HALO, :] = gate[0:HALO, :] * keep
    gate_scr[HALO:HALO + tm, :] = gate[HALO:, :]
    cw = cw_ref[...]
    conv = cb_ref[...] + cw[FFN_CONV - 1:FFN_CONV] * gate[HALO:, :]
    for tap in range(1, FFN_CONV):
        conv = conv + cw[FFN_CONV - 1 - tap:FFN_CONV - tap] * gate_scr[HALO - tap:HALO - tap + tm, :]
    act = (_silu(conv) * _dot(h_scr[HALO:HALO + tm, :], wu_ref[...])).astype(BF16)
    dn = o_ref.shape[1] // n_out_chunks
    for c in range(n_out_chunks):
        o_ref[:, c * dn:(c + 1) * dn] += _dot(act, wd_ref[:, c * dn:(c + 1) * dn])

    if last_layer:
        @pl.when(f == pl.num_programs(1) - 1)
        def _():
            o_ref[...] = _rms(o_ref[...], fg_ref[...])


def _ffn(x2d, prm, wg16, wu16, wd16, final_gain, layer, seq, tm=512, tf=512):
    t, d = x2d.shape
    tm = min(tm, seq)
    tiles_per_seq = seq // tm
    halo_blocks = tm // HALO
    last_layer = layer == prm["ffn_norm"].shape[0] - 1
    lay = lambda shape, idx: pl.BlockSpec((None,) + shape, idx)
    return pl.pallas_call(
        functools.partial(_ffn_body, tm=tm, tiles_per_seq=tiles_per_seq, n_out_chunks=4,
                          last_layer=last_layer),
        grid=(t // tm, D_FF // tf),
        in_specs=[
            pl.BlockSpec((tm, d), lambda i, f: (i, 0)),
            pl.BlockSpec((HALO, d), lambda i, f: (jnp.maximum(i * halo_blocks - 1, 0), 0)),
            lay((1, d), lambda i, f: (layer, 0, 0)),
            pl.BlockSpec((d, tf), lambda i, f: (0, f)),
            pl.BlockSpec((d, tf), lambda i, f: (0, f)),
            lay((FFN_CONV, tf), lambda i, f: (layer, 0, f)),
            lay((1, tf), lambda i, f: (layer, 0, f)),
            pl.BlockSpec((tf, d), lambda i, f: (f, 0)),
            pl.BlockSpec((1, d), lambda i, f: (0, 0)),
        ],
        out_specs=pl.BlockSpec((tm, d), lambda i, f: (i, 0)),
        out_shape=jax.ShapeDtypeStruct((t, d), F32),
        scratch_shapes=[
            pltpu.VMEM((HALO + tm, d), BF16),
            pltpu.VMEM((HALO + tm, tf), F32),
        ],
        compiler_params=_cparams(("parallel", "arbitrary")),
        name="ffn",
    )(x2d, x2d, prm["ffn_norm"], wg16, wu16, prm["ffn_dw_w"], prm["ffn_dw_b"], wd16, final_gain)


_W_IN_PIECES = (
    ((1344, 1856),), ((1856, 2368),),
    ((832, 1344),), ((2888, 3400),), ((3416, 3928),), ((2376, 2632),), ((2632, 2888),),
    ((2368, 2376), (3400, 3416)),
    ((0, 512),), ((512, 768),),
    ((768, 832), (768, 832)),
)


def _reorder_w_in_body(wt_ref, o_ref):
    tk = wt_ref.shape[1]
    col = 0
    for ranges in _W_IN_PIECES:
        parts = [wt_ref[a:b, :] for a, b in ranges]
        rows = sum(b - a for a, b in ranges)
        if rows % 128:
            parts.append(jnp.zeros((-rows % 128, tk), F32))
            rows += -rows % 128
        piece = parts[0] if len(parts) == 1 else jnp.concatenate(parts, axis=0)
        o_ref[:, col:col + rows] = piece.T.astype(BF16)
        col += rows
    assert col == P_WIDTH


def _reorder_w_in(w_in_t, layer, tk=512):
    _, d_in, d = w_in_t.shape
    return pl.pallas_call(
        _reorder_w_in_body,
        grid=(d // tk,),
        in_specs=[pl.BlockSpec((None, d_in, tk), lambda i: (layer, 0, i))],
        out_specs=pl.BlockSpec((tk, P_WIDTH), lambda i: (i, 0)),
        out_shape=jax.ShapeDtypeStruct((d, P_WIDTH), BF16),
        compiler_params=_cparams(("parallel",)),
        name="reorder_w_in",
    )(w_in_t)


def _prep_mla_weights_body(wq_ref, wkv_ref, oq_ref, okv_ref):
    w = wq_ref[...]
    per_head = MLA_NOPE + MLA_ROPE
    pieces = []
    for h in range(MLA_HEADS):
        pe = w[:, h * per_head + MLA_NOPE:(h + 1) * per_head]
        pieces += [w[:, h * per_head:h * per_head + MLA_NOPE], pe, pe]
    oq_ref[...] = jnp.concatenate(pieces, axis=1).astype(BF16)
    okv_ref[...] = wkv_ref[...].astype(BF16)


def _prep_mla_weights(w_uq, w_ukv):
    depth, rq, nq = w_uq.shape
    _, rkv, nkv = w_ukv.shape
    return pl.pallas_call(
        _prep_mla_weights_body,
        grid=(depth,),
        in_specs=[pl.BlockSpec((None, rq, nq), lambda l: (l, 0, 0)),
                  pl.BlockSpec((None, rkv, nkv), lambda l: (l, 0, 0))],
        out_specs=[pl.BlockSpec((None, rq, MLA_HEADS * MLA_QK), lambda l: (l, 0, 0)),
                   pl.BlockSpec((None, rkv, nkv), lambda l: (l, 0, 0))],
        out_shape=[jax.ShapeDtypeStruct((depth, rq, MLA_HEADS * MLA_QK), BF16),
                   jax.ShapeDtypeStruct((depth, rkv, nkv), BF16)],
        compiler_params=_cparams(("parallel",)),
        name="prep_mla_weights",
    )(w_uq, w_ukv)


def _prepare(seq, attn_norm, mla_q_norm, mla_w_uq, mla_kv_norm, mla_w_ukv,
             ssd_conv_w, ssd_conv_b, ssd_dt_bias, ssd_a_log, ssd_d, ssd_norm,
             gla_w_gk, gla_b_gk, gla_norm, ffn_norm, ffn_dw_w, ffn_dw_b):
    depth = mla_w_uq.shape[0]
    wq, wkv = _prep_mla_weights(mla_w_uq, mla_w_ukv)

    inv_freq = ROPE_BASE ** (-jnp.arange(0, MLA_ROPE, 2, dtype=F32) / MLA_ROPE)
    ang = jnp.arange(seq, dtype=F32)[:, None] * inv_freq[None, :]
    cos, sin = jnp.cos(ang), jnp.sin(ang)
    zeros = jnp.zeros((seq, MLA_ROPE), F32)
    cc = jnp.concatenate([cos, cos, zeros], axis=1)
    ss = jnp.concatenate([-sin, sin, zeros], axis=1)

    pad_lanes = lambda a: jnp.pad(a, ((0, 0), (0, 128 - a.shape[1])))[:, None, :]
    wgk = jnp.zeros((depth, 128, GLA_KD), F32).at[:, SM_GLOW:SM_GLOW + GLA_GATE_RANK, :].set(gla_w_gk)

    rows = jnp.arange(GLA_SUB * GLA_KD)
    cols = jnp.arange(GLA_HEADS * GLA_CHUNK)
    seg = ((rows[:, None] % GLA_KD) // GLA_HEAD_K == cols[None, :] // GLA_CHUNK) & (
        rows[:, None] // GLA_KD == cols[None, :] % GLA_SUB)

    return dict(
        attn_norm=attn_norm[:, None, :],
        mla_q_norm=mla_q_norm[:, None, :], mla_kv_norm=mla_kv_norm[:, None, :], wq=wq, wkv=wkv,
        cc=cc, ss=ss,
        ssd_conv_w=ssd_conv_w, ssd_conv_b=ssd_conv_b[:, None, :],
        ssd_dtb=pad_lanes(ssd_dt_bias), ssd_dtb_t=ssd_dt_bias[:, :, None],
        ssd_alog=pad_lanes(ssd_a_log), ssd_alog_t=ssd_a_log[:, :, None],
        ssd_dexp=jnp.repeat(ssd_d, SSD_HEAD_DIM, axis=1)[:, None, :], ssd_norm=ssd_norm[:, None, :],
        gla_wgk=wgk.astype(BF16), gla_bgk=gla_b_gk[:, None, :], gla_norm=gla_norm[:, None, :],
        gla_seg=seg.astype(BF16),
        ffn_norm=ffn_norm[:, None, :], ffn_dw_w=ffn_dw_w, ffn_dw_b=ffn_dw_b[:, None, :],
    )


def kernel(x, attn_norm, w_in, mla_q_norm, mla_w_uq, mla_kv_norm, mla_w_ukv, ssd_conv_w, ssd_conv_b,
           ssd_dt_bias, ssd_a_log, ssd_d, ssd_norm, gla_w_gk, gla_b_gk, gla_norm, w_out, ffn_norm,
           ffn_w_gate, ffn_w_up, ffn_dw_w, ffn_dw_b, ffn_w_down, final_norm):
    batch, seq, d = x.shape
    prm = _prepare(seq, attn_norm, mla_q_norm, mla_w_uq, mla_kv_norm, mla_w_ukv,
                   ssd_conv_w, ssd_conv_b, ssd_dt_bias, ssd_a_log, ssd_d, ssd_norm,
                   gla_w_gk, gla_b_gk, gla_norm, ffn_norm, ffn_dw_w, ffn_dw_b)
    depth = w_in.shape[0]
    w_in_t = jnp.swapaxes(w_in, 1, 2)
    w_in16 = _reorder_w_in(w_in_t, 0)
    x2d = x.reshape(batch * seq, d)
    for layer in range(depth):
        proj, qt, k, vt, wg16, wu16 = _in_proj(x2d, prm, w_in16, layer, batch, seq,
                                               (ffn_w_gate, ffn_w_up))
        y_mla, wd16, wo16 = _mla_attn(qt, k, vt, layer, (ffn_w_down, w_out))
        y_mla = y_mla.reshape(batch * seq, MLA_HEADS * MLA_V)
        y_ssd = _ssd(proj, prm, layer, batch, seq)
        y_gla = _gla(proj, prm, layer, batch, seq)
        x2d, w_in16 = _out_proj(x2d, y_mla, y_ssd, y_gla, wo16, w_in_t,
                                layer + 1 if layer + 1 < depth else None)
        x2d = _ffn(x2d, prm, wg16, wu16, wd16, final_norm[None, :], layer, seq)
    return x2d.reshape(batch, seq, d)
```
